```python
import math
import jax, jax.numpy as jnp
from jax import lax
import numpy as np

D_MODEL = 2048
BATCH = 8
SEQ = 4096
DEPTH = 2

HEAD_DIM = 128
ATTN_SCALE = HEAD_DIM ** -0.5
ROPE_THETA = 10000.0
A_PATTERNS = ((128, 1), (512, 4), (2048, 16))
A_HEADS_PER_GROUP = 2
A_HEADS = A_HEADS_PER_GROUP * len(A_PATTERNS)
A_OUT = A_HEADS_PER_GROUP * HEAD_DIM
B_HEADS = 6
B_OUT = B_HEADS * HEAD_DIM
MOBA_BLOCK = 256
MOBA_TOPK = 3
MOBA_QUERY_CHUNK = 16
C_Q_HEADS = 8
C_KV_HEADS = 2
C_WINDOW = 128
C_OUT = C_Q_HEADS * HEAD_DIM
BAND_BLOCK = 128
N_Q_HEADS = A_HEADS + B_HEADS + C_Q_HEADS
N_KV_HEADS = A_HEADS + B_HEADS + C_KV_HEADS
N_BRANCHES = 3
N_GROUPS = 4
EXPERTS_PER_GROUP = 4
N_EXPERTS = N_GROUPS * EXPERTS_PER_GROUP
EXPERT_TOPK = 2
D_EXPERT = D_MODEL // 2
LN_EPS = 1e-5
DEEPNORM_ALPHA = (2 * DEPTH) ** 0.25
DEEPNORM_BETA = (8 * DEPTH) ** -0.25
NEG = -1e30

kernel_name = 'hybrid_dilated_moba_sinkswa_hmoe_deepnorm'


def _layer_norm(x, g, b):
    xf = x.astype(jnp.float32)
    mu = jnp.mean(xf, axis=-1, keepdims=True)
    var = jnp.mean(jnp.square(xf - mu), axis=-1, keepdims=True)
    return ((xf - mu) * lax.rsqrt(var + LN_EPS) * g + b).astype(x.dtype)


def _rope_tables(positions):
    inv = ROPE_THETA ** (-jnp.arange(0, HEAD_DIM, 2, dtype=jnp.float32) / HEAD_DIM)
    ang = positions.astype(jnp.float32)[..., None] * inv
    return jnp.cos(ang), jnp.sin(ang)


def _apply_rope(t, cos, sin):
    tf = t.astype(jnp.float32)
    t1, t2 = tf[..., :HEAD_DIM // 2], tf[..., HEAD_DIM // 2:]
    c, s = cos[:, :, None, :], sin[:, :, None, :]
    return jnp.concatenate([t1 * c - t2 * s, t2 * c + t1 * s], axis=-1).astype(t.dtype)


def _banded_attention(q, k, v, max_dist):
    B, L, Hk, G, hd = q.shape
    nb = L // BAND_BLOCK
    qb = q.reshape(B, nb, BAND_BLOCK, Hk, G, hd)

    def with_prev(t):
        t = t.reshape(B, nb, BAND_BLOCK, Hk, hd)
        prev = jnp.concatenate([jnp.zeros_like(t[:, :1]), t[:, :-1]], axis=1)
        return jnp.concatenate([prev, t], axis=2)

    kw, vw = with_prev(k), with_prev(v)
    s = jnp.einsum('bnqhgd,bnkhd->bnhgqk', qb, kw, preferred_element_type=jnp.float32) * ATTN_SCALE
    qi = jnp.arange(BAND_BLOCK)[:, None]
    kj = jnp.arange(2 * BAND_BLOCK)[None, :]
    dist = qi + BAND_BLOCK - kj
    blk = jnp.arange(nb)[:, None, None]
    valid = (dist >= 0) & (dist <= max_dist) & ((blk > 0) | (kj >= BAND_BLOCK))
    s = jnp.where(valid[None, :, None, None], s, NEG)
    lse = jax.nn.logsumexp(s, axis=-1)
    o = jnp.einsum('bnhgqk,bnkhd->bnqhgd', jnp.exp(s - lse[..., None]), vw)
    return o.reshape(B, L, Hk, G, hd), lse.transpose(0, 1, 4, 2, 3).reshape(B, L, Hk, G)


def _dilated_attention(q, k, v):
    B, S, _, hd = q.shape
    outs, lses = [], []
    for g, (window, dil) in enumerate(A_PATTERNS):
        hs = slice(g * A_HEADS_PER_GROUP, (g + 1) * A_HEADS_PER_GROUP)
        unit = dil * BAND_BLOCK
        s_pad = -(-S // unit) * unit
        L = s_pad // dil

        def by_residue(t):
            t = jnp.pad(t[:, :, hs], ((0, 0), (0, s_pad - S), (0, 0), (0, 0)))
            t = t.reshape(B, L, dil, A_HEADS_PER_GROUP, hd).transpose(0, 2, 1, 3, 4)
            return t.reshape(B * dil, L, A_HEADS_PER_GROUP, hd)

        o, lse = _banded_attention(by_residue(q)[:, :, :, None], by_residue(k), by_residue(v), window // dil)
        o = o[:, :, :, 0].reshape(B, dil, L, A_HEADS_PER_GROUP, hd).transpose(0, 2, 1, 3, 4)
        lse = lse[..., 0].reshape(B, dil, L, A_HEADS_PER_GROUP).transpose(0, 2, 1, 3)
        outs.append(o.reshape(B, s_pad, A_HEADS_PER_GROUP, hd)[:, :S])
        lses.append(lse.reshape(B, s_pad, A_HEADS_PER_GROUP)[:, :S])
    w = jax.nn.softmax(jnp.stack(lses), axis=0)
    o = jnp.einsum('gbsh,gbshd->bshd', w, jnp.stack(outs))
    return o.reshape(B, S, A_OUT)


def _moba_attention(q, k, v):
    B, S, H, hd = q.shape
    s_pad = -(-S // MOBA_BLOCK) * MOBA_BLOCK
    pad = ((0, 0), (0, s_pad - S), (0, 0), (0, 0))
    q, k, v = jnp.pad(q, pad), jnp.pad(k, pad), jnp.pad(v, pad)
    nblk = s_pad // MOBA_BLOCK
    qb = q.reshape(B, nblk, MOBA_BLOCK, H, hd)
    kb = k.reshape(B, nblk, MOBA_BLOCK, H, hd)
    vb = v.reshape(B, nblk, MOBA_BLOCK, H, hd)
    s = jnp.einsum('bnqhd,bnkhd->bnhqk', qb, kb, preferred_element_type=jnp.float32) * ATTN_SCALE
    causal = jnp.tril(jnp.ones((MOBA_BLOCK, MOBA_BLOCK), dtype=bool))
    s = jnp.where(causal, s, NEG)
    lse_own = jax.nn.logsumexp(s, axis=-1)
    o_own = jnp.einsum('bnhqk,bnkhd->bnqhd', jnp.exp(s - lse_own[..., None]), vb).reshape(B, s_pad, H, hd)
    lse_own = lse_own.transpose(0, 1, 3, 2).reshape(B, s_pad, H)
    k_mean = jnp.mean(kb.astype(jnp.float32), axis=2)
    gate = jnp.einsum('bshd,bnhd->bshn', q, k_mean, preferred_element_type=jnp.float32)
    n_past = jnp.arange(s_pad) // MOBA_BLOCK
    is_past = jnp.arange(nblk)[None, :] < n_past[:, None]
    gate = jnp.where(is_past[None, :, None, :], gate, NEG)
    n_sel = min(MOBA_TOPK, nblk)
    _, sel = lax.top_k(gate, n_sel)
    sel_ok = jnp.arange(n_sel)[None, :] < n_past[:, None]
    kbt = kb.transpose(0, 3, 1, 2, 4)
    vbt = vb.transpose(0, 3, 1, 2, 4)
    n_chunks = s_pad // MOBA_QUERY_CHUNK
    b_idx = jnp.arange(B)[:, None, None, None]
    h_idx = jnp.arange(H)[None, None, :, None]

    def to_chunks(t):
        return t.reshape((B, n_chunks, MOBA_QUERY_CHUNK) + t.shape[2:]).swapaxes(0, 1)

    def attend_selected(args):
        qc, selc, okc = args
        kg = kbt[b_idx, h_idx, selc]
        vg = vbt[b_idx, h_idx, selc]
        sc = jnp.einsum('bqhd,bqhrkd->bqhrk', qc, kg, preferred_element_type=jnp.float32) * ATTN_SCALE
        sc = jnp.where(okc[None, :, None, :, None], sc, NEG)
        lse = jax.nn.logsumexp(sc, axis=(-2, -1))
        o = jnp.einsum('bqhrk,bqhrkd->bqhd', jnp.exp(sc - lse[..., None, None]), vg)
        return o, lse

    o_sel, lse_sel = lax.map(attend_selected, (to_chunks(q), to_chunks(sel), sel_ok.reshape(n_chunks, MOBA_QUERY_CHUNK, n_sel)))
    o_sel = o_sel.swapaxes(0, 1).reshape(B, s_pad, H, hd)
    lse_sel = lse_sel.swapaxes(0, 1).reshape(B, s_pad, H)
    m = jnp.maximum(lse_own, lse_sel)
    w_own = jnp.exp(lse_own - m)[..., None]
    w_sel = jnp.exp(lse_sel - m)[..., None]
    o = (w_own * o_own + w_sel * o_sel) / (w_own + w_sel)
    return o[:, :S].reshape(B, S, B_OUT)


def _sink_window_attention(q, k, v, sinks):
    B, S, _, hd = q.shape
    grp = C_Q_HEADS // C_KV_HEADS
    o, lse = _banded_attention(q.reshape(B, S, C_KV_HEADS, grp, hd), k, v, C_WINDOW - 1)
    keep = jax.nn.sigmoid(lse - sinks.reshape(C_KV_HEADS, grp))
    return (o * keep[..., None]).reshape(B, S, C_OUT)


def _mixer(h, cos, sin, w_in, w_gate, b_gate, w_branch_a, w_branch_b, w_branch_c, w_out, sinks):
    B, S, D = h.shape
    proj = h @ w_in
    nq, nkv = N_Q_HEADS * HEAD_DIM, N_KV_HEADS * HEAD_DIM
    q = _apply_rope(proj[..., :nq].reshape(B, S, N_Q_HEADS, HEAD_DIM), cos, sin)
    k = _apply_rope(proj[..., nq:nq + nkv].reshape(B, S, N_KV_HEADS, HEAD_DIM), cos, sin)
    v = proj[..., nq + nkv:].reshape(B, S, N_KV_HEADS, HEAD_DIM)
    a0, b0 = A_HEADS, A_HEADS + B_HEADS
    out_a = _dilated_attention(q[:, :, :a0], k[:, :, :a0], v[:, :, :a0]).astype(h.dtype)
    out_b = _moba_attention(q[:, :, a0:b0], k[:, :, a0:b0], v[:, :, a0:b0]).astype(h.dtype)
    out_c = _sink_window_attention(q[:, :, b0:], k[:, :, b0:], v[:, :, b0:], sinks).astype(h.dtype)
    gates = jax.nn.sigmoid((h @ w_gate + b_gate).astype(jnp.float32)).reshape(B, S, N_BRANCHES, D)
    merged = (gates[:, :, 0] * (out_a @ w_branch_a) + gates[:, :, 1] * (out_b @ w_branch_b)
              + gates[:, :, 2] * (out_c @ w_branch_c)).astype(h.dtype)
    return merged @ w_out


def _hier_moe(h, w_router_group, w_router_expert, w_exp_gate, w_exp_up, w_exp_down):
    B, S, D = h.shape
    t = h.reshape(B * S, D)
    g_prob = jax.nn.softmax((t @ w_router_group).astype(jnp.float32), axis=-1)
    g_val, g_sel = lax.top_k(g_prob, 1)
    e_logit = (t @ w_router_expert).astype(jnp.float32).reshape(-1, N_GROUPS, EXPERTS_PER_GROUP)
    e_logit = jnp.take_along_axis(e_logit, g_sel[:, :, None], axis=1)[:, 0]
    e_val, e_sel = lax.top_k(e_logit, EXPERT_TOPK)
    w = jax.nn.softmax(e_val, axis=-1) * g_val
    expert_id = g_sel * EXPERTS_PER_GROUP + e_sel
    comb = jnp.einsum('tk,tke->te', w, jax.nn.one_hot(expert_id, N_EXPERTS, dtype=jnp.float32))
    out = jnp.zeros((B * S, D), jnp.float32)
    for e in range(N_EXPERTS):
        hid = jax.nn.silu(t @ w_exp_gate[e]) * (t @ w_exp_up[e])
        out = out + comb[:, e:e + 1] * (hid @ w_exp_down[e])
    return out.reshape(B, S, D).astype(h.dtype)


def setup_inputs(seed: int = 0) -> dict:
    key = jax.random.key(seed)
    ks = jax.random.split(key, 20)
    f32 = jnp.float32
    d = D_MODEL

    def nrm(k, shape, scale):
        return jax.random.normal(k, shape, f32) * scale

    qk_cols = (N_Q_HEADS + N_KV_HEADS) * HEAD_DIM
    v_cols = N_KV_HEADS * HEAD_DIM
    x = nrm(ks[0], (BATCH, SEQ, d), 1.0)
    offsets = jax.random.randint(ks[1], (BATCH, 1), 0, 1024, dtype=jnp.int32)
    positions = offsets + jnp.arange(SEQ, dtype=jnp.int32)[None, :]
    w_in = jnp.concatenate([nrm(ks[2], (DEPTH, d, qk_cols), d ** -0.5),
                            nrm(ks[3], (DEPTH, d, v_cols), DEEPNORM_BETA * d ** -0.5)], axis=-1)
    w_gate = nrm(ks[4], (DEPTH, d, N_BRANCHES * d), d ** -0.5)
    b_gate = nrm(ks[5], (DEPTH, N_BRANCHES * d), 0.02)
    w_branch_a = nrm(ks[6], (DEPTH, A_OUT, d), A_OUT ** -0.5)
    w_branch_b = nrm(ks[7], (DEPTH, B_OUT, d), B_OUT ** -0.5)
    w_branch_c = nrm(ks[8], (DEPTH, C_OUT, d), C_OUT ** -0.5)
    w_out = nrm(ks[9], (DEPTH, d, d), DEEPNORM_BETA * d ** -0.5)
    sinks = 4.0 + nrm(ks[10], (DEPTH, C_Q_HEADS), 1.0)
    ln1_g = 1.0 + nrm(ks[11], (DEPTH, d), 0.02)
    ln1_b = nrm(ks[12], (DEPTH, d), 0.02)
    w_router_group = nrm(ks[13], (DEPTH, d, N_GROUPS), d ** -0.5)
    w_router_expert = nrm(ks[14], (DEPTH, d, N_EXPERTS), d ** -0.5)
    w_exp_gate = nrm(ks[15], (DEPTH, N_EXPERTS, d, D_EXPERT), DEEPNORM_BETA * d ** -0.5)
    w_exp_up = nrm(ks[16], (DEPTH, N_EXPERTS, d, D_EXPERT), DEEPNORM_BETA * d ** -0.5)
    w_exp_down = nrm(ks[17], (DEPTH, N_EXPERTS, D_EXPERT, d), DEEPNORM_BETA * D_EXPERT ** -0.5)
    ln2_g = 1.0 + nrm(ks[18], (DEPTH, d), 0.02)
    ln2_b = nrm(ks[19], (DEPTH, d), 0.02)
    return {'x': x, 'positions': positions, 'w_in': w_in, 'w_gate': w_gate, 'b_gate': b_gate,
            'w_branch_a': w_branch_a, 'w_branch_b': w_branch_b, 'w_branch_c': w_branch_c,
            'w_out': w_out, 'sinks': sinks, 'ln1_g': ln1_g, 'ln1_b': ln1_b,
            'w_router_group': w_router_group, 'w_router_expert': w_router_expert,
            'w_exp_gate': w_exp_gate, 'w_exp_up': w_exp_up, 'w_exp_down': w_exp_down,
            'ln2_g': ln2_g, 'ln2_b': ln2_b}


def reference(x, positions, w_in, w_gate, b_gate, w_branch_a, w_branch_b, w_branch_c, w_out, sinks,
              ln1_g, ln1_b, w_router_group, w_router_expert, w_exp_gate, w_exp_up, w_exp_down,
              ln2_g, ln2_b):
    cos, sin = _rope_tables(positions)
    h = x
    for l in range(DEPTH):
        mix = _mixer(h, cos, sin, w_in[l], w_gate[l], b_gate[l], w_branch_a[l], w_branch_b[l],
                     w_branch_c[l], w_out[l], sinks[l])
        h = _layer_norm(DEEPNORM_ALPHA * h + mix, ln1_g[l], ln1_b[l])
        ffn = _hier_moe(h, w_router_group[l], w_router_expert[l], w_exp_gate[l], w_exp_up[l], w_exp_down[l])
        h = _layer_norm(DEEPNORM_ALPHA * h + ffn, ln2_g[l], ln2_b[l])
    return h
```

```python
import functools
import math

import jax
import jax.numpy as jnp
from jax import lax
from jax.experimental import pallas as pl
from jax.experimental.pallas import tpu as pltpu

F32 = jnp.float32
BF16 = jnp.bfloat16

HEAD_DIM = 128
ATTN_SCALE = HEAD_DIM ** -0.5
ROPE_THETA = 10000.0
A_PATTERNS = ((128, 1), (512, 4), (2048, 16))
A_HEADS_PER_GROUP = 2
A_HEADS = A_HEADS_PER_GROUP * len(A_PATTERNS)
B_HEADS = 6
MOBA_BLOCK = 256
MOBA_TOPK = 3
C_Q_HEADS = 8
C_KV_HEADS = 2
C_GROUP = C_Q_HEADS // C_KV_HEADS
C_WINDOW = 128
BAND = 128
N_Q_HEADS = A_HEADS + B_HEADS + C_Q_HEADS
N_KV_HEADS = A_HEADS + B_HEADS + C_KV_HEADS
N_HEADS_ALL = N_Q_HEADS + 2 * N_KV_HEADS
N_GROUPS = 4
EXPERTS_PER_GROUP = 4
N_EXPERTS = N_GROUPS * EXPERTS_PER_GROUP
LN_EPS = 1e-5
NEG = -1e30

V7X_LANES = 128
V7X_VMEM_BYTES = 64 * 1024 * 1024
VMEM_LIMIT = 56 * 1024 * 1024


def _cparams(sem, vmem=VMEM_LIMIT):
    return pltpu.CompilerParams(dimension_semantics=sem, vmem_limit_bytes=vmem)


def _rope_table_kernel(pos_ref, inv_ref, cc_ref, ss_ref):
    ang = pos_ref[...].astype(F32) * inv_ref[...]
    lane = lax.broadcasted_iota(jnp.int32, ang.shape, 1)
    s = jnp.sin(ang)
    cc_ref[...] = jnp.cos(ang)
    ss_ref[...] = jnp.where(lane < HEAD_DIM // 2, -s, s)


def _rope_tables(positions):
    t = positions.size
    tm = 2048
    inv = ROPE_THETA ** (-jnp.arange(0, HEAD_DIM, 2, dtype=F32) / HEAD_DIM)
    inv2 = jnp.concatenate([inv, inv]).reshape(1, HEAD_DIM)
    pos = positions.reshape(t, 1)
    return pl.pallas_call(
        _rope_table_kernel,
        out_shape=(jax.ShapeDtypeStruct((t, HEAD_DIM), F32),) * 2,
        grid=(t // tm,),
        in_specs=[pl.BlockSpec((tm, 1), lambda i: (i, 0)),
                  pl.BlockSpec((1, HEAD_DIM), lambda i: (0, 0))],
        out_specs=(pl.BlockSpec((tm, HEAD_DIM), lambda i: (i, 0)),) * 2,
        compiler_params=_cparams(("arbitrary",)),
        name="rope_tables",
    )(pos, inv2)


def _qkv_kernel(x_ref, w_ref, cc_ref, ss_ref, o_ref, *, heads_per_tile):
    j = pl.program_id(1)
    acc = jnp.dot(x_ref[...], w_ref[...], preferred_element_type=F32)
    for hh in range(heads_per_tile):
        head = j * heads_per_tile + hh
        t = acc[:, hh * HEAD_DIM:(hh + 1) * HEAD_DIM]

        @pl.when(head < N_Q_HEADS + N_KV_HEADS)
        def _():
            r = t * cc_ref[...] + pltpu.roll(t, HEAD_DIM // 2, 1) * ss_ref[...]
            sc = jnp.where(head < N_Q_HEADS, ATTN_SCALE, 1.0).astype(F32)
            o_ref[hh] = (r * sc).astype(BF16)

        @pl.when(head >= N_Q_HEADS + N_KV_HEADS)
        def _():
            o_ref[hh] = t.astype(BF16)


def _qkv_proj(hb, w_in_b, cc, ss):
    t, d = hb.shape
    n = w_in_b.shape[1]
    tm, tn = 1024, 512
    hpt = tn // HEAD_DIM
    return pl.pallas_call(
        functools.partial(_qkv_kernel, heads_per_tile=hpt),
        out_shape=jax.ShapeDtypeStruct((n // HEAD_DIM, t, HEAD_DIM), BF16),
        grid=(t // tm, n // tn),
        in_specs=[pl.BlockSpec((tm, d), lambda i, j: (i, 0)),
                  pl.BlockSpec((d, tn), lambda i, j: (0, j)),
                  pl.BlockSpec((tm, HEAD_DIM), lambda i, j: (i, 0)),
                  pl.BlockSpec((tm, HEAD_DIM), lambda i, j: (i, 0))],
        out_specs=pl.BlockSpec((hpt, tm, HEAD_DIM), lambda i, j: (j, i, 0)),
        compiler_params=_cparams(("arbitrary", "arbitrary")),
        name="qkv_rope",
    )(hb, w_in_b, cc, ss)


def _band_bias(max_dist):
    qi = lax.broadcasted_iota(jnp.int32, (BAND, 2 * BAND), 0)
    kj = lax.broadcasted_iota(jnp.int32, (BAND, 2 * BAND), 1)
    dist = qi + BAND - kj
    valid = (dist >= 0) & (dist <= max_dist)
    bias = jnp.where(valid, 0.0, NEG).astype(F32)
    bias_first = jnp.where(valid & (kj >= BAND), 0.0, NEG).astype(F32)
    return bias, bias_first


def _band_block(qb, kwin, vwin, bias):
    g = qb.shape[0] // BAND
    s = lax.dot_general(qb, kwin, (((1,), (1,)), ((), ())), preferred_element_type=F32)
    s = s.reshape(g, BAND, 2 * BAND) + bias[None]
    m = jnp.max(s, axis=-1, keepdims=True)
    p = jnp.exp(s - m)
    l = jnp.sum(p, axis=-1, keepdims=True)
    o = jnp.dot(p.reshape(g * BAND, 2 * BAND).astype(BF16), vwin, preferred_element_type=F32)
    o = o.reshape(g, BAND, HEAD_DIM) / l
    return o, m + jnp.log(l)


def _swa_kernel(sink_ref, q_ref, k_ref, kh_ref, v_ref, vh_ref, o_ref, *, tq):
    kv = pl.program_id(1)
    t = pl.program_id(2)
    bias, bias_first = _band_bias(C_WINDOW - 1)
    for blk in range(tq // BAND):
        qb = q_ref[:, blk * BAND:(blk + 1) * BAND, :].reshape(C_GROUP * BAND, HEAD_DIM)
        if blk == 0:
            kwin = jnp.concatenate([kh_ref[0], k_ref[0, :BAND]], axis=0)
            vwin = jnp.concatenate([vh_ref[0], v_ref[0, :BAND]], axis=0)
            b = jnp.where(t == 0, bias_first, bias)
        else:
            kwin = k_ref[0, (blk - 1) * BAND:(blk + 1) * BAND]
            vwin = v_ref[0, (blk - 1) * BAND:(blk + 1) * BAND]
            b = bias
        o, lse = _band_block(qb, kwin, vwin, b)
        for g in range(C_GROUP):
            keep = jax.nn.sigmoid(lse[g] - sink_ref[kv * C_GROUP + g])
            o_ref[blk * BAND:(blk + 1) * BAND, g * HEAD_DIM:(g + 1) * HEAD_DIM] = (o[g] * keep).astype(BF16)


def _swa_attention(qkv, sinks, batch, seq):
    tq = 512
    nt = seq // tq
    q0 = A_HEADS + B_HEADS
    k0 = N_Q_HEADS + A_HEADS + B_HEADS
    v0 = N_Q_HEADS + N_KV_HEADS + A_HEADS + B_HEADS
    rb = tq // BAND

    def halo(b, kv, t):
        return jnp.maximum(b * (seq // BAND) + t * rb - 1, 0)

    return pl.pallas_call(
        functools.partial(_swa_kernel, tq=tq),
        out_shape=jax.ShapeDtypeStruct((batch * seq, C_Q_HEADS * HEAD_DIM), BF16),
        grid=(batch, C_KV_HEADS, nt),
        in_specs=[
            pl.BlockSpec(memory_space=pltpu.SMEM),
            pl.BlockSpec((C_GROUP, tq, HEAD_DIM), lambda b, kv, t: (q0 // C_GROUP + kv, b * nt + t, 0)),
            pl.BlockSpec((1, tq, HEAD_DIM), lambda b, kv, t: (k0 + kv, b * nt + t, 0)),
            pl.BlockSpec((1, BAND, HEAD_DIM), lambda b, kv, t: (k0 + kv, halo(b, kv, t), 0)),
            pl.BlockSpec((1, tq, HEAD_DIM), lambda b, kv, t: (v0 + kv, b * nt + t, 0)),
            pl.BlockSpec((1, BAND, HEAD_DIM), lambda b, kv, t: (v0 + kv, halo(b, kv, t), 0)),
        ],
        out_specs=pl.BlockSpec((tq, C_GROUP * HEAD_DIM), lambda b, kv, t: (b * nt + t, kv)),
        compiler_params=_cparams(("arbitrary",) * 3),
        name="swa_sink",
    )(sinks, qkv, qkv, qkv, qkv, qkv)


def _dilated_kernel(*refs, seq):
    ng = len(A_PATTERNS)
    in_refs = refs[:3 * ng]
    o_ref = refs[3 * ng]
    qf, kf, vf, og, lg = refs[3 * ng + 1:]
    for g, (window, dil) in enumerate(A_PATTERNS):
        q_ref, k_ref, v_ref = in_refs[3 * g:3 * g + 3]
        qf[...] = q_ref[0].astype(F32)
        kf[...] = k_ref[0].astype(F32)
        vf[...] = v_ref[0].astype(F32)
        nblk = seq // dil // BAND
        bias, bias_first = _band_bias(window // dil)

        def block(idx, carry, g=g, dil=dil, nblk=nblk, bias=bias, bias_first=bias_first):
            r = idx // nblk
            n = idx % nblk
            own = pl.ds(r + dil * BAND * n, BAND, stride=dil)
            prev = pl.ds(r + dil * BAND * jnp.maximum(n - 1, 0), BAND, stride=dil)
            qb = qf[own, :].astype(BF16)
            kwin = jnp.concatenate([kf[prev, :], kf[own, :]], axis=0).astype(BF16)
            vwin = jnp.concatenate([vf[prev, :], vf[own, :]], axis=0).astype(BF16)
            o, lse = _band_block(qb, kwin, vwin, jnp.where(n == 0, bias_first, bias))
            og[g, own, :] = o[0]
            lg[g, own, :] = jnp.broadcast_to(lse[0], (BAND, HEAD_DIM))
            return carry

        lax.fori_loop(0, dil * nblk, block, 0, unroll=2)

    rows = 256

    def combine(c, carry):
        sl = pl.ds(pl.multiple_of(c * rows, rows), rows)
        ls = [lg[g, sl, :] for g in range(ng)]
        m = functools.reduce(jnp.maximum, ls)
        es = [jnp.exp(l - m) for l in ls]
        num = functools.reduce(jnp.add, [e * og[g, sl, :] for g, e in enumerate(es)])
        o_ref[sl, :] = (num / functools.reduce(jnp.add, es)).astype(BF16)
        return carry

    lax.fori_loop(0, seq // rows, combine, 0)


def _dilated_attention(qkv, batch, seq):
    ng = len(A_PATTERNS)
    k0 = N_Q_HEADS
    v0 = N_Q_HEADS + N_KV_HEADS
    in_specs = []
    for g in range(ng):
        for base in (0, k0, v0):
            in_specs.append(pl.BlockSpec(
                (1, seq, HEAD_DIM),
                lambda b, j, h=base + g * A_HEADS_PER_GROUP: (h + j, b, 0)))
    return pl.pallas_call(
        functools.partial(_dilated_kernel, seq=seq),
        out_shape=jax.ShapeDtypeStruct((batch * seq, A_HEADS_PER_GROUP * HEAD_DIM), BF16),
        grid=(batch, A_HEADS_PER_GROUP),
        in_specs=in_specs,
        out_specs=pl.BlockSpec((seq, HEAD_DIM), lambda b, j: (b, j)),
        scratch_shapes=[pltpu.VMEM((seq, HEAD_DIM), F32)] * 3
        + [pltpu.VMEM((ng, seq, HEAD_DIM), F32)] * 2,
        compiler_params=_cparams(("arbitrary",) * 2),
        name="dilated_attn",
    )(*([qkv] * (3 * ng)))


def _moba_kernel(q_ref, k_ref, v_ref, o_ref, kaug, kmean, *, seq):
    i = pl.program_id(2)
    nblk = seq // MOBA_BLOCK
    blk = MOBA_BLOCK

    @pl.when(i == 0)
    def _():
        kaug[:, :HEAD_DIM] = k_ref[0]
        row = lax.broadcasted_iota(jnp.int32, (seq, HEAD_DIM), 0)
        lane = lax.broadcasted_iota(jnp.int32, (seq, HEAD_DIM), 1)
        kaug[:, HEAD_DIM:] = jnp.where(row // blk == lane, 1.0, 0.0).astype(BF16)
        kmean[...] = jnp.zeros_like(kmean)
        for n in range(nblk):
            kb = k_ref[0, n * blk:(n + 1) * blk, :].astype(F32)
            kmean[n:n + 1, :] = jnp.mean(kb, axis=0, keepdims=True)

    q = q_ref[0]
    km = kmean[...]
    km_hi = km.astype(BF16)
    km_lo = (km - km_hi.astype(F32)).astype(BF16)
    nt = (((1,), (1,)), ((), ()))
    gate = (lax.dot_general(q, km_hi, nt, preferred_element_type=F32)
            + lax.dot_general(q, km_lo, nt, preferred_element_type=F32))
    lane = lax.broadcasted_iota(jnp.int32, (blk, HEAD_DIM), 1)
    g = jnp.where(lane < i, gate, NEG)
    sel = jnp.zeros((blk, HEAD_DIM), jnp.bool_)
    for _ in range(MOBA_TOPK):
        mx = jnp.max(g, axis=1, keepdims=True)
        first = jnp.min(jnp.where(g == mx, lane, HEAD_DIM), axis=1, keepdims=True)
        pick = lane == first
        sel = sel | pick
        g = jnp.where(pick, NEG, g)
    visible = (sel & (lane < i)) | (lane == i)
    qaug = jnp.concatenate([q, jnp.where(visible, 0.0, NEG).astype(BF16)], axis=1)

    ri = lax.broadcasted_iota(jnp.int32, (blk, blk), 0)
    ci = lax.broadcasted_iota(jnp.int32, (blk, blk), 1)
    causal = ri >= ci

    def body(j, carry):
        m, l, acc = carry
        rows = pl.ds(pl.multiple_of(j * blk, blk), blk)
        s = lax.dot_general(qaug, kaug[rows, :], (((1,), (1,)), ((), ())), preferred_element_type=F32)
        s = jnp.where(causal | (j < i), s, NEG)
        m_new = jnp.maximum(m, jnp.max(s, axis=1, keepdims=True))
        alpha = jnp.exp(m - m_new)
        p = jnp.exp(s - m_new)
        l = alpha * l + jnp.sum(p, axis=1, keepdims=True)
        acc = alpha * acc + jnp.dot(p.astype(BF16), v_ref[0, rows, :], preferred_element_type=F32)
        return m_new, l, acc

    init = (jnp.full((blk, 1), NEG, F32), jnp.zeros((blk, 1), F32), jnp.zeros((blk, HEAD_DIM), F32))
    m, l, acc = lax.fori_loop(0, i + 1, body, init)
    o_ref[...] = (acc / l).astype(BF16)


def _moba_attention(qkv, batch, seq):
    nblk = seq // MOBA_BLOCK
    q0 = A_HEADS
    k0 = N_Q_HEADS + A_HEADS
    v0 = N_Q_HEADS + N_KV_HEADS + A_HEADS
    return pl.pallas_call(
        functools.partial(_moba_kernel, seq=seq),
        out_shape=jax.ShapeDtypeStruct((batch * seq, B_HEADS * HEAD_DIM), BF16),
        grid=(batch, B_HEADS, nblk),
        in_specs=[
            pl.BlockSpec((1, MOBA_BLOCK, HEAD_DIM), lambda b, h, i: (q0 + h, b * nblk + i, 0)),
            pl.BlockSpec((1, seq, HEAD_DIM), lambda b, h, i: (k0 + h, b, 0)),
            pl.BlockSpec((1, seq, HEAD_DIM), lambda b, h, i: (v0 + h, b, 0)),
        ],
        out_specs=pl.BlockSpec((MOBA_BLOCK, HEAD_DIM), lambda b, h, i: (b * nblk + i, h)),
        scratch_shapes=[pltpu.VMEM((seq, 2 * HEAD_DIM), BF16), pltpu.VMEM((HEAD_DIM, HEAD_DIM), F32)],
        compiler_params=_cparams(("arbitrary",) * 3),
        name="moba_attn",
    )(qkv, qkv, qkv)


def _merge_kernel(h_ref, oa_ref, ob_ref, oc_ref, g0_ref, g1_ref, g2_ref, b0_ref, b1_ref, b2_ref,
                  wa_ref, wb_ref, wc_ref, o_ref):
    x = h_ref[...]
    acc = None
    for g_ref, b_ref, br_ref, w_ref in ((g0_ref, b0_ref, oa_ref, wa_ref),
                                        (g1_ref, b1_ref, ob_ref, wb_ref),
                                        (g2_ref, b2_ref, oc_ref, wc_ref)):
        gate = jax.nn.sigmoid(jnp.dot(x, g_ref[...], preferred_element_type=F32) + b_ref[...])
        term = gate * jnp.dot(br_ref[...], w_ref[...], preferred_element_type=F32)
        acc = term if acc is None else acc + term
    o_ref[...] = acc.astype(BF16)


def _merge(hb, out_a, out_b, out_c, w_gate_b, b_gate, wa_b, wb_b, wc_b):
    t, d = hb.shape
    tm, tn = 1024, 512
    nj = d // tn
    row = lambda w: pl.BlockSpec((tm, w), lambda j, i: (i, 0))
    gate_w = lambda x: pl.BlockSpec((d, tn), lambda j, i, x=x: (0, x * nj + j))
    gate_b = lambda x: pl.BlockSpec((1, tn), lambda j, i, x=x: (0, x * nj + j))
    br_w = lambda k: pl.BlockSpec((k, tn), lambda j, i: (0, j))
    bg = b_gate.reshape(1, -1)
    return pl.pallas_call(
        _merge_kernel,
        out_shape=jax.ShapeDtypeStruct((t, d), BF16),
        grid=(nj, t // tm),
        in_specs=[row(d), row(out_a.shape[1]), row(out_b.shape[1]), row(out_c.shape[1]),
                  gate_w(0), gate_w(1), gate_w(2), gate_b(0), gate_b(1), gate_b(2),
                  br_w(wa_b.shape[0]), br_w(wb_b.shape[0]), br_w(wc_b.shape[0])],
        out_specs=pl.BlockSpec((tm, tn), lambda j, i: (i, j)),
        compiler_params=_cparams(("arbitrary",) * 2),
        name="branch_merge",
    )(hb, out_a, out_b, out_c, w_gate_b, w_gate_b, w_gate_b, bg, bg, bg, wa_b, wb_b, wc_b)


def _layer_norm_rows(y, g, b):
    mu = jnp.mean(y, axis=-1, keepdims=True)
    yc = y - mu
    var = jnp.mean(yc * yc, axis=-1, keepdims=True)
    return yc * lax.rsqrt(var + LN_EPS) * g + b


def _route(lt):
    tm = lt.shape[1]
    gl = lt[0:N_GROUPS]
    gmax = jnp.max(gl, axis=0, keepdims=True)
    ge = jnp.exp(gl - gmax)
    gp = ge / jnp.sum(ge, axis=0, keepdims=True)
    g_val = jnp.max(gp, axis=0, keepdims=True)
    row = lax.broadcasted_iota(jnp.int32, (N_GROUPS, tm), 0)
    g_sel = jnp.min(jnp.where(gp == g_val, row, N_GROUPS), axis=0, keepdims=True)
    el = jnp.zeros((EXPERTS_PER_GROUP, tm), F32)
    for g in range(N_GROUPS):
        lo = 8 + g * EXPERTS_PER_GROUP
        el = jnp.where(g_sel == g, lt[lo:lo + EXPERTS_PER_GROUP], el)
    v0 = jnp.max(el, axis=0, keepdims=True)
    i0 = jnp.min(jnp.where(el == v0, row, EXPERTS_PER_GROUP), axis=0, keepdims=True)
    el1 = jnp.where(row == i0, -jnp.inf, el)
    v1 = jnp.max(el1, axis=0, keepdims=True)
    i1 = jnp.min(jnp.where(el1 == v1, row, EXPERTS_PER_GROUP), axis=0, keepdims=True)
    e1 = jnp.exp(v1 - v0)
    den = 1.0 + e1
    w0 = 1.0 / den * g_val
    w1 = e1 / den * g_val
    id0 = (g_sel * EXPERTS_PER_GROUP + i0).astype(F32)
    id1 = (g_sel * EXPERTS_PER_GROUP + i1).astype(F32)
    return id0, id1, w0, w1


def _outproj_ln_route_kernel(m_ref, w_ref, h_ref, g_ref, b_ref, wr_ref, hf_ref, hb_ref, rt_ref, *, alpha):
    mix = jnp.dot(m_ref[...], w_ref[...], preferred_element_type=F32)
    h1 = _layer_norm_rows(alpha * h_ref[...] + mix, g_ref[...], b_ref[...])
    hf_ref[...] = h1
    hb_ref[...] = h1.astype(BF16)
    lt = lax.dot_general(wr_ref[...], h1, (((1,), (1,)), ((), ())),
                         precision=lax.Precision.HIGHEST, preferred_element_type=F32)
    id0, id1, w0, w1 = _route(lt)
    rt_ref[...] = jnp.concatenate([id0, id1, w0, w1, jnp.zeros((4, lt.shape[1]), F32)], axis=0)


def _outproj_ln_route(merged, w_out_b, h, ln_g, ln_b, w_router_t, alpha):
    t, d = h.shape
    tm = 512
    return pl.pallas_call(
        functools.partial(_outproj_ln_route_kernel, alpha=alpha),
        out_shape=(jax.ShapeDtypeStruct((t, d), F32), jax.ShapeDtypeStruct((t, d), BF16),
                   jax.ShapeDtypeStruct((8, t), F32)),
        grid=(t // tm,),
        in_specs=[pl.BlockSpec((tm, d), lambda i: (i, 0)),
                  pl.BlockSpec((d, d), lambda i: (0, 0)),
                  pl.BlockSpec((tm, d), lambda i: (i, 0)),
                  pl.BlockSpec((1, d), lambda i: (0, 0)),
                  pl.BlockSpec((1, d), lambda i: (0, 0)),
                  pl.BlockSpec((32, d), lambda i: (0, 0))],
        out_specs=(pl.BlockSpec((tm, d), lambda i: (i, 0)),
                   pl.BlockSpec((tm, d), lambda i: (i, 0)),
                   pl.BlockSpec((8, tm), lambda i: (0, i))),
        compiler_params=_cparams(("arbitrary",)),
        name="outproj_ln1_route",
    )(merged, w_out_b, h, ln_g.reshape(1, d), ln_b.reshape(1, d), w_router_t)


MOE_TILE = 256
GATHER_TILE = 256


def _dispatch_plan(rt, t):
    ids = rt[0:2].astype(jnp.int32).reshape(-1)
    wts = rt[2:4].reshape(-1)
    tok = jnp.tile(jnp.arange(t, dtype=jnp.int32), 2)
    onehot = (ids[:, None] == jnp.arange(N_EXPERTS, dtype=jnp.int32)[None, :]).astype(jnp.int32)
    csum = jnp.cumsum(onehot, axis=0)
    counts = csum[-1]
    rank = jnp.take_along_axis(csum, ids[:, None], axis=1)[:, 0] - 1
    padded = (counts + MOE_TILE - 1) // MOE_TILE * MOE_TILE
    ends = jnp.cumsum(padded)
    pos = (ends - padded)[ids] + rank
    p_rows = 2 * t + N_EXPERTS * MOE_TILE
    src = jnp.zeros((p_rows,), jnp.int32).at[pos].set(tok)
    w_sorted = jnp.zeros((p_rows,), F32).at[pos].set(wts)
    n_tiles = p_rows // MOE_TILE
    n_used = (ends[-1] // MOE_TILE).astype(jnp.int32)
    tile_idx = jnp.arange(n_tiles, dtype=jnp.int32)
    tile_e = jnp.searchsorted(ends, jnp.minimum(tile_idx, n_used - 1) * MOE_TILE, side="right")
    tile_e = jnp.minimum(tile_e, N_EXPERTS - 1).astype(jnp.int32)
    return src, w_sorted.reshape(p_rows, 1), pos.astype(jnp.int32), tile_e, n_used.reshape(1)


def _gather_rows_kernel(idx_ref, x_hbm, o_ref, sem):
    i = pl.program_id(0)

    def issue(r, carry):
        row = idx_ref[i * GATHER_TILE + r]
        pltpu.make_async_copy(x_hbm.at[pl.ds(row, 1)], o_ref.at[pl.ds(r, 1)], sem).start()
        return carry

    lax.fori_loop(0, GATHER_TILE, issue, 0)
    pltpu.make_async_copy(x_hbm.at[pl.ds(0, GATHER_TILE)], o_ref, sem).wait()


def _gather_rows(idx, x):
    p = idx.shape[0]
    d = x.shape[1]
    return pl.pallas_call(
        _gather_rows_kernel,
        out_shape=jax.ShapeDtypeStruct((p, d), x.dtype),
        grid_spec=pltpu.PrefetchScalarGridSpec(
            num_scalar_prefetch=1, grid=(p // GATHER_TILE,),
            in_specs=[pl.BlockSpec(memory_space=pl.ANY)],
            out_specs=pl.BlockSpec((GATHER_TILE, d), lambda i, idx: (i, 0)),
            scratch_shapes=[pltpu.SemaphoreType.DMA(())]),
        compiler_params=_cparams(("arbitrary",)),
        name="moe_gather",
    )(idx, x)


def _expert_mlp_kernel(te_ref, nu_ref, x_ref, w_ref, wg_ref, wu_ref, wd_ref, y_ref):
    @pl.when(pl.program_id(0) < nu_ref[0])
    def _():
        x = x_ref[...].astype(BF16)
        a = jnp.dot(x, wg_ref[0], preferred_element_type=F32)
        u = jnp.dot(x, wu_ref[0], preferred_element_type=F32)
        hid = (jax.nn.silu(a) * u).astype(BF16)
        y_ref[...] = jnp.dot(hid, wd_ref[0], preferred_element_type=F32) * w_ref[...]

    @pl.when(pl.program_id(0) >= nu_ref[0])
    def _():
        y_ref[...] = jnp.zeros_like(y_ref)


def _expert_mlp(x_sorted, w_sorted, tile_e, n_used, wg_b, wu_b, wd_b):
    p, d = x_sorted.shape
    de = wg_b.shape[2]
    row = lambda i, te, nu: (jnp.minimum(i, nu[0] - 1), 0)
    exp = lambda i, te, nu: (te[i], 0, 0)
    return pl.pallas_call(
        _expert_mlp_kernel,
        out_shape=jax.ShapeDtypeStruct((p, d), F32),
        grid_spec=pltpu.PrefetchScalarGridSpec(
            num_scalar_prefetch=2, grid=(p // MOE_TILE,),
            in_specs=[pl.BlockSpec((MOE_TILE, d), row),
                      pl.BlockSpec((MOE_TILE, 1), row),
                      pl.BlockSpec((1, d, de), exp),
                      pl.BlockSpec((1, d, de), exp),
                      pl.BlockSpec((1, de, d), exp)],
            out_specs=pl.BlockSpec((MOE_TILE, d), lambda i, te, nu: (i, 0))),
        compiler_params=_cparams(("arbitrary",)),
        name="moe_expert_mlp",
    )(tile_e, n_used, x_sorted, w_sorted, wg_b, wu_b, wd_b)


def _combine_ln_kernel(pos_ref, y_hbm, h_ref, g_ref, b_ref, of_ref, ob_ref, buf, sem, *, alpha, t):
    i = pl.program_id(0)
    tm = h_ref.shape[0]

    def issue(r, carry):
        tok = i * tm + r
        for k in range(2):
            pltpu.make_async_copy(y_hbm.at[pl.ds(pos_ref[k * t + tok], 1)],
                                  buf.at[k, pl.ds(r, 1)], sem).start()
        return carry

    lax.fori_loop(0, tm, issue, 0)
    for k in range(2):
        pltpu.make_async_copy(y_hbm.at[pl.ds(0, tm)], buf.at[k], sem).wait()
    out = _layer_norm_rows(alpha * h_ref[...] + (buf[0] + buf[1]), g_ref[...], b_ref[...])
    of_ref[...] = out
    ob_ref[...] = out.astype(BF16)


def _combine_ln(pos, y_sorted, h1, ln_g, ln_b, alpha):
    t, d = h1.shape
    tm = GATHER_TILE
    return pl.pallas_call(
        functools.partial(_combine_ln_kernel, alpha=alpha, t=t),
        out_shape=(jax.ShapeDtypeStruct((t, d), F32), jax.ShapeDtypeStruct((t, d), BF16)),
        grid_spec=pltpu.PrefetchScalarGridSpec(
            num_scalar_prefetch=1, grid=(t // tm,),
            in_specs=[pl.BlockSpec(memory_space=pl.ANY),
                      pl.BlockSpec((tm, d), lambda i, pos: (i, 0)),
                      pl.BlockSpec((1, d), lambda i, pos: (0, 0)),
                      pl.BlockSpec((1, d), lambda i, pos: (0, 0))],
            out_specs=(pl.BlockSpec((tm, d), lambda i, pos: (i, 0)),
                       pl.BlockSpec((tm, d), lambda i, pos: (i, 0))),
            scratch_shapes=[pltpu.VMEM((2, tm, d), F32), pltpu.SemaphoreType.DMA(())]),
        compiler_params=_cparams(("arbitrary",)),
        name="moe_combine_ln2",
    )(pos, y_sorted, h1, ln_g.reshape(1, d), ln_b.reshape(1, d))


def _moe(h1f, rt, wg_b, wu_b, wd_b, ln_g, ln_b, alpha):
    t = h1f.shape[0]
    src, w_sorted, pos, tile_e, n_used = _dispatch_plan(rt, t)
    x_sorted = _gather_rows(src, h1f)
    y_sorted = _expert_mlp(x_sorted, w_sorted, tile_e, n_used, wg_b, wu_b, wd_b)
    return _combine_ln(pos, y_sorted, h1f, ln_g, ln_b, alpha)


def kernel(x, positions, w_in, w_gate, b_gate, w_branch_a, w_branch_b, w_branch_c, w_out, sinks,
           ln1_g, ln1_b, w_router_group, w_router_expert, w_exp_gate, w_exp_up, w_exp_down,
           ln2_g, ln2_b):
    batch, seq, d = x.shape
    depth = w_in.shape[0]
    t = batch * seq
    alpha = (2 * depth) ** 0.25
    cc, ss = _rope_tables(positions)
    hf = x.reshape(t, d)
    hb = hf.astype(BF16)
    for l in range(depth):
        qkv = _qkv_proj(hb, w_in[l].astype(BF16), cc, ss)
        out_a = _dilated_attention(qkv, batch, seq)
        out_b = _moba_attention(qkv, batch, seq)
        out_c = _swa_attention(qkv, sinks[l], batch, seq)
        merged = _merge(hb, out_a, out_b, out_c, w_gate[l].astype(BF16), b_gate[l],
                        w_branch_a[l].astype(BF16), w_branch_b[l].astype(BF16),
                        w_branch_c[l].astype(BF16))
        w_router_t = jnp.zeros((32, d), F32)
        w_router_t = w_router_t.at[0:N_GROUPS].set(w_router_group[l].T)
        w_router_t = w_router_t.at[8:8 + N_EXPERTS].set(w_router_expert[l].T)
        h1f, _, rt = _outproj_ln_route(merged, w_out[l].astype(BF16), hf, ln1_g[l], ln1_b[l],
                                       w_router_t, alpha)
        hf, hb = _moe(h1f, rt, w_exp_gate[l].astype(BF16), w_exp_up[l].astype(BF16),
                      w_exp_down[l].astype(BF16), ln2_g[l], ln2_b[l], alpha)
    return hf.reshape(batch, seq, d)
```

```python
import functools
import math

import jax
import jax.numpy as jnp
from jax import lax
from jax.experimental import pallas as pl
from jax.experimental.pallas import tpu as pltpu

F32 = jnp.float32
BF16 = jnp.bfloat16

HEAD_DIM = 128
ATTN_SCALE = HEAD_DIM ** -0.5
ROPE_THETA = 10000.0
A_PATTERNS = ((128, 1), (512, 4), (2048, 16))
A_HEADS_PER_GROUP = 2
A_HEADS = A_HEADS_PER_GROUP * len(A_PATTERNS)
B_HEADS = 6
MOBA_BLOCK = 256
MOBA_TOPK = 3
C_Q_HEADS = 8
C_KV_HEADS = 2
C_GROUP = C_Q_HEADS // C_KV_HEADS
C_WINDOW = 128
BAND = 128
N_Q_HEADS = A_HEADS + B_HEADS + C_Q_HEADS
N_KV_HEADS = A_HEADS + B_HEADS + C_KV_HEADS
N_HEADS_ALL = N_Q_HEADS + 2 * N_KV_HEADS
N_GROUPS = 4
EXPERTS_PER_GROUP = 4
N_EXPERTS = N_GROUPS * EXPERTS_PER_GROUP
LN_EPS = 1e-5
NEG = -1e30

V7X_LANES = 128
V7X_VMEM_BYTES = 64 * 1024 * 1024
VMEM_LIMIT = 56 * 1024 * 1024


def _cparams(sem, vmem=VMEM_LIMIT):
    return pltpu.CompilerParams(dimension_semantics=sem, vmem_limit_bytes=vmem)


def _rope_table_kernel(pos_ref, inv_ref, cc_ref, ss_ref):
    ang = pos_ref[...].astype(F32) * inv_ref[...]
    lane = lax.broadcasted_iota(jnp.int32, ang.shape, 1)
    s = jnp.sin(ang)
    cc_ref[...] = jnp.cos(ang)
    ss_ref[...] = jnp.where(lane < HEAD_DIM // 2, -s, s)


def _rope_tables(positions):
    t = positions.size
    tm = 2048
    inv = ROPE_THETA ** (-jnp.arange(0, HEAD_DIM, 2, dtype=F32) / HEAD_DIM)
    inv2 = jnp.concatenate([inv, inv]).reshape(1, HEAD_DIM)
    pos = positions.reshape(t, 1)
    return pl.pallas_call(
        _rope_table_kernel,
        out_shape=(jax.ShapeDtypeStruct((t, HEAD_DIM), F32),) * 2,
        grid=(t // tm,),
        in_specs=[pl.BlockSpec((tm, 1), lambda i: (i, 0)),
                  pl.BlockSpec((1, HEAD_DIM), lambda i: (0, 0))],
        out_specs=(pl.BlockSpec((tm, HEAD_DIM), lambda i: (i, 0)),) * 2,
        compiler_params=_cparams(("arbitrary",)),
        name="rope_tables",
    )(pos, inv2)


def _qkv_kernel(x_ref, w_ref, cc_ref, ss_ref, o_ref, *, heads_per_tile):
    j = pl.program_id(1)
    acc = jnp.dot(x_ref[...], w_ref[...], preferred_element_type=F32)
    for hh in range(heads_per_tile):
        head = j * heads_per_tile + hh
        t = acc[:, hh * HEAD_DIM:(hh + 1) * HEAD_DIM]

        @pl.when(head < N_Q_HEADS + N_KV_HEADS)
        def _():
            r = t * cc_ref[...] + pltpu.roll(t, HEAD_DIM // 2, 1) * ss_ref[...]
            sc = jnp.where(head < N_Q_HEADS, ATTN_SCALE, 1.0).astype(F32)
            o_ref[hh] = (r * sc).astype(BF16)

        @pl.when(head >= N_Q_HEADS + N_KV_HEADS)
        def _():
            o_ref[hh] = t.astype(BF16)


def _qkv_proj(hb, w_in_b, cc, ss):
    t, d = hb.shape
    n = w_in_b.shape[1]
    tm, tn = 1024, 512
    hpt = tn // HEAD_DIM
    return pl.pallas_call(
        functools.partial(_qkv_kernel, heads_per_tile=hpt),
        out_shape=jax.ShapeDtypeStruct((n // HEAD_DIM, t, HEAD_DIM), BF16),
        grid=(t // tm, n // tn),
        in_specs=[pl.BlockSpec((tm, d), lambda i, j: (i, 0)),
                  pl.BlockSpec((d, tn), lambda i, j: (0, j)),
                  pl.BlockSpec((tm, HEAD_DIM), lambda i, j: (i, 0)),
                  pl.BlockSpec((tm, HEAD_DIM), lambda i, j: (i, 0))],
        out_specs=pl.BlockSpec((hpt, tm, HEAD_DIM), lambda i, j: (j, i, 0)),
        compiler_params=_cparams(("arbitrary", "arbitrary")),
        name="qkv_rope",
    )(hb, w_in_b, cc, ss)


def _band_bias(max_dist):
    qi = lax.broadcasted_iota(jnp.int32, (BAND, 2 * BAND), 0)
    kj = lax.broadcasted_iota(jnp.int32, (BAND, 2 * BAND), 1)
    dist = qi + BAND - kj
    valid = (dist >= 0) & (dist <= max_dist)
    bias = jnp.where(valid, 0.0, NEG).astype(F32)
    bias_first = jnp.where(valid & (kj >= BAND), 0.0, NEG).astype(F32)
    return bias, bias_first


def _band_block(qb, kwin, vwin, bias):
    g = qb.shape[0] // BAND
    s = lax.dot_general(qb, kwin, (((1,), (1,)), ((), ())), preferred_element_type=F32)
    s = s.reshape(g, BAND, 2 * BAND) + bias[None]
    m = jnp.max(s, axis=-1, keepdims=True)
    p = jnp.exp(s - m)
    l = jnp.sum(p, axis=-1, keepdims=True)
    o = jnp.dot(p.reshape(g * BAND, 2 * BAND).astype(BF16), vwin, preferred_element_type=F32)
    o = o.reshape(g, BAND, HEAD_DIM) / l
    return o, m + jnp.log(l)


def _swa_kernel(sink_ref, q_ref, k_ref, kh_ref, v_ref, vh_ref, o_ref, *, tq):
    kv = pl.program_id(1)
    t = pl.program_id(2)
    bias, bias_first = _band_bias(C_WINDOW - 1)
    for blk in range(tq // BAND):
        qb = q_ref[:, blk * BAND:(blk + 1) * BAND, :].reshape(C_GROUP * BAND, HEAD_DIM)
        if blk == 0:
            kwin = jnp.concatenate([kh_ref[0], k_ref[0, :BAND]], axis=0)
            vwin = jnp.concatenate([vh_ref[0], v_ref[0, :BAND]], axis=0)
            b = jnp.where(t == 0, bias_first, bias)
        else:
            kwin = k_ref[0, (blk - 1) * BAND:(blk + 1) * BAND]
            vwin = v_ref[0, (blk - 1) * BAND:(blk + 1) * BAND]
            b = bias
        o, lse = _band_block(qb, kwin, vwin, b)
        for g in range(C_GROUP):
            keep = jax.nn.sigmoid(lse[g] - sink_ref[kv * C_GROUP + g])
            o_ref[blk * BAND:(blk + 1) * BAND, g * HEAD_DIM:(g + 1) * HEAD_DIM] = (o[g] * keep).astype(BF16)


def _swa_attention(qkv, sinks, batch, seq):
    tq = 512
    nt = seq // tq
    q0 = A_HEADS + B_HEADS
    k0 = N_Q_HEADS + A_HEADS + B_HEADS
    v0 = N_Q_HEADS + N_KV_HEADS + A_HEADS + B_HEADS
    rb = tq // BAND

    def halo(b, kv, t):
        return jnp.maximum(b * (seq // BAND) + t * rb - 1, 0)

    return pl.pallas_call(
        functools.partial(_swa_kernel, tq=tq),
        out_shape=jax.ShapeDtypeStruct((batch * seq, C_Q_HEADS * HEAD_DIM), BF16),
        grid=(batch, C_KV_HEADS, nt),
        in_specs=[
            pl.BlockSpec(memory_space=pltpu.SMEM),
            pl.BlockSpec((C_GROUP, tq, HEAD_DIM), lambda b, kv, t: (q0 // C_GROUP + kv, b * nt + t, 0)),
            pl.BlockSpec((1, tq, HEAD_DIM), lambda b, kv, t: (k0 + kv, b * nt + t, 0)),
            pl.BlockSpec((1, BAND, HEAD_DIM), lambda b, kv, t: (k0 + kv, halo(b, kv, t), 0)),
            pl.BlockSpec((1, tq, HEAD_DIM), lambda b, kv, t: (v0 + kv, b * nt + t, 0)),
            pl.BlockSpec((1, BAND, HEAD_DIM), lambda b, kv, t: (v0 + kv, halo(b, kv, t), 0)),
        ],
        out_specs=pl.BlockSpec((tq, C_GROUP * HEAD_DIM), lambda b, kv, t: (b * nt + t, kv)),
        compiler_params=_cparams(("arbitrary",) * 3),
        name="swa_sink",
    )(sinks, qkv, qkv, qkv, qkv, qkv)


def _dilated_kernel(*refs, seq):
    ng = len(A_PATTERNS)
    in_refs = refs[:3 * ng]
    o_ref = refs[3 * ng]
    qf, kf, vf, og, lg = refs[3 * ng + 1:]
    for g, (window, dil) in enumerate(A_PATTERNS):
        q_ref, k_ref, v_ref = in_refs[3 * g:3 * g + 3]
        qf[...] = q_ref[0].astype(F32)
        kf[...] = k_ref[0].astype(F32)
        vf[...] = v_ref[0].astype(F32)
        nblk = seq // dil // BAND
        bias, bias_first = _band_bias(window // dil)

        def block(idx, carry, g=g, dil=dil, nblk=nblk, bias=bias, bias_first=bias_first):
            r = idx // nblk
            n = idx % nblk
            own = pl.ds(r + dil * BAND * n, BAND, stride=dil)
            prev = pl.ds(r + dil * BAND * jnp.maximum(n - 1, 0), BAND, stride=dil)
            qb = qf[own, :].astype(BF16)
            kwin = jnp.concatenate([kf[prev, :], kf[own, :]], axis=0).astype(BF16)
            vwin = jnp.concatenate([vf[prev, :], vf[own, :]], axis=0).astype(BF16)
            o, lse = _band_block(qb, kwin, vwin, jnp.where(n == 0, bias_first, bias))
            og[g, own, :] = o[0]
            lg[g, own, :] = jnp.broadcast_to(lse[0], (BAND, HEAD_DIM))
            return carry

        lax.fori_loop(0, dil * nblk, block, 0, unroll=2)

    rows = 256

    def combine(c, carry):
        sl = pl.ds(pl.multiple_of(c * rows, rows), rows)
        ls = [lg[g, sl, :] for g in range(ng)]
        m = functools.reduce(jnp.maximum, ls)
        es = [jnp.exp(l - m) for l in ls]
        num = functools.reduce(jnp.add, [e * og[g, sl, :] for g, e in enumerate(es)])
        o_ref[sl, :] = (num / functools.reduce(jnp.add, es)).astype(BF16)
        return carry

    lax.fori_loop(0, seq // rows, combine, 0)


def _dilated_attention(qkv, batch, seq):
    ng = len(A_PATTERNS)
    k0 = N_Q_HEADS
    v0 = N_Q_HEADS + N_KV_HEADS
    in_specs = []
    for g in range(ng):
        for base in (0, k0, v0):
            in_specs.append(pl.BlockSpec(
                (1, seq, HEAD_DIM),
                lambda b, j, h=base + g * A_HEADS_PER_GROUP: (h + j, b, 0)))
    return pl.pallas_call(
        functools.partial(_dilated_kernel, seq=seq),
        out_shape=jax.ShapeDtypeStruct((batch * seq, A_HEADS_PER_GROUP * HEAD_DIM), BF16),
        grid=(batch, A_HEADS_PER_GROUP),
        in_specs=in_specs,
        out_specs=pl.BlockSpec((seq, HEAD_DIM), lambda b, j: (b, j)),
        scratch_shapes=[pltpu.VMEM((seq, HEAD_DIM), F32)] * 3
        + [pltpu.VMEM((ng, seq, HEAD_DIM), F32)] * 2,
        compiler_params=_cparams(("arbitrary",) * 2),
        name="dilated_attn",
    )(*([qkv] * (3 * ng)))


MOBA_TQ = 1024


def _online_update(state, s, v):
    mc = jnp.max(s, axis=1, keepdims=True)
    if state is None:
        p = jnp.exp(s - mc)
        return mc, jnp.sum(p, axis=1, keepdims=True), jnp.dot(p.astype(BF16), v, preferred_element_type=F32)
    m, l, acc = state
    m_new = jnp.maximum(m, mc)
    alpha = jnp.exp(m - m_new)
    p = jnp.exp(s - m_new)
    l_new = alpha * l + jnp.sum(p, axis=1, keepdims=True)
    acc_new = alpha * acc + jnp.dot(p.astype(BF16), v, preferred_element_type=F32)
    return m_new, l_new, acc_new


def _moba_kernel(q_ref, k_ref, v_ref, o_ref, kaug, kmean, *, seq):
    t = pl.program_id(2)
    nblk = seq // MOBA_BLOCK
    blk = MOBA_BLOCK
    tq = MOBA_TQ
    nb = tq // blk

    @pl.when(t == 0)
    def _():
        kaug[:, :HEAD_DIM] = k_ref[0]
        row = lax.broadcasted_iota(jnp.int32, (seq, HEAD_DIM), 0)
        lane = lax.broadcasted_iota(jnp.int32, (seq, HEAD_DIM), 1)
        kaug[:, HEAD_DIM:] = jnp.where(row // blk == lane, 1.0, 0.0).astype(BF16)
        kmean[...] = jnp.zeros_like(kmean)
        for n in range(nblk):
            kb = k_ref[0, n * blk:(n + 1) * blk, :].astype(F32)
            kmean[n:n + 1, :] = jnp.mean(kb, axis=0, keepdims=True)

    q = q_ref[0]
    km = kmean[...]
    km_hi = km.astype(BF16)
    km_lo = (km - km_hi.astype(F32)).astype(BF16)
    nt = (((1,), (1,)), ((), ()))
    gate = (lax.dot_general(q, km_hi, nt, preferred_element_type=F32)
            + lax.dot_general(q, km_lo, nt, preferred_element_type=F32))
    lane = lax.broadcasted_iota(jnp.int32, (tq, HEAD_DIM), 1).astype(F32)
    own = (t * nb + lax.broadcasted_iota(jnp.int32, (tq, HEAD_DIM), 0) // blk).astype(F32)
    g = jnp.where(lane < own, gate, NEG)
    sel = jnp.zeros((tq, HEAD_DIM), jnp.bool_)
    for _ in range(MOBA_TOPK):
        mx = jnp.max(g, axis=1, keepdims=True)
        first = jnp.min(jnp.where(g == mx, lane, float(HEAD_DIM)), axis=1, keepdims=True)
        pick = lane == first
        sel = sel | pick
        g = jnp.where(pick, NEG, g)
    visible = (sel & (lane < own)) | (lane == own)
    qaug = jnp.concatenate([q, jnp.where(visible, 0.0, NEG).astype(BF16)], axis=1)

    ri = lax.broadcasted_iota(jnp.int32, (blk, blk), 0)
    ci = lax.broadcasted_iota(jnp.int32, (blk, blk), 1)
    causal = ri >= ci

    for c in range(seq // tq):
        @pl.when(t == c)
        def _(c=c):
            state = None
            for cc in range(c + 1):
                keys = slice(cc * tq, (cc + 1) * tq)
                s = lax.dot_general(qaug, kaug[keys, :], nt, preferred_element_type=F32)
                if cc == c:
                    s = jnp.concatenate(
                        [jnp.concatenate(
                            [jnp.where(causal, s[a * blk:(a + 1) * blk, e * blk:(e + 1) * blk], NEG)
                             if e == a else s[a * blk:(a + 1) * blk, e * blk:(e + 1) * blk]
                             for e in range(nb)], axis=1) for a in range(nb)], axis=0)
                state = _online_update(state, s, v_ref[0, keys, :])
            m, l, acc = state
            o_ref[...] = (acc / l).astype(BF16)


def _moba_attention(qkv, batch, seq):
    nt = seq // MOBA_TQ
    q0 = A_HEADS
    k0 = N_Q_HEADS + A_HEADS
    v0 = N_Q_HEADS + N_KV_HEADS + A_HEADS
    return pl.pallas_call(
        functools.partial(_moba_kernel, seq=seq),
        out_shape=jax.ShapeDtypeStruct((batch * seq, B_HEADS * HEAD_DIM), BF16),
        grid=(batch, B_HEADS, nt),
        in_specs=[
            pl.BlockSpec((1, MOBA_TQ, HEAD_DIM), lambda b, h, t: (q0 + h, b * nt + t, 0)),
            pl.BlockSpec((1, seq, HEAD_DIM), lambda b, h, t: (k0 + h, b, 0)),
            pl.BlockSpec((1, seq, HEAD_DIM), lambda b, h, t: (v0 + h, b, 0)),
        ],
        out_specs=pl.BlockSpec((MOBA_TQ, HEAD_DIM), lambda b, h, t: (b * nt + t, h)),
        scratch_shapes=[pltpu.VMEM((seq, 2 * HEAD_DIM), BF16), pltpu.VMEM((HEAD_DIM, HEAD_DIM), F32)],
        compiler_params=_cparams(("arbitrary",) * 3),
        name="moba_attn",
    )(qkv, qkv, qkv)


def _merge_kernel(h_ref, oa_ref, ob_ref, oc_ref, g0_ref, g1_ref, g2_ref, b0_ref, b1_ref, b2_ref,
                  wa_ref, wb_ref, wc_ref, o_ref):
    x = h_ref[...]
    acc = None
    for g_ref, b_ref, br_ref, w_ref in ((g0_ref, b0_ref, oa_ref, wa_ref),
                                        (g1_ref, b1_ref, ob_ref, wb_ref),
                                        (g2_ref, b2_ref, oc_ref, wc_ref)):
        gate = jax.nn.sigmoid(jnp.dot(x, g_ref[...], preferred_element_type=F32) + b_ref[...])
        term = gate * jnp.dot(br_ref[...], w_ref[...], preferred_element_type=F32)
        acc = term if acc is None else acc + term
    o_ref[...] = acc.astype(BF16)


def _merge(hb, out_a, out_b, out_c, w_gate_b, b_gate, wa_b, wb_b, wc_b):
    t, d = hb.shape
    tm, tn = 1024, 512
    nj = d // tn
    row = lambda w: pl.BlockSpec((tm, w), lambda j, i: (i, 0))
    gate_w = lambda x: pl.BlockSpec((d, tn), lambda j, i, x=x: (0, x * nj + j))
    gate_b = lambda x: pl.BlockSpec((1, tn), lambda j, i, x=x: (0, x * nj + j))
    br_w = lambda k: pl.BlockSpec((k, tn), lambda j, i: (0, j))
    bg = b_gate.reshape(1, -1)
    return pl.pallas_call(
        _merge_kernel,
        out_shape=jax.ShapeDtypeStruct((t, d), BF16),
        grid=(nj, t // tm),
        in_specs=[row(d), row(out_a.shape[1]), row(out_b.shape[1]), row(out_c.shape[1]),
                  gate_w(0), gate_w(1), gate_w(2), gate_b(0), gate_b(1), gate_b(2),
                  br_w(wa_b.shape[0]), br_w(wb_b.shape[0]), br_w(wc_b.shape[0])],
        out_specs=pl.BlockSpec((tm, tn), lambda j, i: (i, j)),
        compiler_params=_cparams(("arbitrary",) * 2),
        name="branch_merge",
    )(hb, out_a, out_b, out_c, w_gate_b, w_gate_b, w_gate_b, bg, bg, bg, wa_b, wb_b, wc_b)


def _layer_norm_rows(y, g, b):
    mu = jnp.mean(y, axis=-1, keepdims=True)
    yc = y - mu
    var = jnp.mean(yc * yc, axis=-1, keepdims=True)
    return yc * lax.rsqrt(var + LN_EPS) * g + b


def _route(lt):
    tm = lt.shape[1]
    gl = lt[0:N_GROUPS]
    gmax = jnp.max(gl, axis=0, keepdims=True)
    ge = jnp.exp(gl - gmax)
    gp = ge / jnp.sum(ge, axis=0, keepdims=True)
    g_val = jnp.max(gp, axis=0, keepdims=True)
    row = lax.broadcasted_iota(jnp.int32, (N_GROUPS, tm), 0)
    g_sel = jnp.min(jnp.where(gp == g_val, row, N_GROUPS), axis=0, keepdims=True)
    el = jnp.zeros((EXPERTS_PER_GROUP, tm), F32)
    for g in range(N_GROUPS):
        lo = 8 + g * EXPERTS_PER_GROUP
        el = jnp.where(g_sel == g, lt[lo:lo + EXPERTS_PER_GROUP], el)
    v0 = jnp.max(el, axis=0, keepdims=True)
    i0 = jnp.min(jnp.where(el == v0, row, EXPERTS_PER_GROUP), axis=0, keepdims=True)
    el1 = jnp.where(row == i0, -jnp.inf, el)
    v1 = jnp.max(el1, axis=0, keepdims=True)
    i1 = jnp.min(jnp.where(el1 == v1, row, EXPERTS_PER_GROUP), axis=0, keepdims=True)
    e1 = jnp.exp(v1 - v0)
    den = 1.0 + e1
    w0 = 1.0 / den * g_val
    w1 = e1 / den * g_val
    id0 = (g_sel * EXPERTS_PER_GROUP + i0).astype(F32)
    id1 = (g_sel * EXPERTS_PER_GROUP + i1).astype(F32)
    return id0, id1, w0, w1


def _outproj_ln_route_kernel(m_ref, w_ref, h_ref, g_ref, b_ref, wr_ref, hf_ref, rt_ref, *, alpha):
    mix = jnp.dot(m_ref[...], w_ref[...], preferred_element_type=F32)
    h1 = _layer_norm_rows(alpha * h_ref[...] + mix, g_ref[...], b_ref[...])
    hf_ref[...] = h1
    lt = lax.dot_general(wr_ref[...], h1, (((1,), (1,)), ((), ())),
                         precision=lax.Precision.HIGHEST, preferred_element_type=F32)
    id0, id1, w0, w1 = _route(lt)
    rt_ref[...] = jnp.concatenate([id0, id1, w0, w1, jnp.zeros((4, lt.shape[1]), F32)], axis=0)


def _outproj_ln_route(merged, w_out_b, h, ln_g, ln_b, w_router_t, alpha):
    t, d = h.shape
    tm = 512
    return pl.pallas_call(
        functools.partial(_outproj_ln_route_kernel, alpha=alpha),
        out_shape=(jax.ShapeDtypeStruct((t, d), F32), jax.ShapeDtypeStruct((8, t), F32)),
        grid=(t // tm,),
        in_specs=[pl.BlockSpec((tm, d), lambda i: (i, 0)),
                  pl.BlockSpec((d, d), lambda i: (0, 0)),
                  pl.BlockSpec((tm, d), lambda i: (i, 0)),
                  pl.BlockSpec((1, d), lambda i: (0, 0)),
                  pl.BlockSpec((1, d), lambda i: (0, 0)),
                  pl.BlockSpec((32, d), lambda i: (0, 0))],
        out_specs=(pl.BlockSpec((tm, d), lambda i: (i, 0)),
                   pl.BlockSpec((8, tm), lambda i: (0, i))),
        compiler_params=_cparams(("arbitrary",)),
        name="outproj_ln1_route",
    )(merged, w_out_b, h, ln_g.reshape(1, d), ln_b.reshape(1, d), w_router_t)


MOE_TILE = 256
GATHER_TILE = 512
DMA_UNROLL = 8


def _dispatch_plan(rt, t):
    ids = rt[0:2].astype(jnp.int32).reshape(-1)
    tok = jnp.tile(jnp.arange(t, dtype=jnp.int32), 2)
    onehot = (ids[:, None] == jnp.arange(N_EXPERTS, dtype=jnp.int32)[None, :]).astype(jnp.int32)
    csum = jnp.cumsum(onehot, axis=0)
    counts = csum[-1]
    rank = jnp.take_along_axis(csum, ids[:, None], axis=1)[:, 0] - 1
    padded = (counts + MOE_TILE - 1) // MOE_TILE * MOE_TILE
    ends = jnp.cumsum(padded)
    pos = (ends - padded)[ids] + rank
    p_rows = 2 * t + N_EXPERTS * MOE_TILE
    src = jnp.zeros((p_rows,), jnp.int32).at[pos].set(tok)
    n_tiles = p_rows // MOE_TILE
    n_used = (ends[-1] // MOE_TILE).astype(jnp.int32)
    tile_idx = jnp.arange(n_tiles, dtype=jnp.int32)
    tile_start = jnp.minimum(tile_idx, n_used - 1) * MOE_TILE
    tile_e = jnp.sum((ends[None, :] <= tile_start[:, None]).astype(jnp.int32), axis=1)
    tile_e = jnp.minimum(tile_e, N_EXPERTS - 1)
    w_cols = rt[2:4].reshape(2, t, 1)
    return src, pos.astype(jnp.int32), w_cols, tile_e, n_used.reshape(1)


def _gather_rows_kernel(idx_ref, x_hbm, o_ref, buf, sem):
    i = pl.program_id(0)

    def issue(c, carry):
        base = pl.multiple_of(c * DMA_UNROLL, DMA_UNROLL)
        dst = buf.at[pl.ds(base, DMA_UNROLL)]
        for j in range(DMA_UNROLL):
            row = idx_ref[i * GATHER_TILE + base + j]
            pltpu.make_async_copy(x_hbm.at[pl.ds(row, 1)], dst.at[pl.ds(j, 1)], sem).start()
        return carry

    lax.fori_loop(0, GATHER_TILE // DMA_UNROLL, issue, 0)
    pltpu.make_async_copy(x_hbm.at[pl.ds(0, GATHER_TILE)], buf, sem).wait()
    o_ref[...] = buf[...].astype(BF16)


def _gather_rows(idx, x):
    p = idx.shape[0]
    d = x.shape[1]
    return pl.pallas_call(
        _gather_rows_kernel,
        out_shape=jax.ShapeDtypeStruct((p, d), BF16),
        grid_spec=pltpu.PrefetchScalarGridSpec(
            num_scalar_prefetch=1, grid=(p // GATHER_TILE,),
            in_specs=[pl.BlockSpec(memory_space=pl.ANY)],
            out_specs=pl.BlockSpec((GATHER_TILE, d), lambda i, idx: (i, 0)),
            scratch_shapes=[pltpu.VMEM((GATHER_TILE, d), x.dtype), pltpu.SemaphoreType.DMA(())]),
        compiler_params=_cparams(("arbitrary",)),
        name="moe_gather",
    )(idx, x)


def _expert_mlp_kernel(te_ref, nu_ref, x_ref, wg_ref, wu_ref, wd_ref, y_ref):
    @pl.when(pl.program_id(0) < nu_ref[0])
    def _():
        x = x_ref[...]
        a = jnp.dot(x, wg_ref[0], preferred_element_type=F32)
        u = jnp.dot(x, wu_ref[0], preferred_element_type=F32)
        hid = (jax.nn.silu(a) * u).astype(BF16)
        y_ref[...] = jnp.dot(hid, wd_ref[0], preferred_element_type=F32)

    @pl.when(pl.program_id(0) >= nu_ref[0])
    def _():
        y_ref[...] = jnp.zeros_like(y_ref)


def _expert_mlp(x_sorted, tile_e, n_used, wg_b, wu_b, wd_b):
    p, d = x_sorted.shape
    de = wg_b.shape[2]
    row = lambda i, te, nu: (jnp.minimum(i, nu[0] - 1), 0)
    exp = lambda i, te, nu: (te[i], 0, 0)
    return pl.pallas_call(
        _expert_mlp_kernel,
        out_shape=jax.ShapeDtypeStruct((p, d), F32),
        grid_spec=pltpu.PrefetchScalarGridSpec(
            num_scalar_prefetch=2, grid=(p // MOE_TILE,),
            in_specs=[pl.BlockSpec((MOE_TILE, d), row),
                      pl.BlockSpec((1, d, de), exp),
                      pl.BlockSpec((1, d, de), exp),
                      pl.BlockSpec((1, de, d), exp)],
            out_specs=pl.BlockSpec((MOE_TILE, d), lambda i, te, nu: (i, 0))),
        compiler_params=_cparams(("arbitrary",)),
        name="moe_expert_mlp",
    )(tile_e, n_used, x_sorted, wg_b, wu_b, wd_b)


def _combine_ln_kernel(pos_ref, y_hbm, h_ref, w_ref, g_ref, b_ref, of_ref, ob_ref, buf, sem, *, alpha, t):
    i = pl.program_id(0)
    tm = h_ref.shape[0]

    def issue(c, carry):
        base = pl.multiple_of(c * DMA_UNROLL, DMA_UNROLL)
        for k in range(2):
            dst = buf.at[k, pl.ds(base, DMA_UNROLL)]
            for j in range(DMA_UNROLL):
                row = pos_ref[k * t + i * tm + base + j]
                pltpu.make_async_copy(y_hbm.at[pl.ds(row, 1)], dst.at[pl.ds(j, 1)], sem).start()
        return carry

    lax.fori_loop(0, tm // DMA_UNROLL, issue, 0)
    for k in range(2):
        pltpu.make_async_copy(y_hbm.at[pl.ds(0, tm)], buf.at[k], sem).wait()
    ffn = w_ref[0] * buf[0] + w_ref[1] * buf[1]
    out = _layer_norm_rows(alpha * h_ref[...] + ffn, g_ref[...], b_ref[...])
    of_ref[...] = out
    ob_ref[...] = out.astype(BF16)


def _combine_ln(pos, w_cols, y_sorted, h1, ln_g, ln_b, alpha):
    t, d = h1.shape
    tm = GATHER_TILE
    return pl.pallas_call(
        functools.partial(_combine_ln_kernel, alpha=alpha, t=t),
        out_shape=(jax.ShapeDtypeStruct((t, d), F32), jax.ShapeDtypeStruct((t, d), BF16)),
        grid_spec=pltpu.PrefetchScalarGridSpec(
            num_scalar_prefetch=1, grid=(t // tm,),
            in_specs=[pl.BlockSpec(memory_space=pl.ANY),
                      pl.BlockSpec((tm, d), lambda i, pos: (i, 0)),
                      pl.BlockSpec((2, tm, 1), lambda i, pos: (0, i, 0)),
                      pl.BlockSpec((1, d), lambda i, pos: (0, 0)),
                      pl.BlockSpec((1, d), lambda i, pos: (0, 0))],
            out_specs=(pl.BlockSpec((tm, d), lambda i, pos: (i, 0)),
                       pl.BlockSpec((tm, d), lambda i, pos: (i, 0))),
            scratch_shapes=[pltpu.VMEM((2, tm, d), F32), pltpu.SemaphoreType.DMA(())]),
        compiler_params=_cparams(("arbitrary",)),
        name="moe_combine_ln2",
    )(pos, y_sorted, h1, w_cols, ln_g.reshape(1, d), ln_b.reshape(1, d))


def _moe(h1f, rt, wg_b, wu_b, wd_b, ln_g, ln_b, alpha):
    t = h1f.shape[0]
    src, pos, w_cols, tile_e, n_used = _dispatch_plan(rt, t)
    x_sorted = _gather_rows(src, h1f)
    y_sorted = _expert_mlp(x_sorted, tile_e, n_used, wg_b, wu_b, wd_b)
    return _combine_ln(pos, w_cols, y_sorted, h1f, ln_g, ln_b, alpha)


def kernel(x, positions, w_in, w_gate, b_gate, w_branch_a, w_branch_b, w_branch_c, w_out, sinks,
           ln1_g, ln1_b, w_router_group, w_router_expert, w_exp_gate, w_exp_up, w_exp_down,
           ln2_g, ln2_b):
    batch, seq, d = x.shape
    depth = w_in.shape[0]
    t = batch * seq
    alpha = (2 * depth) ** 0.25
    cc, ss = _rope_tables(positions)
    hf = x.reshape(t, d)
    hb = hf.astype(BF16)
    for l in range(depth):
        qkv = _qkv_proj(hb, w_in[l].astype(BF16), cc, ss)
        out_a = _dilated_attention(qkv, batch, seq)
        out_b = _moba_attention(qkv, batch, seq)
        out_c = _swa_attention(qkv, sinks[l], batch, seq)
        merged = _merge(hb, out_a, out_b, out_c, w_gate[l].astype(BF16), b_gate[l],
                        w_branch_a[l].astype(BF16), w_branch_b[l].astype(BF16),
                        w_branch_c[l].astype(BF16))
        w_router_t = jnp.zeros((32, d), F32)
        w_router_t = w_router_t.at[0:N_GROUPS].set(w_router_group[l].T)
        w_router_t = w_router_t.at[8:8 + N_EXPERTS].set(w_router_expert[l].T)
        h1f, rt = _outproj_ln_route(merged, w_out[l].astype(BF16), hf, ln1_g[l], ln1_b[l],
                                       w_router_t, alpha)
        hf, hb = _moe(h1f, rt, w_exp_gate[l].astype(BF16), w_exp_up[l].astype(BF16),
                      w_exp_down[l].astype(BF16), ln2_g[l], ln2_b[l], alpha)
    return hf.reshape(batch, seq, d)
```

```python
import functools
import math

import jax
import jax.numpy as jnp
from jax import lax
from jax.experimental import pallas as pl
from jax.experimental.pallas import tpu as pltpu

F32 = jnp.float32
BF16 = jnp.bfloat16

HEAD_DIM = 128
ATTN_SCALE = HEAD_DIM ** -0.5
ROPE_THETA = 10000.0
A_PATTERNS = ((128, 1), (512, 4), (2048, 16))
A_HEADS_PER_GROUP = 2
A_HEADS = A_HEADS_PER_GROUP * len(A_PATTERNS)
B_HEADS = 6
MOBA_BLOCK = 256
MOBA_TOPK = 3
C_Q_HEADS = 8
C_KV_HEADS = 2
C_GROUP = C_Q_HEADS // C_KV_HEADS
C_WINDOW = 128
BAND = 128
N_Q_HEADS = A_HEADS + B_HEADS + C_Q_HEADS
N_KV_HEADS = A_HEADS + B_HEADS + C_KV_HEADS
N_HEADS_ALL = N_Q_HEADS + 2 * N_KV_HEADS
N_GROUPS = 4
EXPERTS_PER_GROUP = 4
N_EXPERTS = N_GROUPS * EXPERTS_PER_GROUP
LN_EPS = 1e-5
NEG = -1e30

V7X_LANES = 128
V7X_VMEM_BYTES = 64 * 1024 * 1024
VMEM_LIMIT = 56 * 1024 * 1024


def _cparams(sem, vmem=VMEM_LIMIT):
    return pltpu.CompilerParams(dimension_semantics=sem, vmem_limit_bytes=vmem)


ROPE_SLOT_Q, ROPE_SLOT_K, ROPE_SLOT_V = 0, 1, 2


def _rope_table_kernel(pos_ref, inv_ref, cc_ref, ss_ref):
    ang = pos_ref[...].astype(F32) * inv_ref[...]
    lane = lax.broadcasted_iota(jnp.int32, ang.shape, 1)
    s = jnp.sin(ang)
    c = jnp.cos(ang)
    s = jnp.where(lane < HEAD_DIM // 2, -s, s)
    cc_ref[ROPE_SLOT_Q] = c * ATTN_SCALE
    ss_ref[ROPE_SLOT_Q] = s * ATTN_SCALE
    cc_ref[ROPE_SLOT_K] = c
    ss_ref[ROPE_SLOT_K] = s
    cc_ref[ROPE_SLOT_V] = jnp.ones_like(c)
    ss_ref[ROPE_SLOT_V] = jnp.zeros_like(s)


def _rope_tables(positions):
    t = positions.size
    tm = 2048
    inv = ROPE_THETA ** (-jnp.arange(0, HEAD_DIM, 2, dtype=F32) / HEAD_DIM)
    inv2 = jnp.concatenate([inv, inv]).reshape(1, HEAD_DIM)
    pos = positions.reshape(t, 1)
    return pl.pallas_call(
        _rope_table_kernel,
        out_shape=(jax.ShapeDtypeStruct((3, t, HEAD_DIM), F32),) * 2,
        grid=(t // tm,),
        in_specs=[pl.BlockSpec((tm, 1), lambda i: (i, 0)),
                  pl.BlockSpec((1, HEAD_DIM), lambda i: (0, 0))],
        out_specs=(pl.BlockSpec((3, tm, HEAD_DIM), lambda i: (0, i, 0)),) * 2,
        compiler_params=_cparams(("arbitrary",)),
        name="rope_tables",
    )(pos, inv2)


def _qkv_kernel(x_ref, w_ref, cc_ref, ss_ref, o_ref, *, heads_per_tile):
    j = pl.program_id(1)
    acc = jnp.dot(x_ref[...], w_ref[...], preferred_element_type=F32)
    for hh in range(heads_per_tile):
        head = j * heads_per_tile + hh
        slot = (head >= N_Q_HEADS).astype(jnp.int32) + (head >= N_Q_HEADS + N_KV_HEADS).astype(jnp.int32)
        t = acc[:, hh * HEAD_DIM:(hh + 1) * HEAD_DIM]
        r = t * cc_ref[slot] + pltpu.roll(t, HEAD_DIM // 2, 1) * ss_ref[slot]
        o_ref[hh] = r.astype(BF16)


def _qkv_proj(hb, w_in_b, cc, ss):
    t, d = hb.shape
    n = w_in_b.shape[1]
    tm, tn = 1024, 512
    hpt = tn // HEAD_DIM
    return pl.pallas_call(
        functools.partial(_qkv_kernel, heads_per_tile=hpt),
        out_shape=jax.ShapeDtypeStruct((n // HEAD_DIM, t, HEAD_DIM), BF16),
        grid=(t // tm, n // tn),
        in_specs=[pl.BlockSpec((tm, d), lambda i, j: (i, 0)),
                  pl.BlockSpec((d, tn), lambda i, j: (0, j)),
                  pl.BlockSpec((3, tm, HEAD_DIM), lambda i, j: (0, i, 0)),
                  pl.BlockSpec((3, tm, HEAD_DIM), lambda i, j: (0, i, 0))],
        out_specs=pl.BlockSpec((hpt, tm, HEAD_DIM), lambda i, j: (j, i, 0)),
        compiler_params=_cparams(("arbitrary", "arbitrary")),
        name="qkv_rope",
    )(hb, w_in_b, cc, ss)


def _band_bias(max_dist):
    qi = lax.broadcasted_iota(jnp.int32, (BAND, 2 * BAND), 0)
    kj = lax.broadcasted_iota(jnp.int32, (BAND, 2 * BAND), 1)
    dist = qi + BAND - kj
    valid = (dist >= 0) & (dist <= max_dist)
    bias = jnp.where(valid, 0.0, NEG).astype(F32)
    bias_first = jnp.where(valid & (kj >= BAND), 0.0, NEG).astype(F32)
    return bias, bias_first


def _band_block(qb, kwin, vwin, bias):
    g = qb.shape[0] // BAND
    s = lax.dot_general(qb, kwin, (((1,), (1,)), ((), ())), preferred_element_type=F32)
    s = s.reshape(g, BAND, 2 * BAND) + bias[None]
    m = jnp.max(s, axis=-1, keepdims=True)
    p = jnp.exp(s - m)
    l = jnp.sum(p, axis=-1, keepdims=True)
    o = jnp.dot(p.reshape(g * BAND, 2 * BAND).astype(BF16), vwin, preferred_element_type=F32)
    o = o.reshape(g, BAND, HEAD_DIM) / l
    return o, m + jnp.log(l)


def _swa_kernel(sink_ref, q_ref, k_ref, kh_ref, v_ref, vh_ref, o_ref, *, tq):
    kv = pl.program_id(1)
    t = pl.program_id(2)
    bias, bias_first = _band_bias(C_WINDOW - 1)
    for blk in range(tq // BAND):
        qb = q_ref[:, blk * BAND:(blk + 1) * BAND, :].reshape(C_GROUP * BAND, HEAD_DIM)
        if blk == 0:
            kwin = jnp.concatenate([kh_ref[0], k_ref[0, :BAND]], axis=0)
            vwin = jnp.concatenate([vh_ref[0], v_ref[0, :BAND]], axis=0)
            b = jnp.where(t == 0, bias_first, bias)
        else:
            kwin = k_ref[0, (blk - 1) * BAND:(blk + 1) * BAND]
            vwin = v_ref[0, (blk - 1) * BAND:(blk + 1) * BAND]
            b = bias
        o, lse = _band_block(qb, kwin, vwin, b)
        for g in range(C_GROUP):
            keep = jax.nn.sigmoid(lse[g] - sink_ref[kv * C_GROUP + g])
            o_ref[blk * BAND:(blk + 1) * BAND, g * HEAD_DIM:(g + 1) * HEAD_DIM] = (o[g] * keep).astype(BF16)


def _swa_attention(qkv, sinks, batch, seq):
    tq = 512
    nt = seq // tq
    q0 = A_HEADS + B_HEADS
    k0 = N_Q_HEADS + A_HEADS + B_HEADS
    v0 = N_Q_HEADS + N_KV_HEADS + A_HEADS + B_HEADS
    rb = tq // BAND

    def halo(b, kv, t):
        return jnp.maximum(b * (seq // BAND) + t * rb - 1, 0)

    return pl.pallas_call(
        functools.partial(_swa_kernel, tq=tq),
        out_shape=jax.ShapeDtypeStruct((batch * seq, C_Q_HEADS * HEAD_DIM), BF16),
        grid=(batch, C_KV_HEADS, nt),
        in_specs=[
            pl.BlockSpec(memory_space=pltpu.SMEM),
            pl.BlockSpec((C_GROUP, tq, HEAD_DIM), lambda b, kv, t: (q0 // C_GROUP + kv, b * nt + t, 0)),
            pl.BlockSpec((1, tq, HEAD_DIM), lambda b, kv, t: (k0 + kv, b * nt + t, 0)),
            pl.BlockSpec((1, BAND, HEAD_DIM), lambda b, kv, t: (k0 + kv, halo(b, kv, t), 0)),
            pl.BlockSpec((1, tq, HEAD_DIM), lambda b, kv, t: (v0 + kv, b * nt + t, 0)),
            pl.BlockSpec((1, BAND, HEAD_DIM), lambda b, kv, t: (v0 + kv, halo(b, kv, t), 0)),
        ],
        out_specs=pl.BlockSpec((tq, C_GROUP * HEAD_DIM), lambda b, kv, t: (b * nt + t, kv)),
        compiler_params=_cparams(("arbitrary",) * 3),
        name="swa_sink",
    )(sinks, qkv, qkv, qkv, qkv, qkv)


def _dilated_kernel(*refs, seq):
    ng = len(A_PATTERNS)
    in_refs = refs[:3 * ng]
    o_ref = refs[3 * ng]
    qf, kf, vf, og, lg = refs[3 * ng + 1:]
    for g, (window, dil) in enumerate(A_PATTERNS):
        q_ref, k_ref, v_ref = in_refs[3 * g:3 * g + 3]
        qf[...] = q_ref[0].astype(F32)
        kf[...] = k_ref[0].astype(F32)
        vf[...] = v_ref[0].astype(F32)
        nblk = seq // dil // BAND
        bias, bias_first = _band_bias(window // dil)

        for r in range(dil):
            for n in range(nblk):
                own = pl.ds(r + dil * BAND * n, BAND, stride=dil)
                prev = pl.ds(r + dil * BAND * max(n - 1, 0), BAND, stride=dil)
                qb = qf[own, :].astype(BF16)
                kwin = jnp.concatenate([kf[prev, :], kf[own, :]], axis=0).astype(BF16)
                vwin = jnp.concatenate([vf[prev, :], vf[own, :]], axis=0).astype(BF16)
                o, lse = _band_block(qb, kwin, vwin, bias_first if n == 0 else bias)
                og[g, own, :] = o[0]
                lg[g, own, :] = jnp.broadcast_to(lse[0], (BAND, HEAD_DIM))

    rows = 256

    def combine(c, carry):
        sl = pl.ds(pl.multiple_of(c * rows, rows), rows)
        ls = [lg[g, sl, :] for g in range(ng)]
        m = functools.reduce(jnp.maximum, ls)
        es = [jnp.exp(l - m) for l in ls]
        num = functools.reduce(jnp.add, [e * og[g, sl, :] for g, e in enumerate(es)])
        o_ref[sl, :] = (num / functools.reduce(jnp.add, es)).astype(BF16)
        return carry

    lax.fori_loop(0, seq // rows, combine, 0)


def _dilated_attention(qkv, batch, seq):
    ng = len(A_PATTERNS)
    k0 = N_Q_HEADS
    v0 = N_Q_HEADS + N_KV_HEADS
    in_specs = []
    for g in range(ng):
        for base in (0, k0, v0):
            in_specs.append(pl.BlockSpec(
                (1, seq, HEAD_DIM),
                lambda b, j, h=base + g * A_HEADS_PER_GROUP: (h + j, b, 0)))
    return pl.pallas_call(
        functools.partial(_dilated_kernel, seq=seq),
        out_shape=jax.ShapeDtypeStruct((batch * seq, A_HEADS_PER_GROUP * HEAD_DIM), BF16),
        grid=(batch, A_HEADS_PER_GROUP),
        in_specs=in_specs,
        out_specs=pl.BlockSpec((seq, HEAD_DIM), lambda b, j: (b, j)),
        scratch_shapes=[pltpu.VMEM((seq, HEAD_DIM), F32)] * 3
        + [pltpu.VMEM((ng, seq, HEAD_DIM), F32)] * 2,
        compiler_params=_cparams(("arbitrary",) * 2),
        name="dilated_attn",
    )(*([qkv] * (3 * ng)))


MOBA_TQ = 1024


def _online_update(state, s, v):
    mc = jnp.max(s, axis=1, keepdims=True)
    if state is None:
        p = jnp.exp(s - mc)
        return mc, jnp.sum(p, axis=1, keepdims=True), jnp.dot(p.astype(BF16), v, preferred_element_type=F32)
    m, l, acc = state
    m_new = jnp.maximum(m, mc)
    alpha = jnp.exp(m - m_new)
    p = jnp.exp(s - m_new)
    l_new = alpha * l + jnp.sum(p, axis=1, keepdims=True)
    acc_new = alpha * acc + jnp.dot(p.astype(BF16), v, preferred_element_type=F32)
    return m_new, l_new, acc_new


def _moba_kernel(q_ref, k_ref, v_ref, o_ref, kaug, kmean, *, seq):
    t = pl.program_id(2)
    nblk = seq // MOBA_BLOCK
    blk = MOBA_BLOCK
    tq = MOBA_TQ
    nb = tq // blk

    @pl.when(t == 0)
    def _():
        kaug[:, :HEAD_DIM] = k_ref[0]
        row = lax.broadcasted_iota(jnp.int32, (seq, HEAD_DIM), 0)
        lane = lax.broadcasted_iota(jnp.int32, (seq, HEAD_DIM), 1)
        kaug[:, HEAD_DIM:] = jnp.where(row // blk == lane, 1.0, 0.0).astype(BF16)
        kmean[...] = jnp.zeros_like(kmean)
        for n in range(nblk):
            kb = k_ref[0, n * blk:(n + 1) * blk, :].astype(F32)
            kmean[n:n + 1, :] = jnp.mean(kb, axis=0, keepdims=True)

    q = q_ref[0]
    km = kmean[...]
    km_hi = km.astype(BF16)
    km_lo = (km - km_hi.astype(F32)).astype(BF16)
    nt = (((1,), (1,)), ((), ()))
    gate = (lax.dot_general(q, km_hi, nt, preferred_element_type=F32)
            + lax.dot_general(q, km_lo, nt, preferred_element_type=F32))
    lane = lax.broadcasted_iota(jnp.int32, (tq, HEAD_DIM), 1).astype(F32)
    own = (t * nb + lax.broadcasted_iota(jnp.int32, (tq, HEAD_DIM), 0) // blk).astype(F32)
    g = jnp.where(lane < own, gate, NEG)
    sel = jnp.zeros((tq, HEAD_DIM), jnp.bool_)
    for _ in range(MOBA_TOPK):
        mx = jnp.max(g, axis=1, keepdims=True)
        first = jnp.min(jnp.where(g == mx, lane, float(HEAD_DIM)), axis=1, keepdims=True)
        pick = lane == first
        sel = sel | pick
        g = jnp.where(pick, NEG, g)
    visible = (sel & (lane < own)) | (lane == own)
    qaug = jnp.concatenate([q, jnp.where(visible, 0.0, NEG).astype(BF16)], axis=1)

    ri = lax.broadcasted_iota(jnp.int32, (blk, blk), 0)
    ci = lax.broadcasted_iota(jnp.int32, (blk, blk), 1)
    causal = ri >= ci

    for c in range(seq // tq):
        @pl.when(t == c)
        def _(c=c):
            state = None
            for cc in range(c + 1):
                keys = slice(cc * tq, (cc + 1) * tq)
                s = lax.dot_general(qaug, kaug[keys, :], nt, preferred_element_type=F32)
                if cc == c:
                    s = jnp.concatenate(
                        [jnp.concatenate(
                            [jnp.where(causal, s[a * blk:(a + 1) * blk, e * blk:(e + 1) * blk], NEG)
                             if e == a else s[a * blk:(a + 1) * blk, e * blk:(e + 1) * blk]
                             for e in range(nb)], axis=1) for a in range(nb)], axis=0)
                state = _online_update(state, s, v_ref[0, keys, :])
            m, l, acc = state
            o_ref[...] = (acc / l).astype(BF16)


def _moba_attention(qkv, batch, seq):
    nt = seq // MOBA_TQ
    q0 = A_HEADS
    k0 = N_Q_HEADS + A_HEADS
    v0 = N_Q_HEADS + N_KV_HEADS + A_HEADS
    return pl.pallas_call(
        functools.partial(_moba_kernel, seq=seq),
        out_shape=jax.ShapeDtypeStruct((batch * seq, B_HEADS * HEAD_DIM), BF16),
        grid=(batch, B_HEADS, nt),
        in_specs=[
            pl.BlockSpec((1, MOBA_TQ, HEAD_DIM), lambda b, h, t: (q0 + h, b * nt + t, 0)),
            pl.BlockSpec((1, seq, HEAD_DIM), lambda b, h, t: (k0 + h, b, 0)),
            pl.BlockSpec((1, seq, HEAD_DIM), lambda b, h, t: (v0 + h, b, 0)),
        ],
        out_specs=pl.BlockSpec((MOBA_TQ, HEAD_DIM), lambda b, h, t: (b * nt + t, h)),
        scratch_shapes=[pltpu.VMEM((seq, 2 * HEAD_DIM), BF16), pltpu.VMEM((HEAD_DIM, HEAD_DIM), F32)],
        compiler_params=_cparams(("arbitrary",) * 3),
        name="moba_attn",
    )(qkv, qkv, qkv)


def _merge_kernel(h_ref, oa_ref, ob_ref, oc_ref, g0_ref, g1_ref, g2_ref, b0_ref, b1_ref, b2_ref,
                  wa_ref, wb_ref, wc_ref, o_ref):
    x = h_ref[...]
    acc = None
    for g_ref, b_ref, br_ref, w_ref in ((g0_ref, b0_ref, oa_ref, wa_ref),
                                        (g1_ref, b1_ref, ob_ref, wb_ref),
                                        (g2_ref, b2_ref, oc_ref, wc_ref)):
        gate = jax.nn.sigmoid(jnp.dot(x, g_ref[...], preferred_element_type=F32) + b_ref[...])
        term = gate * jnp.dot(br_ref[...], w_ref[...], preferred_element_type=F32)
        acc = term if acc is None else acc + term
    o_ref[...] = acc.astype(BF16)


def _merge(hb, out_a, out_b, out_c, w_gate_b, b_gate, wa_b, wb_b, wc_b):
    t, d = hb.shape
    tm, tn = 1024, 512
    nj = d // tn
    row = lambda w: pl.BlockSpec((tm, w), lambda j, i: (i, 0))
    gate_w = lambda x: pl.BlockSpec((d, tn), lambda j, i, x=x: (0, x * nj + j))
    gate_b = lambda x: pl.BlockSpec((1, tn), lambda j, i, x=x: (0, x * nj + j))
    br_w = lambda k: pl.BlockSpec((k, tn), lambda j, i: (0, j))
    bg = b_gate.reshape(1, -1)
    return pl.pallas_call(
        _merge_kernel,
        out_shape=jax.ShapeDtypeStruct((t, d), BF16),
        grid=(nj, t // tm),
        in_specs=[row(d), row(out_a.shape[1]), row(out_b.shape[1]), row(out_c.shape[1]),
                  gate_w(0), gate_w(1), gate_w(2), gate_b(0), gate_b(1), gate_b(2),
                  br_w(wa_b.shape[0]), br_w(wb_b.shape[0]), br_w(wc_b.shape[0])],
        out_specs=pl.BlockSpec((tm, tn), lambda j, i: (i, j)),
        compiler_params=_cparams(("arbitrary",) * 2),
        name="branch_merge",
    )(hb, out_a, out_b, out_c, w_gate_b, w_gate_b, w_gate_b, bg, bg, bg, wa_b, wb_b, wc_b)


def _layer_norm_rows(y, g, b):
    mu = jnp.mean(y, axis=-1, keepdims=True)
    yc = y - mu
    var = jnp.mean(yc * yc, axis=-1, keepdims=True)
    return yc * lax.rsqrt(var + LN_EPS) * g + b


def _route(lt):
    tm = lt.shape[1]
    gl = lt[0:N_GROUPS]
    gmax = jnp.max(gl, axis=0, keepdims=True)
    ge = jnp.exp(gl - gmax)
    gp = ge / jnp.sum(ge, axis=0, keepdims=True)
    g_val = jnp.max(gp, axis=0, keepdims=True)
    row = lax.broadcasted_iota(jnp.int32, (N_GROUPS, tm), 0)
    g_sel = jnp.min(jnp.where(gp == g_val, row, N_GROUPS), axis=0, keepdims=True)
    el = jnp.zeros((EXPERTS_PER_GROUP, tm), F32)
    for g in range(N_GROUPS):
        lo = 8 + g * EXPERTS_PER_GROUP
        el = jnp.where(g_sel == g, lt[lo:lo + EXPERTS_PER_GROUP], el)
    v0 = jnp.max(el, axis=0, keepdims=True)
    i0 = jnp.min(jnp.where(el == v0, row, EXPERTS_PER_GROUP), axis=0, keepdims=True)
    el1 = jnp.where(row == i0, -jnp.inf, el)
    v1 = jnp.max(el1, axis=0, keepdims=True)
    i1 = jnp.min(jnp.where(el1 == v1, row, EXPERTS_PER_GROUP), axis=0, keepdims=True)
    e1 = jnp.exp(v1 - v0)
    den = 1.0 + e1
    w0 = 1.0 / den * g_val
    w1 = e1 / den * g_val
    id0 = (g_sel * EXPERTS_PER_GROUP + i0).astype(F32)
    id1 = (g_sel * EXPERTS_PER_GROUP + i1).astype(F32)
    return id0, id1, w0, w1


def _outproj_ln_route_kernel(m_ref, w_ref, h_ref, g_ref, b_ref, wr_ref, hf_ref, rt_ref, *, alpha):
    mix = jnp.dot(m_ref[...], w_ref[...], preferred_element_type=F32)
    h1 = _layer_norm_rows(alpha * h_ref[...] + mix, g_ref[...], b_ref[...])
    hf_ref[...] = h1
    lt = lax.dot_general(wr_ref[...], h1, (((1,), (1,)), ((), ())),
                         precision=lax.Precision.HIGHEST, preferred_element_type=F32)
    id0, id1, w0, w1 = _route(lt)
    rt_ref[...] = jnp.concatenate([id0, id1, w0, w1, jnp.zeros((4, lt.shape[1]), F32)], axis=0)


def _outproj_ln_route(merged, w_out_b, h, ln_g, ln_b, w_router_t, alpha):
    t, d = h.shape
    tm = 512
    return pl.pallas_call(
        functools.partial(_outproj_ln_route_kernel, alpha=alpha),
        out_shape=(jax.ShapeDtypeStruct((t, d), F32), jax.ShapeDtypeStruct((8, t), F32)),
        grid=(t // tm,),
        in_specs=[pl.BlockSpec((tm, d), lambda i: (i, 0)),
                  pl.BlockSpec((d, d), lambda i: (0, 0)),
                  pl.BlockSpec((tm, d), lambda i: (i, 0)),
                  pl.BlockSpec((1, d), lambda i: (0, 0)),
                  pl.BlockSpec((1, d), lambda i: (0, 0)),
                  pl.BlockSpec((32, d), lambda i: (0, 0))],
        out_specs=(pl.BlockSpec((tm, d), lambda i: (i, 0)),
                   pl.BlockSpec((8, tm), lambda i: (0, i))),
        compiler_params=_cparams(("arbitrary",)),
        name="outproj_ln1_route",
    )(merged, w_out_b, h, ln_g.reshape(1, d), ln_b.reshape(1, d), w_router_t)


MOE_TILE = 256
GATHER_TILE = 512
DMA_UNROLL = 8


def _dispatch_plan(rt, t):
    ids = rt[0:2].astype(jnp.int32).reshape(-1)
    tok = jnp.tile(jnp.arange(t, dtype=jnp.int32), 2)
    onehot = (ids[:, None] == jnp.arange(N_EXPERTS, dtype=jnp.int32)[None, :]).astype(jnp.int32)
    csum = jnp.cumsum(onehot, axis=0)
    counts = csum[-1]
    rank = jnp.take_along_axis(csum, ids[:, None], axis=1)[:, 0] - 1
    padded = (counts + MOE_TILE - 1) // MOE_TILE * MOE_TILE
    ends = jnp.cumsum(padded)
    pos = (ends - padded)[ids] + rank
    p_rows = 2 * t + N_EXPERTS * MOE_TILE
    src = jnp.zeros((p_rows,), jnp.int32).at[pos].set(tok)
    n_tiles = p_rows // MOE_TILE
    n_used = (ends[-1] // MOE_TILE).astype(jnp.int32)
    tile_idx = jnp.arange(n_tiles, dtype=jnp.int32)
    tile_start = jnp.minimum(tile_idx, n_used - 1) * MOE_TILE
    tile_e = jnp.sum((ends[None, :] <= tile_start[:, None]).astype(jnp.int32), axis=1)
    tile_e = jnp.minimum(tile_e, N_EXPERTS - 1)
    w_cols = rt[2:4].reshape(2, t, 1)
    return src, pos.astype(jnp.int32), w_cols, tile_e, n_used.reshape(1)


def _gather_rows_kernel(idx_ref, x_hbm, o_ref, buf, sem):
    i = pl.program_id(0)

    def issue_tile(tile, slot):
        def issue(c, carry):
            base = pl.multiple_of(c * DMA_UNROLL, DMA_UNROLL)
            dst = buf.at[slot, pl.ds(base, DMA_UNROLL)]
            for j in range(DMA_UNROLL):
                row = idx_ref[tile * GATHER_TILE + base + j]
                pltpu.make_async_copy(x_hbm.at[pl.ds(row, 1)], dst.at[pl.ds(j, 1)], sem.at[slot]).start()
            return carry

        lax.fori_loop(0, GATHER_TILE // DMA_UNROLL, issue, 0)

    @pl.when(i == 0)
    def _():
        issue_tile(0, 0)

    @pl.when(i + 1 < pl.num_programs(0))
    def _():
        issue_tile(i + 1, (i + 1) % 2)

    slot = i % 2
    pltpu.make_async_copy(x_hbm.at[pl.ds(0, GATHER_TILE)], buf.at[slot], sem.at[slot]).wait()
    o_ref[...] = buf[slot].astype(BF16)


def _gather_rows(idx, x):
    p = idx.shape[0]
    d = x.shape[1]
    return pl.pallas_call(
        _gather_rows_kernel,
        out_shape=jax.ShapeDtypeStruct((p, d), BF16),
        grid_spec=pltpu.PrefetchScalarGridSpec(
            num_scalar_prefetch=1, grid=(p // GATHER_TILE,),
            in_specs=[pl.BlockSpec(memory_space=pl.ANY)],
            out_specs=pl.BlockSpec((GATHER_TILE, d), lambda i, idx: (i, 0)),
            scratch_shapes=[pltpu.VMEM((2, GATHER_TILE, d), x.dtype), pltpu.SemaphoreType.DMA((2,))]),
        compiler_params=_cparams(("arbitrary",)),
        name="moe_gather",
    )(idx, x)


def _expert_mlp_kernel(te_ref, nu_ref, x_ref, wg_ref, wu_ref, wd_ref, y_ref):
    @pl.when(pl.program_id(0) < nu_ref[0])
    def _():
        x = x_ref[...]
        a = jnp.dot(x, wg_ref[0], preferred_element_type=F32)
        u = jnp.dot(x, wu_ref[0], preferred_element_type=F32)
        hid = (jax.nn.silu(a) * u).astype(BF16)
        y_ref[...] = jnp.dot(hid, wd_ref[0], preferred_element_type=F32)

    @pl.when(pl.program_id(0) >= nu_ref[0])
    def _():
        y_ref[...] = jnp.zeros_like(y_ref)


def _expert_mlp(x_sorted, tile_e, n_used, wg_b, wu_b, wd_b):
    p, d = x_sorted.shape
    de = wg_b.shape[2]
    row = lambda i, te, nu: (jnp.minimum(i, nu[0] - 1), 0)
    exp = lambda i, te, nu: (te[i], 0, 0)
    return pl.pallas_call(
        _expert_mlp_kernel,
        out_shape=jax.ShapeDtypeStruct((p, d), F32),
        grid_spec=pltpu.PrefetchScalarGridSpec(
            num_scalar_prefetch=2, grid=(p // MOE_TILE,),
            in_specs=[pl.BlockSpec((MOE_TILE, d), row),
                      pl.BlockSpec((1, d, de), exp),
                      pl.BlockSpec((1, d, de), exp),
                      pl.BlockSpec((1, de, d), exp)],
            out_specs=pl.BlockSpec((MOE_TILE, d), lambda i, te, nu: (i, 0))),
        compiler_params=_cparams(("arbitrary",)),
        name="moe_expert_mlp",
    )(tile_e, n_used, x_sorted, wg_b, wu_b, wd_b)


def _combine_ln_kernel(pos_ref, y_hbm, h_ref, w_ref, g_ref, b_ref, of_ref, ob_ref, buf, sem, *, alpha, t):
    i = pl.program_id(0)
    tm = h_ref.shape[0]

    def issue_tile(tile, slot):
        def issue(c, carry):
            base = pl.multiple_of(c * DMA_UNROLL, DMA_UNROLL)
            for k in range(2):
                dst = buf.at[slot, k, pl.ds(base, DMA_UNROLL)]
                for j in range(DMA_UNROLL):
                    row = pos_ref[k * t + tile * tm + base + j]
                    pltpu.make_async_copy(y_hbm.at[pl.ds(row, 1)], dst.at[pl.ds(j, 1)],
                                          sem.at[slot]).start()
            return carry

        lax.fori_loop(0, tm // DMA_UNROLL, issue, 0)

    @pl.when(i == 0)
    def _():
        issue_tile(0, 0)

    @pl.when(i + 1 < pl.num_programs(0))
    def _():
        issue_tile(i + 1, (i + 1) % 2)

    slot = i % 2
    for k in range(2):
        pltpu.make_async_copy(y_hbm.at[pl.ds(0, tm)], buf.at[slot, k], sem.at[slot]).wait()
    ffn = w_ref[0] * buf[slot, 0] + w_ref[1] * buf[slot, 1]
    out = _layer_norm_rows(alpha * h_ref[...] + ffn, g_ref[...], b_ref[...])
    of_ref[...] = out
    ob_ref[...] = out.astype(BF16)


def _combine_ln(pos, w_cols, y_sorted, h1, ln_g, ln_b, alpha):
    t, d = h1.shape
    tm = GATHER_TILE
    return pl.pallas_call(
        functools.partial(_combine_ln_kernel, alpha=alpha, t=t),
        out_shape=(jax.ShapeDtypeStruct((t, d), F32), jax.ShapeDtypeStruct((t, d), BF16)),
        grid_spec=pltpu.PrefetchScalarGridSpec(
            num_scalar_prefetch=1, grid=(t // tm,),
            in_specs=[pl.BlockSpec(memory_space=pl.ANY),
                      pl.BlockSpec((tm, d), lambda i, pos: (i, 0)),
                      pl.BlockSpec((2, tm, 1), lambda i, pos: (0, i, 0)),
                      pl.BlockSpec((1, d), lambda i, pos: (0, 0)),
                      pl.BlockSpec((1, d), lambda i, pos: (0, 0))],
            out_specs=(pl.BlockSpec((tm, d), lambda i, pos: (i, 0)),
                       pl.BlockSpec((tm, d), lambda i, pos: (i, 0))),
            scratch_shapes=[pltpu.VMEM((2, 2, tm, d), F32), pltpu.SemaphoreType.DMA((2,))]),
        compiler_params=_cparams(("arbitrary",)),
        name="moe_combine_ln2",
    )(pos, y_sorted, h1, w_cols, ln_g.reshape(1, d), ln_b.reshape(1, d))


def _moe(h1f, rt, wg_b, wu_b, wd_b, ln_g, ln_b, alpha):
    t = h1f.shape[0]
    src, pos, w_cols, tile_e, n_used = _dispatch_plan(rt, t)
    x_sorted = _gather_rows(src, h1f)
    y_sorted = _expert_mlp(x_sorted, tile_e, n_used, wg_b, wu_b, wd_b)
    return _combine_ln(pos, w_cols, y_sorted, h1f, ln_g, ln_b, alpha)


def kernel(x, positions, w_in, w_gate, b_gate, w_branch_a, w_branch_b, w_branch_c, w_out, sinks,
           ln1_g, ln1_b, w_router_group, w_router_expert, w_exp_gate, w_exp_up, w_exp_down,
           ln2_g, ln2_b):
    batch, seq, d = x.shape
    depth = w_in.shape[0]
    t = batch * seq
    alpha = (2 * depth) ** 0.25
    cc, ss = _rope_tables(positions)
    hf = x.reshape(t, d)
    hb = hf.astype(BF16)
    for l in range(depth):
        qkv = _qkv_proj(hb, w_in[l].astype(BF16), cc, ss)
        out_a = _dilated_attention(qkv, batch, seq)
        out_b = _moba_attention(qkv, batch, seq)
        out_c = _swa_attention(qkv, sinks[l], batch, seq)
        merged = _merge(hb, out_a, out_b, out_c, w_gate[l].astype(BF16), b_gate[l],
                        w_branch_a[l].astype(BF16), w_branch_b[l].astype(BF16),
                        w_branch_c[l].astype(BF16))
        w_router_t = jnp.zeros((32, d), F32)
        w_router_t = w_router_t.at[0:N_GROUPS].set(w_router_group[l].T)
        w_router_t = w_router_t.at[8:8 + N_EXPERTS].set(w_router_expert[l].T)
        h1f, rt = _outproj_ln_route(merged, w_out[l].astype(BF16), hf, ln1_g[l], ln1_b[l],
                                       w_router_t, alpha)
        hf, hb = _moe(h1f, rt, w_exp_gate[l].astype(BF16), w_exp_up[l].astype(BF16),
                      w_exp_down[l].astype(BF16), ln2_g[l], ln2_b[l], alpha)
    return hf.reshape(batch, seq, d)
```

```python
import functools

import jax
import jax.numpy as jnp
from jax import lax
from jax.experimental import pallas as pl
from jax.experimental.pallas import tpu as pltpu

F32 = jnp.float32
BF16 = jnp.bfloat16

HEAD_DIM = 128
ATTN_SCALE = HEAD_DIM ** -0.5
ROPE_THETA = 10000.0
A_PATTERNS = ((128, 1), (512, 4), (2048, 16))
A_HEADS_PER_GROUP = 2
A_HEADS = A_HEADS_PER_GROUP * len(A_PATTERNS)
B_HEADS = 6
MOBA_BLOCK = 256
MOBA_TOPK = 3
C_Q_HEADS = 8
C_KV_HEADS = 2
C_GROUP = C_Q_HEADS // C_KV_HEADS
C_WINDOW = 128
BAND = 128
N_Q_HEADS = A_HEADS + B_HEADS + C_Q_HEADS
N_KV_HEADS = A_HEADS + B_HEADS + C_KV_HEADS
N_GROUPS = 4
EXPERTS_PER_GROUP = 4
N_EXPERTS = N_GROUPS * EXPERTS_PER_GROUP
LN_EPS = 1e-5
NEG = -1e30

V7X_VMEM_BYTES = 64 * 1024 * 1024
VMEM_LIMIT = V7X_VMEM_BYTES - 8 * 1024 * 1024


def _cparams(sem, vmem=VMEM_LIMIT):
    return pltpu.CompilerParams(dimension_semantics=sem, vmem_limit_bytes=vmem)


ROPE_SLOT_Q, ROPE_SLOT_K, ROPE_SLOT_V = 0, 1, 2


def _rope_table_kernel(pos_ref, inv_ref, cc_ref, ss_ref):
    ang = pos_ref[...].astype(F32) * inv_ref[...]
    lane = lax.broadcasted_iota(jnp.int32, ang.shape, 1)
    s = jnp.sin(ang)
    c = jnp.cos(ang)
    s = jnp.where(lane < HEAD_DIM // 2, -s, s)
    cc_ref[ROPE_SLOT_Q] = c * ATTN_SCALE
    ss_ref[ROPE_SLOT_Q] = s * ATTN_SCALE
    cc_ref[ROPE_SLOT_K] = c
    ss_ref[ROPE_SLOT_K] = s
    cc_ref[ROPE_SLOT_V] = jnp.ones_like(c)
    ss_ref[ROPE_SLOT_V] = jnp.zeros_like(s)


def _rope_tables(positions):
    t = positions.size
    tm = 2048
    inv = ROPE_THETA ** (-jnp.arange(0, HEAD_DIM, 2, dtype=F32) / HEAD_DIM)
    inv2 = jnp.concatenate([inv, inv]).reshape(1, HEAD_DIM)
    pos = positions.reshape(t, 1)
    return pl.pallas_call(
        _rope_table_kernel,
        out_shape=(jax.ShapeDtypeStruct((3, t, HEAD_DIM), F32),) * 2,
        grid=(t // tm,),
        in_specs=[pl.BlockSpec((tm, 1), lambda i: (i, 0)),
                  pl.BlockSpec((1, HEAD_DIM), lambda i: (0, 0))],
        out_specs=(pl.BlockSpec((3, tm, HEAD_DIM), lambda i: (0, i, 0)),) * 2,
        compiler_params=_cparams(("arbitrary",)),
        name="rope_tables",
    )(pos, inv2)


def _qkv_kernel(x_ref, w_ref, cc_ref, ss_ref, o_ref, *, heads_per_tile):
    j = pl.program_id(1)
    acc = jnp.dot(x_ref[...], w_ref[...], preferred_element_type=F32)
    for hh in range(heads_per_tile):
        head = j * heads_per_tile + hh
        slot = (head >= N_Q_HEADS).astype(jnp.int32) + (head >= N_Q_HEADS + N_KV_HEADS).astype(jnp.int32)
        t = acc[:, hh * HEAD_DIM:(hh + 1) * HEAD_DIM]
        r = t * cc_ref[slot] + pltpu.roll(t, HEAD_DIM // 2, 1) * ss_ref[slot]
        o_ref[hh] = r.astype(BF16)


def _qkv_proj(hb, w_in_b, layer, cc, ss):
    t, d = hb.shape
    n = w_in_b.shape[2]
    tm, tn = 1024, 1024
    hpt = tn // HEAD_DIM
    return pl.pallas_call(
        functools.partial(_qkv_kernel, heads_per_tile=hpt),
        out_shape=jax.ShapeDtypeStruct((n // HEAD_DIM, t, HEAD_DIM), BF16),
        grid=(t // tm, n // tn),
        in_specs=[pl.BlockSpec((tm, d), lambda i, j: (i, 0)),
                  pl.BlockSpec((None, d, tn), lambda i, j: (layer, 0, j)),
                  pl.BlockSpec((3, tm, HEAD_DIM), lambda i, j: (0, i, 0)),
                  pl.BlockSpec((3, tm, HEAD_DIM), lambda i, j: (0, i, 0))],
        out_specs=pl.BlockSpec((hpt, tm, HEAD_DIM), lambda i, j: (j, i, 0)),
        compiler_params=_cparams(("arbitrary", "arbitrary")),
        name="qkv_rope",
    )(hb, w_in_b, cc, ss)


def _band_bias(max_dist):
    qi = lax.broadcasted_iota(jnp.int32, (BAND, 2 * BAND), 0)
    kj = lax.broadcasted_iota(jnp.int32, (BAND, 2 * BAND), 1)
    dist = qi + BAND - kj
    valid = (dist >= 0) & (dist <= max_dist)
    bias = jnp.where(valid, 0.0, NEG).astype(F32)
    bias_first = jnp.where(valid & (kj >= BAND), 0.0, NEG).astype(F32)
    return bias, bias_first


def _band_block(qb, kwin, vwin, bias):
    g = qb.shape[0] // BAND
    s = lax.dot_general(qb, kwin, (((1,), (1,)), ((), ())), preferred_element_type=F32)
    s = s.reshape(g, BAND, 2 * BAND) + bias[None]
    m = jnp.max(s, axis=-1, keepdims=True)
    p = jnp.exp(s - m)
    l = jnp.sum(p, axis=-1, keepdims=True)
    o = jnp.dot(p.reshape(g * BAND, 2 * BAND).astype(BF16), vwin, preferred_element_type=F32)
    o = o.reshape(g, BAND, HEAD_DIM) / l
    return o, m + jnp.log(l)


def _swa_kernel(sink_ref, q_ref, k_ref, kh_ref, v_ref, vh_ref, o_ref, *, tq):
    kv = pl.program_id(1)
    t = pl.program_id(2)
    bias, bias_first = _band_bias(C_WINDOW - 1)
    for blk in range(tq // BAND):
        qb = q_ref[:, blk * BAND:(blk + 1) * BAND, :].reshape(C_GROUP * BAND, HEAD_DIM)
        if blk == 0:
            kwin = jnp.concatenate([kh_ref[0], k_ref[0, :BAND]], axis=0)
            vwin = jnp.concatenate([vh_ref[0], v_ref[0, :BAND]], axis=0)
            b = jnp.where(t == 0, bias_first, bias)
        else:
            kwin = k_ref[0, (blk - 1) * BAND:(blk + 1) * BAND]
            vwin = v_ref[0, (blk - 1) * BAND:(blk + 1) * BAND]
            b = bias
        o, lse = _band_block(qb, kwin, vwin, b)
        for g in range(C_GROUP):
            keep = jax.nn.sigmoid(lse[g] - sink_ref[kv * C_GROUP + g])
            o_ref[blk * BAND:(blk + 1) * BAND, g * HEAD_DIM:(g + 1) * HEAD_DIM] = (o[g] * keep).astype(BF16)


def _swa_attention(qkv, sinks, batch, seq):
    tq = 512
    nt = seq // tq
    q0 = A_HEADS + B_HEADS
    k0 = N_Q_HEADS + A_HEADS + B_HEADS
    v0 = N_Q_HEADS + N_KV_HEADS + A_HEADS + B_HEADS
    rb = tq // BAND

    def halo(b, kv, t):
        return jnp.maximum(b * (seq // BAND) + t * rb - 1, 0)

    return pl.pallas_call(
        functools.partial(_swa_kernel, tq=tq),
        out_shape=jax.ShapeDtypeStruct((batch * seq, C_Q_HEADS * HEAD_DIM), BF16),
        grid=(batch, C_KV_HEADS, nt),
        in_specs=[
            pl.BlockSpec(memory_space=pltpu.SMEM),
            pl.BlockSpec((C_GROUP, tq, HEAD_DIM), lambda b, kv, t: (q0 // C_GROUP + kv, b * nt + t, 0)),
            pl.BlockSpec((1, tq, HEAD_DIM), lambda b, kv, t: (k0 + kv, b * nt + t, 0)),
            pl.BlockSpec((1, BAND, HEAD_DIM), lambda b, kv, t: (k0 + kv, halo(b, kv, t), 0)),
            pl.BlockSpec((1, tq, HEAD_DIM), lambda b, kv, t: (v0 + kv, b * nt + t, 0)),
            pl.BlockSpec((1, BAND, HEAD_DIM), lambda b, kv, t: (v0 + kv, halo(b, kv, t), 0)),
        ],
        out_specs=pl.BlockSpec((tq, C_GROUP * HEAD_DIM), lambda b, kv, t: (b * nt + t, kv)),
        compiler_params=_cparams(("arbitrary",) * 3),
        name="swa_sink",
    )(sinks, qkv, qkv, qkv, qkv, qkv)


def _dilated_kernel(*refs, seq):
    ng = len(A_PATTERNS)
    in_refs = refs[:3 * ng]
    o_ref = refs[3 * ng]
    qf, kf, vf, og, lg = refs[3 * ng + 1:]
    for g, (window, dil) in enumerate(A_PATTERNS):
        q_ref, k_ref, v_ref = in_refs[3 * g:3 * g + 3]
        qf[...] = q_ref[0].astype(F32)
        kf[...] = k_ref[0].astype(F32)
        vf[...] = v_ref[0].astype(F32)
        nblk = seq // dil // BAND
        bias, bias_first = _band_bias(window // dil)

        for r in range(dil):
            for n in range(nblk):
                own = pl.ds(r + dil * BAND * n, BAND, stride=dil)
                prev = pl.ds(r + dil * BAND * max(n - 1, 0), BAND, stride=dil)
                qb = qf[own, :].astype(BF16)
                kwin = jnp.concatenate([kf[prev, :], kf[own, :]], axis=0).astype(BF16)
                vwin = jnp.concatenate([vf[prev, :], vf[own, :]], axis=0).astype(BF16)
                o, lse = _band_block(qb, kwin, vwin, bias_first if n == 0 else bias)
                og[g, own, :] = o[0]
                lg[g, own, :] = jnp.broadcast_to(lse[0], (BAND, HEAD_DIM))

    rows = 256

    def combine(c, carry):
        sl = pl.ds(pl.multiple_of(c * rows, rows), rows)
        ls = [lg[g, sl, :] for g in range(ng)]
        m = functools.reduce(jnp.maximum, ls)
        es = [jnp.exp(l - m) for l in ls]
        num = functools.reduce(jnp.add, [e * og[g, sl, :] for g, e in enumerate(es)])
        o_ref[sl, :] = (num / functools.reduce(jnp.add, es)).astype(BF16)
        return carry

    lax.fori_loop(0, seq // rows, combine, 0)


def _dilated_attention(qkv, batch, seq):
    ng = len(A_PATTERNS)
    k0 = N_Q_HEADS
    v0 = N_Q_HEADS + N_KV_HEADS
    in_specs = []
    for g in range(ng):
        for base in (0, k0, v0):
            in_specs.append(pl.BlockSpec(
                (1, seq, HEAD_DIM),
                lambda b, j, h=base + g * A_HEADS_PER_GROUP: (h + j, b, 0)))
    return pl.pallas_call(
        functools.partial(_dilated_kernel, seq=seq),
        out_shape=jax.ShapeDtypeStruct((batch * seq, A_HEADS_PER_GROUP * HEAD_DIM), BF16),
        grid=(batch, A_HEADS_PER_GROUP),
        in_specs=in_specs,
        out_specs=pl.BlockSpec((seq, HEAD_DIM), lambda b, j: (b, j)),
        scratch_shapes=[pltpu.VMEM((seq, HEAD_DIM), F32)] * 3
        + [pltpu.VMEM((ng, seq, HEAD_DIM), F32)] * 2,
        compiler_params=_cparams(("arbitrary",) * 2),
        name="dilated_attn",
    )(*([qkv] * (3 * ng)))


MOBA_TQ = 1024


def _online_update(state, s, v):
    mc = jnp.max(s, axis=1, keepdims=True)
    if state is None:
        p = jnp.exp(s - mc)
        return mc, jnp.sum(p, axis=1, keepdims=True), jnp.dot(p.astype(BF16), v, preferred_element_type=F32)
    m, l, acc = state
    m_new = jnp.maximum(m, mc)
    alpha = jnp.exp(m - m_new)
    p = jnp.exp(s - m_new)
    l_new = alpha * l + jnp.sum(p, axis=1, keepdims=True)
    acc_new = alpha * acc + jnp.dot(p.astype(BF16), v, preferred_element_type=F32)
    return m_new, l_new, acc_new


def _moba_kernel(q_ref, k_ref, v_ref, o_ref, kaug, kmean, *, seq):
    t = pl.program_id(2)
    nblk = seq // MOBA_BLOCK
    blk = MOBA_BLOCK
    tq = MOBA_TQ
    nb = tq // blk

    @pl.when(t == 0)
    def _():
        kaug[:, :HEAD_DIM] = k_ref[0]
        row = lax.broadcasted_iota(jnp.int32, (seq, HEAD_DIM), 0)
        lane = lax.broadcasted_iota(jnp.int32, (seq, HEAD_DIM), 1)
        kaug[:, HEAD_DIM:] = jnp.where(row // blk == lane, 1.0, 0.0).astype(BF16)
        kmean[...] = jnp.zeros_like(kmean)
        for n in range(nblk):
            kb = k_ref[0, n * blk:(n + 1) * blk, :].astype(F32)
            kmean[n:n + 1, :] = jnp.mean(kb, axis=0, keepdims=True)

    q = q_ref[0]
    km = kmean[...]
    km_hi = km.astype(BF16)
    km_lo = (km - km_hi.astype(F32)).astype(BF16)
    nt = (((1,), (1,)), ((), ()))
    gate = (lax.dot_general(km_hi, q, nt, preferred_element_type=F32)
            + lax.dot_general(km_lo, q, nt, preferred_element_type=F32))
    nrow = -(-nblk // 8) * 8
    row = lax.broadcasted_iota(jnp.int32, (nrow, tq), 0).astype(F32)
    own = (t * nb + lax.broadcasted_iota(jnp.int32, (nrow, tq), 1) // blk).astype(F32)
    g = jnp.where(row < own, gate[:nrow], NEG)
    sel = jnp.zeros((nrow, tq), jnp.bool_)
    for _ in range(MOBA_TOPK):
        mx = jnp.max(g, axis=0, keepdims=True)
        first = jnp.min(jnp.where(g == mx, row, float(nrow)), axis=0, keepdims=True)
        pick = row == first
        sel = sel | pick
        g = jnp.where(pick, NEG, g)
    visible = (sel & (row < own)) | (row == own)
    bias_t = jnp.concatenate([jnp.where(visible, 0.0, NEG), jnp.zeros((HEAD_DIM - nrow, tq), F32)], axis=0)
    qaug = jnp.concatenate([q, bias_t.T.astype(BF16)], axis=1)

    ri = lax.broadcasted_iota(jnp.int32, (blk, blk), 0)
    ci = lax.broadcasted_iota(jnp.int32, (blk, blk), 1)
    causal = ri >= ci

    for c in range(seq // tq):
        @pl.when(t == c)
        def _(c=c):
            state = None
            for cc in range(c):
                keys = slice(cc * tq, (cc + 1) * tq)
                s = lax.dot_general(qaug, kaug[keys, :], nt, preferred_element_type=F32)
                state = _online_update(state, s, v_ref[0, keys, :])
            n_split = 2 if c == 0 else 1
            hb = nb // n_split
            parts = []
            for h in range(n_split):
                rows = slice(h * hb * blk, (h + 1) * hb * blk)
                keys = slice(c * tq, c * tq + (h + 1) * hb * blk)
                s = lax.dot_general(qaug[rows], kaug[keys, :], nt, preferred_element_type=F32)
                s = jnp.concatenate(
                    [jnp.concatenate(
                        [jnp.where(causal, s[a * blk:(a + 1) * blk, e * blk:(e + 1) * blk], NEG)
                         if e == a + h * hb else s[a * blk:(a + 1) * blk, e * blk:(e + 1) * blk]
                         for e in range((h + 1) * hb)], axis=1) for a in range(hb)], axis=0)
                st = None if state is None else tuple(x[rows] for x in state)
                parts.append(_online_update(st, s, v_ref[0, keys, :]))
            m, l, acc = (jnp.concatenate([p[i] for p in parts], axis=0) for i in range(3))
            o_ref[...] = (acc / l).astype(BF16)


def _moba_attention(qkv, batch, seq):
    nt = seq // MOBA_TQ
    q0 = A_HEADS
    k0 = N_Q_HEADS + A_HEADS
    v0 = N_Q_HEADS + N_KV_HEADS + A_HEADS
    return pl.pallas_call(
        functools.partial(_moba_kernel, seq=seq),
        out_shape=jax.ShapeDtypeStruct((batch * seq, B_HEADS * HEAD_DIM), BF16),
        grid=(batch, B_HEADS, nt),
        in_specs=[
            pl.BlockSpec((1, MOBA_TQ, HEAD_DIM), lambda b, h, t: (q0 + h, b * nt + t, 0)),
            pl.BlockSpec((1, seq, HEAD_DIM), lambda b, h, t: (k0 + h, b, 0)),
            pl.BlockSpec((1, seq, HEAD_DIM), lambda b, h, t: (v0 + h, b, 0)),
        ],
        out_specs=pl.BlockSpec((MOBA_TQ, HEAD_DIM), lambda b, h, t: (b * nt + t, h)),
        scratch_shapes=[pltpu.VMEM((seq, 2 * HEAD_DIM), BF16), pltpu.VMEM((HEAD_DIM, HEAD_DIM), F32)],
        compiler_params=_cparams(("arbitrary",) * 3),
        name="moba_attn",
    )(qkv, qkv, qkv)


def _merge_kernel(h_ref, oa_ref, ob_ref, oc_ref, g0_ref, g1_ref, g2_ref, b0_ref, b1_ref, b2_ref,
                  wa_ref, wb_ref, wc_ref, o_ref):
    x = h_ref[...]
    acc = None
    for g_ref, b_ref, br_ref, w_ref in ((g0_ref, b0_ref, oa_ref, wa_ref),
                                        (g1_ref, b1_ref, ob_ref, wb_ref),
                                        (g2_ref, b2_ref, oc_ref, wc_ref)):
        gate = jax.nn.sigmoid(jnp.dot(x, g_ref[...], preferred_element_type=F32) + b_ref[...])
        term = gate * jnp.dot(br_ref[...], w_ref[...], preferred_element_type=F32)
        acc = term if acc is None else acc + term
    o_ref[...] = acc.astype(BF16)


def _merge(hb, out_a, out_b, out_c, w_gate_b, b_gate, wa_b, wb_b, wc_b, layer):
    t, d = hb.shape
    tm, tn = 1024, 512
    nj = d // tn
    row = lambda w: pl.BlockSpec((tm, w), lambda j, i: (i, 0))
    gate_w = lambda x: pl.BlockSpec((None, d, tn), lambda j, i, x=x: (layer, 0, x * nj + j))
    gate_b = lambda x: pl.BlockSpec((None, 1, tn), lambda j, i, x=x: (layer, 0, x * nj + j))
    br_w = lambda k: pl.BlockSpec((None, k, tn), lambda j, i: (layer, 0, j))
    bg = b_gate.reshape(b_gate.shape[0], 1, -1)
    return pl.pallas_call(
        _merge_kernel,
        out_shape=jax.ShapeDtypeStruct((t, d), BF16),
        grid=(nj, t // tm),
        in_specs=[row(d), row(out_a.shape[1]), row(out_b.shape[1]), row(out_c.shape[1]),
                  gate_w(0), gate_w(1), gate_w(2), gate_b(0), gate_b(1), gate_b(2),
                  br_w(wa_b.shape[1]), br_w(wb_b.shape[1]), br_w(wc_b.shape[1])],
        out_specs=pl.BlockSpec((tm, tn), lambda j, i: (i, j)),
        compiler_params=_cparams(("arbitrary",) * 2),
        name="branch_merge",
    )(hb, out_a, out_b, out_c, w_gate_b, w_gate_b, w_gate_b, bg, bg, bg, wa_b, wb_b, wc_b)


def _layer_norm_rows(y, g, b):
    mu = jnp.mean(y, axis=-1, keepdims=True)
    yc = y - mu
    var = jnp.mean(yc * yc, axis=-1, keepdims=True)
    return yc * lax.rsqrt(var + LN_EPS) * g + b


def _route(lt):
    tm = lt.shape[1]
    gl = lt[0:N_GROUPS]
    gmax = jnp.max(gl, axis=0, keepdims=True)
    ge = jnp.exp(gl - gmax)
    gp = ge / jnp.sum(ge, axis=0, keepdims=True)
    g_val = jnp.max(gp, axis=0, keepdims=True)
    row = lax.broadcasted_iota(jnp.int32, (N_GROUPS, tm), 0)
    g_sel = jnp.min(jnp.where(gp == g_val, row, N_GROUPS), axis=0, keepdims=True)
    el = jnp.zeros((EXPERTS_PER_GROUP, tm), F32)
    for g in range(N_GROUPS):
        lo = 8 + g * EXPERTS_PER_GROUP
        el = jnp.where(g_sel == g, lt[lo:lo + EXPERTS_PER_GROUP], el)
    v0 = jnp.max(el, axis=0, keepdims=True)
    i0 = jnp.min(jnp.where(el == v0, row, EXPERTS_PER_GROUP), axis=0, keepdims=True)
    el1 = jnp.where(row == i0, -jnp.inf, el)
    v1 = jnp.max(el1, axis=0, keepdims=True)
    i1 = jnp.min(jnp.where(el1 == v1, row, EXPERTS_PER_GROUP), axis=0, keepdims=True)
    e1 = jnp.exp(v1 - v0)
    den = 1.0 + e1
    w0 = 1.0 / den * g_val
    w1 = e1 / den * g_val
    id0 = (g_sel * EXPERTS_PER_GROUP + i0).astype(F32)
    id1 = (g_sel * EXPERTS_PER_GROUP + i1).astype(F32)
    return id0, id1, w0, w1


def _outproj_ln_route_kernel(m_ref, w_ref, h_ref, g_ref, b_ref, wr_ref, hf_ref, rt_ref, mix_buf,
                             wrh_ref, wrl_ref, *, alpha):
    @pl.when(pl.program_id(0) == 0)
    def _():
        mix_buf[...] = jnp.zeros_like(mix_buf)
        wr = wr_ref[...]
        wr_hi = wr.astype(BF16)
        wrh_ref[...] = wr_hi
        wrl_ref[...] = (wr - wr_hi.astype(F32)).astype(BF16)

    prev = mix_buf[...]
    mix_buf[...] = jnp.dot(m_ref[...], w_ref[...], preferred_element_type=F32)
    h1 = _layer_norm_rows(alpha * h_ref[...] + prev, g_ref[...], b_ref[...])
    hf_ref[...] = h1
    h_hi = h1.astype(BF16)
    h_lo = (h1 - h_hi.astype(F32)).astype(BF16)
    logits = (jnp.dot(h_hi, wrh_ref[...], preferred_element_type=F32)
              + jnp.dot(h_lo, wrh_ref[...], preferred_element_type=F32)
              + jnp.dot(h_hi, wrl_ref[...], preferred_element_type=F32))
    id0, id1, w0, w1 = _route(logits.T)
    rt_ref[...] = jnp.concatenate([id0, id1, w0, w1, jnp.zeros((4, logits.shape[0]), F32)], axis=0)


def _outproj_ln_route(merged, w_out_b, layer, h, ln_g, ln_b, w_router, alpha):
    t, d = h.shape
    tm = 512
    n = t // tm
    cur = lambda i: (jnp.minimum(i, n - 1), 0)
    prv = lambda i: (jnp.maximum(i - 1, 0), 0)
    return pl.pallas_call(
        functools.partial(_outproj_ln_route_kernel, alpha=alpha),
        out_shape=(jax.ShapeDtypeStruct((t, d), F32), jax.ShapeDtypeStruct((8, t), F32)),
        grid=(n + 1,),
        in_specs=[pl.BlockSpec((tm, d), cur),
                  pl.BlockSpec((None, d, d), lambda i: (layer, 0, 0)),
                  pl.BlockSpec((tm, d), prv),
                  pl.BlockSpec((1, d), lambda i: (0, 0)),
                  pl.BlockSpec((1, d), lambda i: (0, 0)),
                  pl.BlockSpec((d, HEAD_DIM), lambda i: (0, 0))],
        out_specs=(pl.BlockSpec((tm, d), prv),
                   pl.BlockSpec((8, tm), lambda i: (0, jnp.maximum(i - 1, 0)))),
        scratch_shapes=[pltpu.VMEM((tm, d), F32),
                        pltpu.VMEM((d, HEAD_DIM), BF16), pltpu.VMEM((d, HEAD_DIM), BF16)],
        compiler_params=_cparams(("arbitrary",)),
        name="outproj_ln1_route",
    )(merged, w_out_b, h, ln_g.reshape(1, d), ln_b.reshape(1, d), w_router)


MOE_TILE = 256
GATHER_TILE = 512
DMA_UNROLL = 8


def _dispatch_plan(rt, t):
    ids = rt[0:2].astype(jnp.int32).reshape(-1)
    tok = jnp.tile(jnp.arange(t, dtype=jnp.int32), 2)
    onehot = (ids[:, None] == jnp.arange(N_EXPERTS, dtype=jnp.int32)[None, :]).astype(jnp.int32)
    csum = jnp.cumsum(onehot, axis=0)
    counts = csum[-1]
    rank = jnp.take_along_axis(csum, ids[:, None], axis=1)[:, 0] - 1
    padded = (counts + MOE_TILE - 1) // MOE_TILE * MOE_TILE
    ends = jnp.cumsum(padded)
    pos = (ends - padded)[ids] + rank
    p_rows = 2 * t + N_EXPERTS * MOE_TILE
    src = jnp.zeros((p_rows,), jnp.int32).at[pos].set(tok)
    n_tiles = p_rows // MOE_TILE
    n_used = (ends[-1] // MOE_TILE).astype(jnp.int32)
    tile_idx = jnp.arange(n_tiles, dtype=jnp.int32)
    tile_start = jnp.minimum(tile_idx, n_used - 1) * MOE_TILE
    tile_e = jnp.sum((ends[None, :] <= tile_start[:, None]).astype(jnp.int32), axis=1)
    tile_e = jnp.minimum(tile_e, N_EXPERTS - 1)
    w_cols = rt[2:4].reshape(2, t, 1)
    return src, pos.astype(jnp.int32), w_cols, tile_e, n_used.reshape(1)


def _gather_rows_kernel(idx_ref, x_hbm, o_ref, buf, sem):
    i = pl.program_id(0)

    def issue_tile(tile, slot):
        def issue(c, carry):
            base = pl.multiple_of(c * DMA_UNROLL, DMA_UNROLL)
            dst = buf.at[slot, pl.ds(base, DMA_UNROLL)]
            for j in range(DMA_UNROLL):
                row = idx_ref[tile * GATHER_TILE + base + j]
                pltpu.make_async_copy(x_hbm.at[pl.ds(row, 1)], dst.at[pl.ds(j, 1)],
                                      sem.at[slot]).start(priority=j % 2)
            return carry

        lax.fori_loop(0, GATHER_TILE // DMA_UNROLL, issue, 0)

    @pl.when(i == 0)
    def _():
        issue_tile(0, 0)

    @pl.when(i + 1 < pl.num_programs(0))
    def _():
        issue_tile(i + 1, (i + 1) % 2)

    slot = i % 2
    pltpu.make_async_copy(x_hbm.at[pl.ds(0, GATHER_TILE)], buf.at[slot], sem.at[slot]).wait()
    o_ref[...] = buf[slot].astype(BF16)


def _gather_rows(idx, x):
    p = idx.shape[0]
    d = x.shape[1]
    return pl.pallas_call(
        _gather_rows_kernel,
        out_shape=jax.ShapeDtypeStruct((p, d), BF16),
        grid_spec=pltpu.PrefetchScalarGridSpec(
            num_scalar_prefetch=1, grid=(p // GATHER_TILE,),
            in_specs=[pl.BlockSpec(memory_space=pl.ANY)],
            out_specs=pl.BlockSpec((GATHER_TILE, d), lambda i, idx: (i, 0)),
            scratch_shapes=[pltpu.VMEM((2, GATHER_TILE, d), x.dtype), pltpu.SemaphoreType.DMA((2,))]),
        compiler_params=_cparams(("arbitrary",)),
        name="moe_gather",
    )(idx, x)


def _expert_mlp_kernel(te_ref, nu_ref, x_ref, wg_ref, wu_ref, wd_ref, y_ref):
    @pl.when(pl.program_id(0) < nu_ref[0])
    def _():
        x = x_ref[...]
        a = jnp.dot(x, wg_ref[0], preferred_element_type=F32)
        u = jnp.dot(x, wu_ref[0], preferred_element_type=F32)
        hid = (jax.nn.silu(a) * u).astype(BF16)
        y_ref[...] = jnp.dot(hid, wd_ref[0], preferred_element_type=F32)

    @pl.when(pl.program_id(0) >= nu_ref[0])
    def _():
        y_ref[...] = jnp.zeros_like(y_ref)


def _expert_mlp(x_sorted, tile_e, n_used, wg_b, wu_b, wd_b, layer):
    p, d = x_sorted.shape
    de = wg_b.shape[2]
    row = lambda i, te, nu: (jnp.minimum(i, nu[0] - 1), 0)
    exp = lambda i, te, nu: (layer * N_EXPERTS + te[i], 0, 0)
    return pl.pallas_call(
        _expert_mlp_kernel,
        out_shape=jax.ShapeDtypeStruct((p, d), F32),
        grid_spec=pltpu.PrefetchScalarGridSpec(
            num_scalar_prefetch=2, grid=(p // MOE_TILE,),
            in_specs=[pl.BlockSpec((MOE_TILE, d), row),
                      pl.BlockSpec((1, d, de), exp),
                      pl.BlockSpec((1, d, de), exp),
                      pl.BlockSpec((1, de, d), exp)],
            out_specs=pl.BlockSpec((MOE_TILE, d), lambda i, te, nu: (i, 0))),
        compiler_params=_cparams(("arbitrary",)),
        name="moe_expert_mlp",
    )(tile_e, n_used, x_sorted, wg_b, wu_b, wd_b)


def _combine_ln_kernel(pos_ref, y_hbm, h_ref, w_ref, g_ref, b_ref, of_ref, ob_ref, buf, sem, *, alpha, t):
    i = pl.program_id(0)
    tm = h_ref.shape[0]

    def issue_tile(tile, slot):
        def issue(c, carry):
            base = pl.multiple_of(c * DMA_UNROLL, DMA_UNROLL)
            for k in range(2):
                dst = buf.at[slot, k, pl.ds(base, DMA_UNROLL)]
                for j in range(DMA_UNROLL):
                    row = pos_ref[k * t + tile * tm + base + j]
                    pltpu.make_async_copy(y_hbm.at[pl.ds(row, 1)], dst.at[pl.ds(j, 1)],
                                          sem.at[slot]).start(priority=j % 2)
            return carry

        lax.fori_loop(0, tm // DMA_UNROLL, issue, 0)

    @pl.when(i == 0)
    def _():
        issue_tile(0, 0)

    @pl.when(i + 1 < pl.num_programs(0))
    def _():
        issue_tile(i + 1, (i + 1) % 2)

    slot = i % 2
    for k in range(2):
        pltpu.make_async_copy(y_hbm.at[pl.ds(0, tm)], buf.at[slot, k], sem.at[slot]).wait()
    ffn = w_ref[0] * buf[slot, 0] + w_ref[1] * buf[slot, 1]
    out = _layer_norm_rows(alpha * h_ref[...] + ffn, g_ref[...], b_ref[...])
    of_ref[...] = out
    ob_ref[...] = out.astype(BF16)


def _combine_ln(pos, w_cols, y_sorted, h1, ln_g, ln_b, alpha):
    t, d = h1.shape
    tm = GATHER_TILE
    return pl.pallas_call(
        functools.partial(_combine_ln_kernel, alpha=alpha, t=t),
        out_shape=(jax.ShapeDtypeStruct((t, d), F32), jax.ShapeDtypeStruct((t, d), BF16)),
        grid_spec=pltpu.PrefetchScalarGridSpec(
            num_scalar_prefetch=1, grid=(t // tm,),
            in_specs=[pl.BlockSpec(memory_space=pl.ANY),
                      pl.BlockSpec((tm, d), lambda i, pos: (i, 0)),
                      pl.BlockSpec((2, tm, 1), lambda i, pos: (0, i, 0)),
                      pl.BlockSpec((1, d), lambda i, pos: (0, 0)),
                      pl.BlockSpec((1, d), lambda i, pos: (0, 0))],
            out_specs=(pl.BlockSpec((tm, d), lambda i, pos: (i, 0)),
                       pl.BlockSpec((tm, d), lambda i, pos: (i, 0))),
            scratch_shapes=[pltpu.VMEM((2, 2, tm, d), F32), pltpu.SemaphoreType.DMA((2,))]),
        compiler_params=_cparams(("arbitrary",)),
        name="moe_combine_ln2",
    )(pos, y_sorted, h1, w_cols, ln_g.reshape(1, d), ln_b.reshape(1, d))


def _moe(h1f, rt, wg_b, wu_b, wd_b, layer, ln_g, ln_b, alpha):
    t = h1f.shape[0]
    src, pos, w_cols, tile_e, n_used = _dispatch_plan(rt, t)
    x_sorted = _gather_rows(src, h1f)
    y_sorted = _expert_mlp(x_sorted, tile_e, n_used, wg_b, wu_b, wd_b, layer)
    return _combine_ln(pos, w_cols, y_sorted, h1f, ln_g, ln_b, alpha)


def kernel(x, positions, w_in, w_gate, b_gate, w_branch_a, w_branch_b, w_branch_c, w_out, sinks,
           ln1_g, ln1_b, w_router_group, w_router_expert, w_exp_gate, w_exp_up, w_exp_down,
           ln2_g, ln2_b):
    batch, seq, d = x.shape
    depth = w_in.shape[0]
    t = batch * seq
    alpha = (2 * depth) ** 0.25
    cc, ss = _rope_tables(positions)
    hf = x.reshape(t, d)
    hb = hf.astype(BF16)
    w_in_b, w_gate_b, w_out_b = w_in.astype(BF16), w_gate.astype(BF16), w_out.astype(BF16)
    wa_b, wb_b, wc_b = w_branch_a.astype(BF16), w_branch_b.astype(BF16), w_branch_c.astype(BF16)
    de = w_exp_gate.shape[-1]
    wg_b = w_exp_gate.astype(BF16).reshape(depth * N_EXPERTS, d, de)
    wu_b = w_exp_up.astype(BF16).reshape(depth * N_EXPERTS, d, de)
    wd_b = w_exp_down.astype(BF16).reshape(depth * N_EXPERTS, de, d)
    for l in range(depth):
        qkv = _qkv_proj(hb, w_in_b, l, cc, ss)
        out_a = _dilated_attention(qkv, batch, seq)
        out_b = _moba_attention(qkv, batch, seq)
        out_c = _swa_attention(qkv, sinks[l], batch, seq)
        merged = _merge(hb, out_a, out_b, out_c, w_gate_b, b_gate, wa_b, wb_b, wc_b, l)
        w_router = jnp.zeros((d, HEAD_DIM), F32)
        w_router = w_router.at[:, 0:N_GROUPS].set(w_router_group[l])
        w_router = w_router.at[:, 8:8 + N_EXPERTS].set(w_router_expert[l])
        h1f, rt = _outproj_ln_route(merged, w_out_b, l, hf, ln1_g[l], ln1_b[l], w_router, alpha)
        hf, hb = _moe(h1f, rt, wg_b, wu_b, wd_b, l, ln2_g[l], ln2_b[l], alpha)
    return hf.reshape(batch, seq, d)
```

```python
import functools

import jax
import jax.numpy as jnp
from jax import lax
from jax.experimental import pallas as pl
from jax.experimental.pallas import tpu as pltpu

F32 = jnp.float32
BF16 = jnp.bfloat16

HEAD_DIM = 128
ATTN_SCALE = HEAD_DIM ** -0.5
ROPE_THETA = 10000.0
A_PATTERNS = ((128, 1), (512, 4), (2048, 16))
A_HEADS_PER_GROUP = 2
A_HEADS = A_HEADS_PER_GROUP * len(A_PATTERNS)
B_HEADS = 6
MOBA_BLOCK = 256
MOBA_TOPK = 3
C_Q_HEADS = 8
C_KV_HEADS = 2
C_GROUP = C_Q_HEADS // C_KV_HEADS
C_WINDOW = 128
BAND = 128
N_Q_HEADS = A_HEADS + B_HEADS + C_Q_HEADS
N_KV_HEADS = A_HEADS + B_HEADS + C_KV_HEADS
N_GROUPS = 4
EXPERTS_PER_GROUP = 4
N_EXPERTS = N_GROUPS * EXPERTS_PER_GROUP
LN_EPS = 1e-5
NEG = -1e30

V7X_VMEM_BYTES = 64 * 1024 * 1024
VMEM_LIMIT = V7X_VMEM_BYTES - 8 * 1024 * 1024


def _cparams(sem, vmem=VMEM_LIMIT):
    return pltpu.CompilerParams(dimension_semantics=sem, vmem_limit_bytes=vmem)


ROPE_SLOT_Q, ROPE_SLOT_K, ROPE_SLOT_V = 0, 1, 2


def _rope_table_kernel(pos_ref, inv_ref, cc_ref, ss_ref):
    ang = pos_ref[...].astype(F32) * inv_ref[...]
    lane = lax.broadcasted_iota(jnp.int32, ang.shape, 1)
    s = jnp.sin(ang)
    c = jnp.cos(ang)
    s = jnp.where(lane < HEAD_DIM // 2, -s, s)
    cc_ref[ROPE_SLOT_Q] = c * ATTN_SCALE
    ss_ref[ROPE_SLOT_Q] = s * ATTN_SCALE
    cc_ref[ROPE_SLOT_K] = c
    ss_ref[ROPE_SLOT_K] = s
    cc_ref[ROPE_SLOT_V] = jnp.ones_like(c)
    ss_ref[ROPE_SLOT_V] = jnp.zeros_like(s)


def _rope_tables(positions):
    t = positions.size
    tm = 2048
    inv = ROPE_THETA ** (-jnp.arange(0, HEAD_DIM, 2, dtype=F32) / HEAD_DIM)
    inv2 = jnp.concatenate([inv, inv]).reshape(1, HEAD_DIM)
    pos = positions.reshape(t, 1)
    return pl.pallas_call(
        _rope_table_kernel,
        out_shape=(jax.ShapeDtypeStruct((3, t, HEAD_DIM), F32),) * 2,
        grid=(t // tm,),
        in_specs=[pl.BlockSpec((tm, 1), lambda i: (i, 0)),
                  pl.BlockSpec((1, HEAD_DIM), lambda i: (0, 0))],
        out_specs=(pl.BlockSpec((3, tm, HEAD_DIM), lambda i: (0, i, 0)),) * 2,
        compiler_params=_cparams(("arbitrary",)),
        name="rope_tables",
    )(pos, inv2)


def _qkv_kernel(x_ref, w_ref, cc_ref, ss_ref, o_ref, *, heads_per_tile):
    j = pl.program_id(1)
    acc = jnp.dot(x_ref[...], w_ref[...], preferred_element_type=F32)
    for hh in range(heads_per_tile):
        head = j * heads_per_tile + hh
        slot = (head >= N_Q_HEADS).astype(jnp.int32) + (head >= N_Q_HEADS + N_KV_HEADS).astype(jnp.int32)
        t = acc[:, hh * HEAD_DIM:(hh + 1) * HEAD_DIM]
        r = t * cc_ref[slot] + pltpu.roll(t, HEAD_DIM // 2, 1) * ss_ref[slot]
        o_ref[hh] = r.astype(BF16)


def _qkv_proj(hb, w_in_b, layer, cc, ss):
    t, d = hb.shape
    n = w_in_b.shape[2]
    tm, tn = 1024, 1024
    hpt = tn // HEAD_DIM
    return pl.pallas_call(
        functools.partial(_qkv_kernel, heads_per_tile=hpt),
        out_shape=jax.ShapeDtypeStruct((n // HEAD_DIM, t, HEAD_DIM), BF16),
        grid=(t // tm, n // tn),
        in_specs=[pl.BlockSpec((tm, d), lambda i, j: (i, 0)),
                  pl.BlockSpec((None, d, tn), lambda i, j: (layer, 0, j)),
                  pl.BlockSpec((3, tm, HEAD_DIM), lambda i, j: (0, i, 0)),
                  pl.BlockSpec((3, tm, HEAD_DIM), lambda i, j: (0, i, 0))],
        out_specs=pl.BlockSpec((hpt, tm, HEAD_DIM), lambda i, j: (j, i, 0)),
        compiler_params=_cparams(("arbitrary", "arbitrary")),
        name="qkv_rope",
    )(hb, w_in_b, cc, ss)


def _band_bias(max_dist):
    qi = lax.broadcasted_iota(jnp.int32, (BAND, 2 * BAND), 0)
    kj = lax.broadcasted_iota(jnp.int32, (BAND, 2 * BAND), 1)
    dist = qi + BAND - kj
    valid = (dist >= 0) & (dist <= max_dist)
    bias = jnp.where(valid, 0.0, NEG).astype(F32)
    bias_first = jnp.where(valid & (kj >= BAND), 0.0, NEG).astype(F32)
    return bias, bias_first


def _band_block(qb, kwin, vwin, bias):
    g = qb.shape[0] // BAND
    s = lax.dot_general(qb, kwin, (((1,), (1,)), ((), ())), preferred_element_type=F32)
    s = s.reshape(g, BAND, 2 * BAND) + bias[None]
    m = jnp.max(s, axis=-1, keepdims=True)
    p = jnp.exp(s - m)
    l = jnp.sum(p, axis=-1, keepdims=True)
    o = jnp.dot(p.reshape(g * BAND, 2 * BAND).astype(BF16), vwin, preferred_element_type=F32)
    o = o.reshape(g, BAND, HEAD_DIM) / l
    return o, m + jnp.log(l)


def _swa_kernel(sink_ref, q_ref, k_ref, kh_ref, v_ref, vh_ref, o_ref, *, tq):
    kv = pl.program_id(1)
    t = pl.program_id(2)
    bias, bias_first = _band_bias(C_WINDOW - 1)
    for blk in range(tq // BAND):
        qb = q_ref[:, blk * BAND:(blk + 1) * BAND, :].reshape(C_GROUP * BAND, HEAD_DIM)
        if blk == 0:
            kwin = jnp.concatenate([kh_ref[0], k_ref[0, :BAND]], axis=0)
            vwin = jnp.concatenate([vh_ref[0], v_ref[0, :BAND]], axis=0)
            b = jnp.where(t == 0, bias_first, bias)
        else:
            kwin = k_ref[0, (blk - 1) * BAND:(blk + 1) * BAND]
            vwin = v_ref[0, (blk - 1) * BAND:(blk + 1) * BAND]
            b = bias
        o, lse = _band_block(qb, kwin, vwin, b)
        for g in range(C_GROUP):
            keep = jax.nn.sigmoid(lse[g] - sink_ref[kv * C_GROUP + g])
            o_ref[blk * BAND:(blk + 1) * BAND, g * HEAD_DIM:(g + 1) * HEAD_DIM] = (o[g] * keep).astype(BF16)


def _swa_attention(qkv, sinks, batch, seq):
    tq = 512
    nt = seq // tq
    q0 = A_HEADS + B_HEADS
    k0 = N_Q_HEADS + A_HEADS + B_HEADS
    v0 = N_Q_HEADS + N_KV_HEADS + A_HEADS + B_HEADS
    rb = tq // BAND

    def halo(b, kv, t):
        return jnp.maximum(b * (seq // BAND) + t * rb - 1, 0)

    return pl.pallas_call(
        functools.partial(_swa_kernel, tq=tq),
        out_shape=jax.ShapeDtypeStruct((batch * seq, C_Q_HEADS * HEAD_DIM), BF16),
        grid=(batch, C_KV_HEADS, nt),
        in_specs=[
            pl.BlockSpec(memory_space=pltpu.SMEM),
            pl.BlockSpec((C_GROUP, tq, HEAD_DIM), lambda b, kv, t: (q0 // C_GROUP + kv, b * nt + t, 0)),
            pl.BlockSpec((1, tq, HEAD_DIM), lambda b, kv, t: (k0 + kv, b * nt + t, 0)),
            pl.BlockSpec((1, BAND, HEAD_DIM), lambda b, kv, t: (k0 + kv, halo(b, kv, t), 0)),
            pl.BlockSpec((1, tq, HEAD_DIM), lambda b, kv, t: (v0 + kv, b * nt + t, 0)),
            pl.BlockSpec((1, BAND, HEAD_DIM), lambda b, kv, t: (v0 + kv, halo(b, kv, t), 0)),
        ],
        out_specs=pl.BlockSpec((tq, C_GROUP * HEAD_DIM), lambda b, kv, t: (b * nt + t, kv)),
        compiler_params=_cparams(("arbitrary",) * 3),
        name="swa_sink",
    )(sinks, qkv, qkv, qkv, qkv, qkv)


def _dilated_kernel(*refs, seq):
    ng = len(A_PATTERNS)
    in_refs = refs[:3 * ng]
    o_ref = refs[3 * ng]
    qf, kf, vf, og, lg = refs[3 * ng + 1:]
    for g, (window, dil) in enumerate(A_PATTERNS):
        q_ref, k_ref, v_ref = in_refs[3 * g:3 * g + 3]
        qf[...] = q_ref[0].astype(F32)
        kf[...] = k_ref[0].astype(F32)
        vf[...] = v_ref[0].astype(F32)
        nblk = seq // dil // BAND
        bias, bias_first = _band_bias(window // dil)

        for r in range(dil):
            for n in range(nblk):
                own = pl.ds(r + dil * BAND * n, BAND, stride=dil)
                prev = pl.ds(r + dil * BAND * max(n - 1, 0), BAND, stride=dil)
                qb = qf[own, :].astype(BF16)
                kwin = jnp.concatenate([kf[prev, :], kf[own, :]], axis=0).astype(BF16)
                vwin = jnp.concatenate([vf[prev, :], vf[own, :]], axis=0).astype(BF16)
                o, lse = _band_block(qb, kwin, vwin, bias_first if n == 0 else bias)
                og[g, own, :] = o[0]
                lg[g, own, :] = jnp.broadcast_to(lse[0], (BAND, HEAD_DIM))

    rows = 256

    def combine(c, carry):
        sl = pl.ds(pl.multiple_of(c * rows, rows), rows)
        ls = [lg[g, sl, :] for g in range(ng)]
        m = functools.reduce(jnp.maximum, ls)
        es = [jnp.exp(l - m) for l in ls]
        num = functools.reduce(jnp.add, [e * og[g, sl, :] for g, e in enumerate(es)])
        o_ref[sl, :] = (num / functools.reduce(jnp.add, es)).astype(BF16)
        return carry

    lax.fori_loop(0, seq // rows, combine, 0)


def _dilated_attention(qkv, batch, seq):
    ng = len(A_PATTERNS)
    k0 = N_Q_HEADS
    v0 = N_Q_HEADS + N_KV_HEADS
    in_specs = []
    for g in range(ng):
        for base in (0, k0, v0):
            in_specs.append(pl.BlockSpec(
                (1, seq, HEAD_DIM),
                lambda b, j, h=base + g * A_HEADS_PER_GROUP: (h + j, b, 0)))
    return pl.pallas_call(
        functools.partial(_dilated_kernel, seq=seq),
        out_shape=jax.ShapeDtypeStruct((batch * seq, A_HEADS_PER_GROUP * HEAD_DIM), BF16),
        grid=(batch, A_HEADS_PER_GROUP),
        in_specs=in_specs,
        out_specs=pl.BlockSpec((seq, HEAD_DIM), lambda b, j: (b, j)),
        scratch_shapes=[pltpu.VMEM((seq, HEAD_DIM), F32)] * 3
        + [pltpu.VMEM((ng, seq, HEAD_DIM), F32)] * 2,
        compiler_params=_cparams(("arbitrary",) * 2),
        name="dilated_attn",
    )(*([qkv] * (3 * ng)))


MOBA_TQ = 1024


def _online_update(state, s, v):
    mc = jnp.max(s, axis=1, keepdims=True)
    if state is None:
        p = jnp.exp(s - mc)
        return mc, jnp.sum(p, axis=1, keepdims=True), jnp.dot(p.astype(BF16), v, preferred_element_type=F32)
    m, l, acc = state
    m_new = jnp.maximum(m, mc)
    alpha = jnp.exp(m - m_new)
    p = jnp.exp(s - m_new)
    l_new = alpha * l + jnp.sum(p, axis=1, keepdims=True)
    acc_new = alpha * acc + jnp.dot(p.astype(BF16), v, preferred_element_type=F32)
    return m_new, l_new, acc_new


def _moba_kernel(q_ref, k_ref, v_ref, o_ref, kaug, kmean, *, seq):
    t = pl.program_id(2)
    nblk = seq // MOBA_BLOCK
    blk = MOBA_BLOCK
    tq = MOBA_TQ
    nb = tq // blk

    @pl.when(t == 0)
    def _():
        kaug[:, :HEAD_DIM] = k_ref[0]
        row = lax.broadcasted_iota(jnp.int32, (seq, HEAD_DIM), 0)
        lane = lax.broadcasted_iota(jnp.int32, (seq, HEAD_DIM), 1)
        kaug[:, HEAD_DIM:] = jnp.where(row // blk == lane, 1.0, 0.0).astype(BF16)
        kmean[...] = jnp.zeros_like(kmean)
        for n in range(nblk):
            kb = k_ref[0, n * blk:(n + 1) * blk, :].astype(F32)
            kmean[n:n + 1, :] = jnp.mean(kb, axis=0, keepdims=True)

    q = q_ref[0]
    km = kmean[...]
    km_hi = km.astype(BF16)
    km_lo = (km - km_hi.astype(F32)).astype(BF16)
    nt = (((1,), (1,)), ((), ()))
    gate = (lax.dot_general(km_hi, q, nt, preferred_element_type=F32)
            + lax.dot_general(km_lo, q, nt, preferred_element_type=F32))
    nrow = -(-nblk // 8) * 8
    row = lax.broadcasted_iota(jnp.int32, (nrow, tq), 0).astype(F32)
    own = (t * nb + lax.broadcasted_iota(jnp.int32, (nrow, tq), 1) // blk).astype(F32)
    g = jnp.where(row < own, gate[:nrow], NEG)
    sel = jnp.zeros((nrow, tq), jnp.bool_)
    for _ in range(MOBA_TOPK):
        mx = jnp.max(g, axis=0, keepdims=True)
        first = jnp.min(jnp.where(g == mx, row, float(nrow)), axis=0, keepdims=True)
        pick = row == first
        sel = sel | pick
        g = jnp.where(pick, NEG, g)
    visible = (sel & (row < own)) | (row == own)
    bias_t = jnp.concatenate([jnp.where(visible, 0.0, NEG), jnp.zeros((HEAD_DIM - nrow, tq), F32)], axis=0)
    qaug = jnp.concatenate([q, bias_t.T.astype(BF16)], axis=1)

    ri = lax.broadcasted_iota(jnp.int32, (blk, blk), 0)
    ci = lax.broadcasted_iota(jnp.int32, (blk, blk), 1)
    causal = ri >= ci

    for c in range(seq // tq):
        @pl.when(t == c)
        def _(c=c):
            state = None
            for cc in range(c):
                keys = slice(cc * tq, (cc + 1) * tq)
                s = lax.dot_general(qaug, kaug[keys, :], nt, preferred_element_type=F32)
                state = _online_update(state, s, v_ref[0, keys, :])
            n_split = 2 if c == 0 else 1
            hb = nb // n_split
            parts = []
            for h in range(n_split):
                rows = slice(h * hb * blk, (h + 1) * hb * blk)
                keys = slice(c * tq, c * tq + (h + 1) * hb * blk)
                s = lax.dot_general(qaug[rows], kaug[keys, :], nt, preferred_element_type=F32)
                s = jnp.concatenate(
                    [jnp.concatenate(
                        [jnp.where(causal, s[a * blk:(a + 1) * blk, e * blk:(e + 1) * blk], NEG)
                         if e == a + h * hb else s[a * blk:(a + 1) * blk, e * blk:(e + 1) * blk]
                         for e in range((h + 1) * hb)], axis=1) for a in range(hb)], axis=0)
                st = None if state is None else tuple(x[rows] for x in state)
                parts.append(_online_update(st, s, v_ref[0, keys, :]))
            m, l, acc = (jnp.concatenate([p[i] for p in parts], axis=0) for i in range(3))
            o_ref[...] = (acc / l).astype(BF16)


def _moba_attention(qkv, batch, seq):
    nt = seq // MOBA_TQ
    q0 = A_HEADS
    k0 = N_Q_HEADS + A_HEADS
    v0 = N_Q_HEADS + N_KV_HEADS + A_HEADS
    return pl.pallas_call(
        functools.partial(_moba_kernel, seq=seq),
        out_shape=jax.ShapeDtypeStruct((batch * seq, B_HEADS * HEAD_DIM), BF16),
        grid=(batch, B_HEADS, nt),
        in_specs=[
            pl.BlockSpec((1, MOBA_TQ, HEAD_DIM), lambda b, h, t: (q0 + h, b * nt + t, 0)),
            pl.BlockSpec((1, seq, HEAD_DIM), lambda b, h, t: (k0 + h, b, 0)),
            pl.BlockSpec((1, seq, HEAD_DIM), lambda b, h, t: (v0 + h, b, 0)),
        ],
        out_specs=pl.BlockSpec((MOBA_TQ, HEAD_DIM), lambda b, h, t: (b * nt + t, h)),
        scratch_shapes=[pltpu.VMEM((seq, 2 * HEAD_DIM), BF16), pltpu.VMEM((HEAD_DIM, HEAD_DIM), F32)],
        compiler_params=_cparams(("arbitrary",) * 3),
        name="moba_attn",
    )(qkv, qkv, qkv)


def _merge_kernel(h_ref, oa_ref, ob_ref, oc_ref, g0_ref, g1_ref, g2_ref, b0_ref, b1_ref, b2_ref,
                  wa_ref, wb_ref, wc_ref, o_ref):
    x = h_ref[...]
    acc = None
    for g_ref, b_ref, br_ref, w_ref in ((g0_ref, b0_ref, oa_ref, wa_ref),
                                        (g1_ref, b1_ref, ob_ref, wb_ref),
                                        (g2_ref, b2_ref, oc_ref, wc_ref)):
        gate = jax.nn.sigmoid(jnp.dot(x, g_ref[...], preferred_element_type=F32) + b_ref[...])
        term = gate * jnp.dot(br_ref[...], w_ref[...], preferred_element_type=F32)
        acc = term if acc is None else acc + term
    o_ref[...] = acc.astype(BF16)


def _merge(hb, out_a, out_b, out_c, w_gate_b, b_gate, wa_b, wb_b, wc_b, layer):
    t, d = hb.shape
    tm, tn = 1024, 512
    nj = d // tn
    row = lambda w: pl.BlockSpec((tm, w), lambda j, i: (i, 0))
    gate_w = lambda x: pl.BlockSpec((None, d, tn), lambda j, i, x=x: (layer, 0, x * nj + j))
    gate_b = lambda x: pl.BlockSpec((None, 1, tn), lambda j, i, x=x: (layer, 0, x * nj + j))
    br_w = lambda k: pl.BlockSpec((None, k, tn), lambda j, i: (layer, 0, j))
    bg = b_gate.reshape(b_gate.shape[0], 1, -1)
    return pl.pallas_call(
        _merge_kernel,
        out_shape=jax.ShapeDtypeStruct((t, d), BF16),
        grid=(nj, t // tm),
        in_specs=[row(d), row(out_a.shape[1]), row(out_b.shape[1]), row(out_c.shape[1]),
                  gate_w(0), gate_w(1), gate_w(2), gate_b(0), gate_b(1), gate_b(2),
                  br_w(wa_b.shape[1]), br_w(wb_b.shape[1]), br_w(wc_b.shape[1])],
        out_specs=pl.BlockSpec((tm, tn), lambda j, i: (i, j)),
        compiler_params=_cparams(("arbitrary",) * 2),
        name="branch_merge",
    )(hb, out_a, out_b, out_c, w_gate_b, w_gate_b, w_gate_b, bg, bg, bg, wa_b, wb_b, wc_b)


def _layer_norm_rows(y, g, b):
    mu = jnp.mean(y, axis=-1, keepdims=True)
    yc = y - mu
    var = jnp.mean(yc * yc, axis=-1, keepdims=True)
    return yc * lax.rsqrt(var + LN_EPS) * g + b


def _route(lt):
    tm = lt.shape[1]
    gl = lt[0:N_GROUPS]
    gmax = jnp.max(gl, axis=0, keepdims=True)
    ge = jnp.exp(gl - gmax)
    gp = ge / jnp.sum(ge, axis=0, keepdims=True)
    g_val = jnp.max(gp, axis=0, keepdims=True)
    row = lax.broadcasted_iota(jnp.int32, (N_GROUPS, tm), 0)
    g_sel = jnp.min(jnp.where(gp == g_val, row, N_GROUPS), axis=0, keepdims=True)
    el = jnp.zeros((EXPERTS_PER_GROUP, tm), F32)
    for g in range(N_GROUPS):
        lo = 8 + g * EXPERTS_PER_GROUP
        el = jnp.where(g_sel == g, lt[lo:lo + EXPERTS_PER_GROUP], el)
    v0 = jnp.max(el, axis=0, keepdims=True)
    i0 = jnp.min(jnp.where(el == v0, row, EXPERTS_PER_GROUP), axis=0, keepdims=True)
    el1 = jnp.where(row == i0, -jnp.inf, el)
    v1 = jnp.max(el1, axis=0, keepdims=True)
    i1 = jnp.min(jnp.where(el1 == v1, row, EXPERTS_PER_GROUP), axis=0, keepdims=True)
    e1 = jnp.exp(v1 - v0)
    den = 1.0 + e1
    w0 = 1.0 / den * g_val
    w1 = e1 / den * g_val
    id0 = (g_sel * EXPERTS_PER_GROUP + i0).astype(F32)
    id1 = (g_sel * EXPERTS_PER_GROUP + i1).astype(F32)
    return id0, id1, w0, w1


def _outproj_ln_route_kernel(m_ref, w_ref, h_ref, g_ref, b_ref, wr_ref, hf_ref, rt_ref, mix_buf,
                             wrh_ref, wrl_ref, *, alpha):
    @pl.when(pl.program_id(0) == 0)
    def _():
        mix_buf[...] = jnp.zeros_like(mix_buf)
        wr = wr_ref[...]
        wr_hi = wr.astype(BF16)
        wrh_ref[...] = wr_hi
        wrl_ref[...] = (wr - wr_hi.astype(F32)).astype(BF16)

    prev = mix_buf[...]
    mix_buf[...] = jnp.dot(m_ref[...], w_ref[...], preferred_element_type=F32)
    h1 = _layer_norm_rows(alpha * h_ref[...] + prev, g_ref[...], b_ref[...])
    hf_ref[...] = h1
    h_hi = h1.astype(BF16)
    h_lo = (h1 - h_hi.astype(F32)).astype(BF16)
    logits = (jnp.dot(h_hi, wrh_ref[...], preferred_element_type=F32)
              + jnp.dot(h_lo, wrh_ref[...], preferred_element_type=F32)
              + jnp.dot(h_hi, wrl_ref[...], preferred_element_type=F32))
    id0, id1, w0, w1 = _route(logits.T)
    rt_ref[...] = jnp.concatenate([id0, id1, w0, w1, jnp.zeros((4, logits.shape[0]), F32)], axis=0)


def _outproj_ln_route(merged, w_out_b, layer, h, ln_g, ln_b, w_router, alpha):
    t, d = h.shape
    tm = 512
    n = t // tm
    cur = lambda i: (jnp.minimum(i, n - 1), 0)
    prv = lambda i: (jnp.maximum(i - 1, 0), 0)
    return pl.pallas_call(
        functools.partial(_outproj_ln_route_kernel, alpha=alpha),
        out_shape=(jax.ShapeDtypeStruct((t, d), F32), jax.ShapeDtypeStruct((8, t), F32)),
        grid=(n + 1,),
        in_specs=[pl.BlockSpec((tm, d), cur),
                  pl.BlockSpec((None, d, d), lambda i: (layer, 0, 0)),
                  pl.BlockSpec((tm, d), prv),
                  pl.BlockSpec((1, d), lambda i: (0, 0)),
                  pl.BlockSpec((1, d), lambda i: (0, 0)),
                  pl.BlockSpec((d, HEAD_DIM), lambda i: (0, 0))],
        out_specs=(pl.BlockSpec((tm, d), prv),
                   pl.BlockSpec((8, tm), lambda i: (0, jnp.maximum(i - 1, 0)))),
        scratch_shapes=[pltpu.VMEM((tm, d), F32),
                        pltpu.VMEM((d, HEAD_DIM), BF16), pltpu.VMEM((d, HEAD_DIM), BF16)],
        compiler_params=_cparams(("arbitrary",)),
        name="outproj_ln1_route",
    )(merged, w_out_b, h, ln_g.reshape(1, d), ln_b.reshape(1, d), w_router)


MOE_TILE = 256
GATHER_TILE = 512
DMA_UNROLL = 8


def _dispatch_plan(rt, t):
    ids = rt[0:2].astype(jnp.int32).reshape(-1)
    tok = jnp.tile(jnp.arange(t, dtype=jnp.int32), 2)
    onehot = (ids[:, None] == jnp.arange(N_EXPERTS, dtype=jnp.int32)[None, :]).astype(jnp.int32)
    csum = jnp.cumsum(onehot, axis=0)
    counts = csum[-1]
    rank = jnp.take_along_axis(csum, ids[:, None], axis=1)[:, 0] - 1
    padded = (counts + MOE_TILE - 1) // MOE_TILE * MOE_TILE
    ends = jnp.cumsum(padded)
    pos = (ends - padded)[ids] + rank
    p_rows = 2 * t + N_EXPERTS * MOE_TILE
    src = jnp.zeros((p_rows + MOE_TILE,), jnp.int32).at[pos].set(tok)
    n_tiles = p_rows // MOE_TILE
    n_used = (ends[-1] // MOE_TILE).astype(jnp.int32)
    tile_idx = jnp.arange(n_tiles, dtype=jnp.int32)
    tile_start = jnp.minimum(tile_idx, n_used - 1) * MOE_TILE
    tile_e = jnp.sum((ends[None, :] <= tile_start[:, None]).astype(jnp.int32), axis=1)
    tile_e = jnp.minimum(tile_e, N_EXPERTS - 1)
    w_cols = rt[2:4].reshape(2, t, 1)
    return src, pos.astype(jnp.int32), w_cols, tile_e, n_used.reshape(1)


def _expert_mlp_kernel(src_ref, te_ref, nu_ref, h_hbm, wg_ref, wu_ref, wd_ref, y_ref, xbuf, sem):
    i = pl.program_id(0)
    n_used = nu_ref[0]
    slot = i % 2

    def issue_rows(tile, dst_slot, lo, hi):
        for j in range(lo, hi):
            row = src_ref[tile * MOE_TILE + j]
            pltpu.make_async_copy(h_hbm.at[pl.ds(row, 1)], xbuf.at[dst_slot, pl.ds(j, 1)],
                                  sem.at[dst_slot]).start()

    def wait_tile(s):
        pltpu.make_async_copy(h_hbm.at[pl.ds(0, MOE_TILE)], xbuf.at[s], sem.at[s]).wait()

    @pl.when(i == 0)
    def _():
        def first(c, carry):
            base = pl.multiple_of(c * DMA_UNROLL, DMA_UNROLL)
            for j in range(DMA_UNROLL):
                row = src_ref[base + j]
                pltpu.make_async_copy(h_hbm.at[pl.ds(row, 1)], xbuf.at[0, pl.ds(base + j, 1)],
                                      sem.at[0]).start()
            return carry
        lax.fori_loop(0, MOE_TILE // DMA_UNROLL, first, 0)

    @pl.when(i < n_used)
    def _():
        wait_tile(slot)
        nxt = 1 - slot
        third = MOE_TILE // 3
        x = xbuf[slot].astype(BF16)
        issue_rows(i + 1, nxt, 0, third)
        a = jnp.dot(x, wg_ref[0], preferred_element_type=F32)
        issue_rows(i + 1, nxt, third, 2 * third)
        u = jnp.dot(x, wu_ref[0], preferred_element_type=F32)
        issue_rows(i + 1, nxt, 2 * third, MOE_TILE)
        hid = (jax.nn.silu(a) * u).astype(BF16)
        y_ref[...] = jnp.dot(hid, wd_ref[0], preferred_element_type=F32)

    @pl.when(i >= n_used)
    def _():
        y_ref[...] = jnp.zeros_like(y_ref)

    @pl.when(i == n_used)
    def _():
        wait_tile(slot)


def _expert_mlp(h1, src, tile_e, n_used, wg_b, wu_b, wd_b, layer):
    t, d = h1.shape
    p = src.shape[0] - MOE_TILE
    de = wg_b.shape[2]
    exp = lambda i, src, te, nu: (layer * N_EXPERTS + te[i], 0, 0)
    return pl.pallas_call(
        _expert_mlp_kernel,
        out_shape=jax.ShapeDtypeStruct((p, d), F32),
        grid_spec=pltpu.PrefetchScalarGridSpec(
            num_scalar_prefetch=3, grid=(p // MOE_TILE,),
            in_specs=[pl.BlockSpec(memory_space=pl.ANY),
                      pl.BlockSpec((1, d, de), exp),
                      pl.BlockSpec((1, d, de), exp),
                      pl.BlockSpec((1, de, d), exp)],
            out_specs=pl.BlockSpec((MOE_TILE, d), lambda i, src, te, nu: (i, 0)),
            scratch_shapes=[pltpu.VMEM((2, MOE_TILE, d), F32), pltpu.SemaphoreType.DMA((2,))]),
        compiler_params=_cparams(("arbitrary",)),
        name="moe_expert_mlp",
    )(src, tile_e, n_used, h1, wg_b, wu_b, wd_b)


def _combine_ln_kernel(pos_ref, y_hbm, h_ref, w_ref, g_ref, b_ref, of_ref, ob_ref, buf, sem, *, alpha, t):
    i = pl.program_id(0)
    tm = h_ref.shape[0]

    def issue_tile(tile, slot):
        def issue(c, carry):
            base = pl.multiple_of(c * DMA_UNROLL, DMA_UNROLL)
            for k in range(2):
                dst = buf.at[slot, k, pl.ds(base, DMA_UNROLL)]
                for j in range(DMA_UNROLL):
                    row = pos_ref[k * t + tile * tm + base + j]
                    pltpu.make_async_copy(y_hbm.at[pl.ds(row, 1)], dst.at[pl.ds(j, 1)],
                                          sem.at[slot]).start(priority=j % 2)
            return carry

        lax.fori_loop(0, tm // DMA_UNROLL, issue, 0)

    @pl.when(i == 0)
    def _():
        issue_tile(0, 0)

    @pl.when(i + 1 < pl.num_programs(0))
    def _():
        issue_tile(i + 1, (i + 1) % 2)

    slot = i % 2
    for k in range(2):
        pltpu.make_async_copy(y_hbm.at[pl.ds(0, tm)], buf.at[slot, k], sem.at[slot]).wait()
    ffn = w_ref[0] * buf[slot, 0] + w_ref[1] * buf[slot, 1]
    out = _layer_norm_rows(alpha * h_ref[...] + ffn, g_ref[...], b_ref[...])
    of_ref[...] = out
    ob_ref[...] = out.astype(BF16)


def _combine_ln(pos, w_cols, y_sorted, h1, ln_g, ln_b, alpha):
    t, d = h1.shape
    tm = GATHER_TILE
    return pl.pallas_call(
        functools.partial(_combine_ln_kernel, alpha=alpha, t=t),
        out_shape=(jax.ShapeDtypeStruct((t, d), F32), jax.ShapeDtypeStruct((t, d), BF16)),
        grid_spec=pltpu.PrefetchScalarGridSpec(
            num_scalar_prefetch=1, grid=(t // tm,),
            in_specs=[pl.BlockSpec(memory_space=pl.ANY),
                      pl.BlockSpec((tm, d), lambda i, pos: (i, 0)),
                      pl.BlockSpec((2, tm, 1), lambda i, pos: (0, i, 0)),
                      pl.BlockSpec((1, d), lambda i, pos: (0, 0)),
                      pl.BlockSpec((1, d), lambda i, pos: (0, 0))],
            out_specs=(pl.BlockSpec((tm, d), lambda i, pos: (i, 0)),
                       pl.BlockSpec((tm, d), lambda i, pos: (i, 0))),
            scratch_shapes=[pltpu.VMEM((2, 2, tm, d), F32), pltpu.SemaphoreType.DMA((2,))]),
        compiler_params=_cparams(("arbitrary",)),
        name="moe_combine_ln2",
    )(pos, y_sorted, h1, w_cols, ln_g.reshape(1, d), ln_b.reshape(1, d))


def _moe(h1f, rt, wg_b, wu_b, wd_b, layer, ln_g, ln_b, alpha):
    t = h1f.shape[0]
    src, pos, w_cols, tile_e, n_used = _dispatch_plan(rt, t)
    y_sorted = _expert_mlp(h1f, src, tile_e, n_used, wg_b, wu_b, wd_b, layer)
    return _combine_ln(pos, w_cols, y_sorted, h1f, ln_g, ln_b, alpha)


def kernel(x, positions, w_in, w_gate, b_gate, w_branch_a, w_branch_b, w_branch_c, w_out, sinks,
           ln1_g, ln1_b, w_router_group, w_router_expert, w_exp_gate, w_exp_up, w_exp_down,
           ln2_g, ln2_b):
    batch, seq, d = x.shape
    depth = w_in.shape[0]
    t = batch * seq
    alpha = (2 * depth) ** 0.25
    cc, ss = _rope_tables(positions)
    hf = x.reshape(t, d)
    hb = hf.astype(BF16)
    w_in_b, w_gate_b, w_out_b = w_in.astype(BF16), w_gate.astype(BF16), w_out.astype(BF16)
    wa_b, wb_b, wc_b = w_branch_a.astype(BF16), w_branch_b.astype(BF16), w_branch_c.astype(BF16)
    de = w_exp_gate.shape[-1]
    wg_b = w_exp_gate.astype(BF16).reshape(depth * N_EXPERTS, d, de)
    wu_b = w_exp_up.astype(BF16).reshape(depth * N_EXPERTS, d, de)
    wd_b = w_exp_down.astype(BF16).reshape(depth * N_EXPERTS, de, d)
    for l in range(depth):
        qkv = _qkv_proj(hb, w_in_b, l, cc, ss)
        out_a = _dilated_attention(qkv, batch, seq)
        out_b = _moba_attention(qkv, batch, seq)
        out_c = _swa_attention(qkv, sinks[l], batch, seq)
        merged = _merge(hb, out_a, out_b, out_c, w_gate_b, b_gate, wa_b, wb_b, wc_b, l)
        w_router = jnp.zeros((d, HEAD_DIM), F32)
        w_router = w_router.at[:, 0:N_GROUPS].set(w_router_group[l])
        w_router = w_router.at[:, 8:8 + N_EXPERTS].set(w_router_expert[l])
        h1f, rt = _outproj_ln_route(merged, w_out_b, l, hf, ln1_g[l], ln1_b[l], w_router, alpha)
        hf, hb = _moe(h1f, rt, wg_b, wu_b, wd_b, l, ln2_g[l], ln2_b[l], alpha)
    return hf.reshape(batch, seq, d)
```

```python
import functools

import jax
import jax.numpy as jnp
from jax import lax
from jax.experimental import pallas as pl
from jax.experimental.pallas import tpu as pltpu

F32 = jnp.float32
BF16 = jnp.bfloat16

HEAD_DIM = 128
ATTN_SCALE = HEAD_DIM ** -0.5
ROPE_THETA = 10000.0
A_PATTERNS = ((128, 1), (512, 4), (2048, 16))
A_HEADS_PER_GROUP = 2
A_HEADS = A_HEADS_PER_GROUP * len(A_PATTERNS)
B_HEADS = 6
MOBA_BLOCK = 256
MOBA_TOPK = 3
C_Q_HEADS = 8
C_KV_HEADS = 2
C_GROUP = C_Q_HEADS // C_KV_HEADS
C_WINDOW = 128
BAND = 128
N_Q_HEADS = A_HEADS + B_HEADS + C_Q_HEADS
N_KV_HEADS = A_HEADS + B_HEADS + C_KV_HEADS
N_GROUPS = 4
EXPERTS_PER_GROUP = 4
N_EXPERTS = N_GROUPS * EXPERTS_PER_GROUP
LN_EPS = 1e-5
NEG = -1e30

V7X_VMEM_BYTES = 64 * 1024 * 1024
VMEM_LIMIT = V7X_VMEM_BYTES - 8 * 1024 * 1024


def _cparams(sem, vmem=VMEM_LIMIT):
    return pltpu.CompilerParams(dimension_semantics=sem, vmem_limit_bytes=vmem)


ROPE_SLOT_Q, ROPE_SLOT_K, ROPE_SLOT_V = 0, 1, 2


def _rope_table_kernel(pos_ref, inv_ref, cc_ref, ss_ref):
    ang = pos_ref[...].astype(F32) * inv_ref[...]
    lane = lax.broadcasted_iota(jnp.int32, ang.shape, 1)
    s = jnp.sin(ang)
    c = jnp.cos(ang)
    s = jnp.where(lane < HEAD_DIM // 2, -s, s)
    cc_ref[ROPE_SLOT_Q] = c * ATTN_SCALE
    ss_ref[ROPE_SLOT_Q] = s * ATTN_SCALE
    cc_ref[ROPE_SLOT_K] = c
    ss_ref[ROPE_SLOT_K] = s
    cc_ref[ROPE_SLOT_V] = jnp.ones_like(c)
    ss_ref[ROPE_SLOT_V] = jnp.zeros_like(s)


def _rope_tables(positions):
    t = positions.size
    tm = 2048
    inv = ROPE_THETA ** (-jnp.arange(0, HEAD_DIM, 2, dtype=F32) / HEAD_DIM)
    inv2 = jnp.concatenate([inv, inv]).reshape(1, HEAD_DIM)
    pos = positions.reshape(t, 1)
    return pl.pallas_call(
        _rope_table_kernel,
        out_shape=(jax.ShapeDtypeStruct((3, t, HEAD_DIM), F32),) * 2,
        grid=(t // tm,),
        in_specs=[pl.BlockSpec((tm, 1), lambda i: (i, 0)),
                  pl.BlockSpec((1, HEAD_DIM), lambda i: (0, 0))],
        out_specs=(pl.BlockSpec((3, tm, HEAD_DIM), lambda i: (0, i, 0)),) * 2,
        compiler_params=_cparams(("arbitrary",)),
        name="rope_tables",
    )(pos, inv2)


def _qkv_kernel(x_ref, w_ref, cc_ref, ss_ref, o_ref, *, heads_per_tile):
    j = pl.program_id(1)
    acc = jnp.dot(x_ref[...], w_ref[...], preferred_element_type=F32)
    for hh in range(heads_per_tile):
        head = j * heads_per_tile + hh
        slot = (head >= N_Q_HEADS).astype(jnp.int32) + (head >= N_Q_HEADS + N_KV_HEADS).astype(jnp.int32)
        t = acc[:, hh * HEAD_DIM:(hh + 1) * HEAD_DIM]
        r = t * cc_ref[slot] + pltpu.roll(t, HEAD_DIM // 2, 1) * ss_ref[slot]
        o_ref[hh] = r.astype(BF16)


def _qkv_proj(hb, w_in_b, layer, cc, ss):
    t, d = hb.shape
    n = w_in_b.shape[2]
    tm, tn = 1024, 1024
    hpt = tn // HEAD_DIM
    return pl.pallas_call(
        functools.partial(_qkv_kernel, heads_per_tile=hpt),
        out_shape=jax.ShapeDtypeStruct((n // HEAD_DIM, t, HEAD_DIM), BF16),
        grid=(t // tm, n // tn),
        in_specs=[pl.BlockSpec((tm, d), lambda i, j: (i, 0)),
                  pl.BlockSpec((None, d, tn), lambda i, j: (layer, 0, j)),
                  pl.BlockSpec((3, tm, HEAD_DIM), lambda i, j: (0, i, 0)),
                  pl.BlockSpec((3, tm, HEAD_DIM), lambda i, j: (0, i, 0))],
        out_specs=pl.BlockSpec((hpt, tm, HEAD_DIM), lambda i, j: (j, i, 0)),
        compiler_params=_cparams(("arbitrary", "arbitrary")),
        name="qkv_rope",
    )(hb, w_in_b, cc, ss)


def _band_bias(max_dist):
    qi = lax.broadcasted_iota(jnp.int32, (BAND, 2 * BAND), 0)
    kj = lax.broadcasted_iota(jnp.int32, (BAND, 2 * BAND), 1)
    dist = qi + BAND - kj
    valid = (dist >= 0) & (dist <= max_dist)
    bias = jnp.where(valid, 0.0, NEG).astype(F32)
    bias_first = jnp.where(valid & (kj >= BAND), 0.0, NEG).astype(F32)
    return bias, bias_first


def _band_block(qb, kwin, vwin, bias):
    g = qb.shape[0] // BAND
    s = lax.dot_general(qb, kwin, (((1,), (1,)), ((), ())), preferred_element_type=F32)
    s = s.reshape(g, BAND, 2 * BAND) + bias[None]
    m = jnp.max(s, axis=-1, keepdims=True)
    p = jnp.exp(s - m)
    l = jnp.sum(p, axis=-1, keepdims=True)
    o = jnp.dot(p.reshape(g * BAND, 2 * BAND).astype(BF16), vwin, preferred_element_type=F32)
    o = o.reshape(g, BAND, HEAD_DIM) / l
    return o, m + jnp.log(l)


def _swa_kernel(sink_ref, q_ref, k_ref, kh_ref, v_ref, vh_ref, o_ref, *, tq):
    kv = pl.program_id(1)
    t = pl.program_id(2)
    bias, bias_first = _band_bias(C_WINDOW - 1)
    for blk in range(tq // BAND):
        qb = q_ref[:, blk * BAND:(blk + 1) * BAND, :].reshape(C_GROUP * BAND, HEAD_DIM)
        if blk == 0:
            kwin = jnp.concatenate([kh_ref[0], k_ref[0, :BAND]], axis=0)
            vwin = jnp.concatenate([vh_ref[0], v_ref[0, :BAND]], axis=0)
            b = jnp.where(t == 0, bias_first, bias)
        else:
            kwin = k_ref[0, (blk - 1) * BAND:(blk + 1) * BAND]
            vwin = v_ref[0, (blk - 1) * BAND:(blk + 1) * BAND]
            b = bias
        o, lse = _band_block(qb, kwin, vwin, b)
        for g in range(C_GROUP):
            keep = jax.nn.sigmoid(lse[g] - sink_ref[kv * C_GROUP + g])
            o_ref[blk * BAND:(blk + 1) * BAND, g * HEAD_DIM:(g + 1) * HEAD_DIM] = (o[g] * keep).astype(BF16)


def _swa_attention(qkv, sinks, batch, seq):
    tq = 512
    nt = seq // tq
    q0 = A_HEADS + B_HEADS
    k0 = N_Q_HEADS + A_HEADS + B_HEADS
    v0 = N_Q_HEADS + N_KV_HEADS + A_HEADS + B_HEADS
    rb = tq // BAND

    def halo(b, kv, t):
        return jnp.maximum(b * (seq // BAND) + t * rb - 1, 0)

    return pl.pallas_call(
        functools.partial(_swa_kernel, tq=tq),
        out_shape=jax.ShapeDtypeStruct((batch * seq, C_Q_HEADS * HEAD_DIM), BF16),
        grid=(batch, C_KV_HEADS, nt),
        in_specs=[
            pl.BlockSpec(memory_space=pltpu.SMEM),
            pl.BlockSpec((C_GROUP, tq, HEAD_DIM), lambda b, kv, t: (q0 // C_GROUP + kv, b * nt + t, 0)),
            pl.BlockSpec((1, tq, HEAD_DIM), lambda b, kv, t: (k0 + kv, b * nt + t, 0)),
            pl.BlockSpec((1, BAND, HEAD_DIM), lambda b, kv, t: (k0 + kv, halo(b, kv, t), 0)),
            pl.BlockSpec((1, tq, HEAD_DIM), lambda b, kv, t: (v0 + kv, b * nt + t, 0)),
            pl.BlockSpec((1, BAND, HEAD_DIM), lambda b, kv, t: (v0 + kv, halo(b, kv, t), 0)),
        ],
        out_specs=pl.BlockSpec((tq, C_GROUP * HEAD_DIM), lambda b, kv, t: (b * nt + t, kv)),
        compiler_params=_cparams(("arbitrary",) * 3),
        name="swa_sink",
    )(sinks, qkv, qkv, qkv, qkv, qkv)


def _dilated_kernel(*refs, seq):
    ng = len(A_PATTERNS)
    in_refs = refs[:3 * ng]
    o_ref = refs[3 * ng]
    qf, kf, vf, og, lg = refs[3 * ng + 1:]
    for g, (window, dil) in enumerate(A_PATTERNS):
        q_ref, k_ref, v_ref = in_refs[3 * g:3 * g + 3]
        qf[...] = q_ref[0].astype(F32)
        kf[...] = k_ref[0].astype(F32)
        vf[...] = v_ref[0].astype(F32)
        nblk = seq // dil // BAND
        bias, bias_first = _band_bias(window // dil)

        for r in range(dil):
            for n in range(nblk):
                own = pl.ds(r + dil * BAND * n, BAND, stride=dil)
                prev = pl.ds(r + dil * BAND * max(n - 1, 0), BAND, stride=dil)
                qb = qf[own, :].astype(BF16)
                kwin = jnp.concatenate([kf[prev, :], kf[own, :]], axis=0).astype(BF16)
                vwin = jnp.concatenate([vf[prev, :], vf[own, :]], axis=0).astype(BF16)
                o, lse = _band_block(qb, kwin, vwin, bias_first if n == 0 else bias)
                og[g, own, :] = o[0]
                lg[g, own, :] = jnp.broadcast_to(lse[0], (BAND, HEAD_DIM))

    rows = 256

    def combine(c, carry):
        sl = pl.ds(pl.multiple_of(c * rows, rows), rows)
        ls = [lg[g, sl, :] for g in range(ng)]
        m = functools.reduce(jnp.maximum, ls)
        es = [jnp.exp(l - m) for l in ls]
        num = functools.reduce(jnp.add, [e * og[g, sl, :] for g, e in enumerate(es)])
        o_ref[sl, :] = (num / functools.reduce(jnp.add, es)).astype(BF16)
        return carry

    lax.fori_loop(0, seq // rows, combine, 0)


def _dilated_attention(qkv, batch, seq):
    ng = len(A_PATTERNS)
    k0 = N_Q_HEADS
    v0 = N_Q_HEADS + N_KV_HEADS
    in_specs = []
    for g in range(ng):
        for base in (0, k0, v0):
            in_specs.append(pl.BlockSpec(
                (1, seq, HEAD_DIM),
                lambda b, j, h=base + g * A_HEADS_PER_GROUP: (h + j, b, 0)))
    return pl.pallas_call(
        functools.partial(_dilated_kernel, seq=seq),
        out_shape=jax.ShapeDtypeStruct((batch * seq, A_HEADS_PER_GROUP * HEAD_DIM), BF16),
        grid=(batch, A_HEADS_PER_GROUP),
        in_specs=in_specs,
        out_specs=pl.BlockSpec((seq, HEAD_DIM), lambda b, j: (b, j)),
        scratch_shapes=[pltpu.VMEM((seq, HEAD_DIM), F32)] * 3
        + [pltpu.VMEM((ng, seq, HEAD_DIM), F32)] * 2,
        compiler_params=_cparams(("arbitrary",) * 2),
        name="dilated_attn",
    )(*([qkv] * (3 * ng)))


MOBA_TQ = 1024


def _online_update(state, s, v):
    mc = jnp.max(s, axis=1, keepdims=True)
    if state is None:
        p = jnp.exp(s - mc)
        return mc, jnp.sum(p, axis=1, keepdims=True), jnp.dot(p.astype(BF16), v, preferred_element_type=F32)
    m, l, acc = state
    m_new = jnp.maximum(m, mc)
    alpha = jnp.exp(m - m_new)
    p = jnp.exp(s - m_new)
    l_new = alpha * l + jnp.sum(p, axis=1, keepdims=True)
    acc_new = alpha * acc + jnp.dot(p.astype(BF16), v, preferred_element_type=F32)
    return m_new, l_new, acc_new


def _moba_kernel(q_ref, k_ref, v_ref, o_ref, kaug, kmean, *, seq):
    t = pl.program_id(2)
    nblk = seq // MOBA_BLOCK
    blk = MOBA_BLOCK
    tq = MOBA_TQ
    nb = tq // blk

    @pl.when(t == 0)
    def _():
        kaug[:, :HEAD_DIM] = k_ref[0]
        row = lax.broadcasted_iota(jnp.int32, (seq, HEAD_DIM), 0)
        lane = lax.broadcasted_iota(jnp.int32, (seq, HEAD_DIM), 1)
        kaug[:, HEAD_DIM:] = jnp.where(row // blk == lane, 1.0, 0.0).astype(BF16)
        kmean[...] = jnp.zeros_like(kmean)
        for n in range(nblk):
            kb = k_ref[0, n * blk:(n + 1) * blk, :].astype(F32)
            kmean[n:n + 1, :] = jnp.mean(kb, axis=0, keepdims=True)

    q = q_ref[0]
    km = kmean[...]
    km_hi = km.astype(BF16)
    km_lo = (km - km_hi.astype(F32)).astype(BF16)
    nt = (((1,), (1,)), ((), ()))
    gate = (lax.dot_general(km_hi, q, nt, preferred_element_type=F32)
            + lax.dot_general(km_lo, q, nt, preferred_element_type=F32))
    nrow = -(-nblk // 8) * 8
    row = lax.broadcasted_iota(jnp.int32, (nrow, tq), 0).astype(F32)
    own = (t * nb + lax.broadcasted_iota(jnp.int32, (nrow, tq), 1) // blk).astype(F32)
    g = jnp.where(row < own, gate[:nrow], NEG)
    sel = jnp.zeros((nrow, tq), jnp.bool_)
    for _ in range(MOBA_TOPK):
        mx = jnp.max(g, axis=0, keepdims=True)
        first = jnp.min(jnp.where(g == mx, row, float(nrow)), axis=0, keepdims=True)
        pick = row == first
        sel = sel | pick
        g = jnp.where(pick, NEG, g)
    visible = (sel & (row < own)) | (row == own)
    bias_t = jnp.concatenate([jnp.where(visible, 0.0, NEG), jnp.zeros((HEAD_DIM - nrow, tq), F32)], axis=0)
    qaug = jnp.concatenate([q, bias_t.T.astype(BF16)], axis=1)

    ri = lax.broadcasted_iota(jnp.int32, (blk, blk), 0)
    ci = lax.broadcasted_iota(jnp.int32, (blk, blk), 1)
    causal = ri >= ci

    for c in range(seq // tq):
        @pl.when(t == c)
        def _(c=c):
            state = None
            for cc in range(c):
                keys = slice(cc * tq, (cc + 1) * tq)
                s = lax.dot_general(qaug, kaug[keys, :], nt, preferred_element_type=F32)
                state = _online_update(state, s, v_ref[0, keys, :])
            n_split = 2 if c == 0 else 1
            hb = nb // n_split
            parts = []
            for h in range(n_split):
                rows = slice(h * hb * blk, (h + 1) * hb * blk)
                keys = slice(c * tq, c * tq + (h + 1) * hb * blk)
                s = lax.dot_general(qaug[rows], kaug[keys, :], nt, preferred_element_type=F32)
                s = jnp.concatenate(
                    [jnp.concatenate(
                        [jnp.where(causal, s[a * blk:(a + 1) * blk, e * blk:(e + 1) * blk], NEG)
                         if e == a + h * hb else s[a * blk:(a + 1) * blk, e * blk:(e + 1) * blk]
                         for e in range((h + 1) * hb)], axis=1) for a in range(hb)], axis=0)
                st = None if state is None else tuple(x[rows] for x in state)
                parts.append(_online_update(st, s, v_ref[0, keys, :]))
            m, l, acc = (jnp.concatenate([p[i] for p in parts], axis=0) for i in range(3))
            o_ref[...] = (acc / l).astype(BF16)


def _moba_attention(qkv, batch, seq):
    nt = seq // MOBA_TQ
    q0 = A_HEADS
    k0 = N_Q_HEADS + A_HEADS
    v0 = N_Q_HEADS + N_KV_HEADS + A_HEADS
    return pl.pallas_call(
        functools.partial(_moba_kernel, seq=seq),
        out_shape=jax.ShapeDtypeStruct((batch * seq, B_HEADS * HEAD_DIM), BF16),
        grid=(batch, B_HEADS, nt),
        in_specs=[
            pl.BlockSpec((1, MOBA_TQ, HEAD_DIM), lambda b, h, t: (q0 + h, b * nt + t, 0)),
            pl.BlockSpec((1, seq, HEAD_DIM), lambda b, h, t: (k0 + h, b, 0)),
            pl.BlockSpec((1, seq, HEAD_DIM), lambda b, h, t: (v0 + h, b, 0)),
        ],
        out_specs=pl.BlockSpec((MOBA_TQ, HEAD_DIM), lambda b, h, t: (b * nt + t, h)),
        scratch_shapes=[pltpu.VMEM((seq, 2 * HEAD_DIM), BF16), pltpu.VMEM((HEAD_DIM, HEAD_DIM), F32)],
        compiler_params=_cparams(("arbitrary",) * 3),
        name="moba_attn",
    )(qkv, qkv, qkv)


def _merge_kernel(h_ref, oa_ref, ob_ref, oc_ref, g0_ref, g1_ref, g2_ref, b0_ref, b1_ref, b2_ref,
                  wa_ref, wb_ref, wc_ref, o_ref):
    x = h_ref[...]
    acc = None
    for g_ref, b_ref, br_ref, w_ref in ((g0_ref, b0_ref, oa_ref, wa_ref),
                                        (g1_ref, b1_ref, ob_ref, wb_ref),
                                        (g2_ref, b2_ref, oc_ref, wc_ref)):
        gate = jax.nn.sigmoid(jnp.dot(x, g_ref[...], preferred_element_type=F32) + b_ref[...])
        term = gate * jnp.dot(br_ref[...], w_ref[...], preferred_element_type=F32)
        acc = term if acc is None else acc + term
    o_ref[...] = acc.astype(BF16)


def _merge(hb, out_a, out_b, out_c, w_gate_b, b_gate, wa_b, wb_b, wc_b, layer):
    t, d = hb.shape
    tm, tn = 1024, 512
    nj = d // tn
    row = lambda w: pl.BlockSpec((tm, w), lambda j, i: (i, 0))
    gate_w = lambda x: pl.BlockSpec((None, d, tn), lambda j, i, x=x: (layer, 0, x * nj + j))
    gate_b = lambda x: pl.BlockSpec((None, 1, tn), lambda j, i, x=x: (layer, 0, x * nj + j))
    br_w = lambda k: pl.BlockSpec((None, k, tn), lambda j, i: (layer, 0, j))
    bg = b_gate.reshape(b_gate.shape[0], 1, -1)
    return pl.pallas_call(
        _merge_kernel,
        out_shape=jax.ShapeDtypeStruct((t, d), BF16),
        grid=(nj, t // tm),
        in_specs=[row(d), row(out_a.shape[1]), row(out_b.shape[1]), row(out_c.shape[1]),
                  gate_w(0), gate_w(1), gate_w(2), gate_b(0), gate_b(1), gate_b(2),
                  br_w(wa_b.shape[1]), br_w(wb_b.shape[1]), br_w(wc_b.shape[1])],
        out_specs=pl.BlockSpec((tm, tn), lambda j, i: (i, j)),
        compiler_params=_cparams(("arbitrary",) * 2),
        name="branch_merge",
    )(hb, out_a, out_b, out_c, w_gate_b, w_gate_b, w_gate_b, bg, bg, bg, wa_b, wb_b, wc_b)


def _layer_norm_rows(y, g, b):
    mu = jnp.mean(y, axis=-1, keepdims=True)
    yc = y - mu
    var = jnp.mean(yc * yc, axis=-1, keepdims=True)
    return yc * lax.rsqrt(var + LN_EPS) * g + b


def _route(lt):
    tm = lt.shape[1]
    gl = lt[0:N_GROUPS]
    gmax = jnp.max(gl, axis=0, keepdims=True)
    ge = jnp.exp(gl - gmax)
    gp = ge / jnp.sum(ge, axis=0, keepdims=True)
    g_val = jnp.max(gp, axis=0, keepdims=True)
    row = lax.broadcasted_iota(jnp.int32, (N_GROUPS, tm), 0)
    g_sel = jnp.min(jnp.where(gp == g_val, row, N_GROUPS), axis=0, keepdims=True)
    el = jnp.zeros((EXPERTS_PER_GROUP, tm), F32)
    for g in range(N_GROUPS):
        lo = 8 + g * EXPERTS_PER_GROUP
        el = jnp.where(g_sel == g, lt[lo:lo + EXPERTS_PER_GROUP], el)
    v0 = jnp.max(el, axis=0, keepdims=True)
    i0 = jnp.min(jnp.where(el == v0, row, EXPERTS_PER_GROUP), axis=0, keepdims=True)
    el1 = jnp.where(row == i0, -jnp.inf, el)
    v1 = jnp.max(el1, axis=0, keepdims=True)
    i1 = jnp.min(jnp.where(el1 == v1, row, EXPERTS_PER_GROUP), axis=0, keepdims=True)
    e1 = jnp.exp(v1 - v0)
    den = 1.0 + e1
    w0 = 1.0 / den * g_val
    w1 = e1 / den * g_val
    id0 = (g_sel * EXPERTS_PER_GROUP + i0).astype(F32)
    id1 = (g_sel * EXPERTS_PER_GROUP + i1).astype(F32)
    return id0, id1, w0, w1


def _outproj_ln_route_kernel(m_ref, w_ref, h_ref, g_ref, b_ref, wr_ref, hf_ref, rt_ref, mix_buf,
                             wrh_ref, wrl_ref, *, alpha):
    @pl.when(pl.program_id(0) == 0)
    def _():
        mix_buf[...] = jnp.zeros_like(mix_buf)
        wr = wr_ref[...]
        wr_hi = wr.astype(BF16)
        wrh_ref[...] = wr_hi
        wrl_ref[...] = (wr - wr_hi.astype(F32)).astype(BF16)

    prev = mix_buf[...]
    mix_buf[...] = jnp.dot(m_ref[...], w_ref[...], preferred_element_type=F32)
    h1 = _layer_norm_rows(alpha * h_ref[...] + prev, g_ref[...], b_ref[...])
    hf_ref[...] = h1
    h_hi = h1.astype(BF16)
    h_lo = (h1 - h_hi.astype(F32)).astype(BF16)
    logits = (jnp.dot(h_hi, wrh_ref[...], preferred_element_type=F32)
              + jnp.dot(h_lo, wrh_ref[...], preferred_element_type=F32)
              + jnp.dot(h_hi, wrl_ref[...], preferred_element_type=F32))
    id0, id1, w0, w1 = _route(logits.T)
    rt_ref[...] = jnp.concatenate([id0, id1, w0, w1, jnp.zeros((4, logits.shape[0]), F32)], axis=0)


def _outproj_ln_route(merged, w_out_b, layer, h, ln_g, ln_b, w_router, alpha):
    t, d = h.shape
    tm = 512
    n = t // tm
    cur = lambda i: (jnp.minimum(i, n - 1), 0)
    prv = lambda i: (jnp.maximum(i - 1, 0), 0)
    return pl.pallas_call(
        functools.partial(_outproj_ln_route_kernel, alpha=alpha),
        out_shape=(jax.ShapeDtypeStruct((t, d), F32), jax.ShapeDtypeStruct((8, t), F32)),
        grid=(n + 1,),
        in_specs=[pl.BlockSpec((tm, d), cur),
                  pl.BlockSpec((None, d, d), lambda i: (layer, 0, 0)),
                  pl.BlockSpec((tm, d), prv),
                  pl.BlockSpec((1, d), lambda i: (0, 0)),
                  pl.BlockSpec((1, d), lambda i: (0, 0)),
                  pl.BlockSpec((d, HEAD_DIM), lambda i: (0, 0))],
        out_specs=(pl.BlockSpec((tm, d), prv),
                   pl.BlockSpec((8, tm), lambda i: (0, jnp.maximum(i - 1, 0)))),
        scratch_shapes=[pltpu.VMEM((tm, d), F32),
                        pltpu.VMEM((d, HEAD_DIM), BF16), pltpu.VMEM((d, HEAD_DIM), BF16)],
        compiler_params=_cparams(("arbitrary",)),
        name="outproj_ln1_route",
    )(merged, w_out_b, h, ln_g.reshape(1, d), ln_b.reshape(1, d), w_router)


MOE_TILE = 256
GATHER_TILE = 512
DMA_UNROLL = 8
MLP_CHUNKS = 4


def _dispatch_plan(rt, t):
    ids = rt[0:2].astype(jnp.int32).reshape(-1)
    tok = jnp.tile(jnp.arange(t, dtype=jnp.int32), 2)
    onehot = (ids[:, None] == jnp.arange(N_EXPERTS, dtype=jnp.int32)[None, :]).astype(jnp.int32)
    csum = jnp.cumsum(onehot, axis=0)
    counts = csum[-1]
    rank = jnp.take_along_axis(csum, ids[:, None], axis=1)[:, 0] - 1
    padded = (counts + MOE_TILE - 1) // MOE_TILE * MOE_TILE
    ends = jnp.cumsum(padded)
    pos = (ends - padded)[ids] + rank
    p_rows = 2 * t + N_EXPERTS * MOE_TILE
    src = jnp.zeros((p_rows + MOE_TILE,), jnp.int32).at[pos].set(tok)
    n_tiles = p_rows // MOE_TILE
    n_used = (ends[-1] // MOE_TILE).astype(jnp.int32)
    tile_idx = jnp.arange(n_tiles, dtype=jnp.int32)
    tile_start = jnp.minimum(tile_idx, n_used - 1) * MOE_TILE
    tile_e = jnp.sum((ends[None, :] <= tile_start[:, None]).astype(jnp.int32), axis=1)
    tile_e = jnp.minimum(tile_e, N_EXPERTS - 1)
    w_cols = rt[2:4].reshape(2, t, 1)
    return src, pos.astype(jnp.int32), w_cols, tile_e, n_used.reshape(1)


def _expert_mlp_kernel(src_ref, te_ref, nu_ref, h_hbm, wg_ref, wu_ref, wd_ref, y_ref, xbuf, sem):
    i = pl.program_id(0)
    n_used = nu_ref[0]
    slot = i % 2

    def issue_rows(tile, dst_slot, lo, hi):
        for j in range(lo, hi):
            row = src_ref[tile * MOE_TILE + j]
            pltpu.make_async_copy(h_hbm.at[pl.ds(row, 1)], xbuf.at[dst_slot, pl.ds(j, 1)],
                                  sem.at[dst_slot]).start(priority=j % 2)

    def wait_tile(s):
        pltpu.make_async_copy(h_hbm.at[pl.ds(0, MOE_TILE)], xbuf.at[s], sem.at[s]).wait()

    @pl.when(i == 0)
    def _():
        def first(c, carry):
            base = pl.multiple_of(c * DMA_UNROLL, DMA_UNROLL)
            for j in range(DMA_UNROLL):
                row = src_ref[base + j]
                pltpu.make_async_copy(h_hbm.at[pl.ds(row, 1)], xbuf.at[0, pl.ds(base + j, 1)],
                                      sem.at[0]).start()
            return carry
        lax.fori_loop(0, MOE_TILE // DMA_UNROLL, first, 0)

    @pl.when(i < n_used)
    def _():
        wait_tile(slot)
        nxt = 1 - slot
        de = wg_ref.shape[2]
        cw = de // MLP_CHUNKS
        n_pieces = 2 * MLP_CHUNKS
        bounds = [MOE_TILE * k // n_pieces for k in range(n_pieces + 1)]
        piece = iter(range(n_pieces))

        def issue_next():
            k = next(piece)
            issue_rows(i + 1, nxt, bounds[k], bounds[k + 1])

        hid = []
        for c in range(MLP_CHUNKS):
            cols = slice(c * cw, (c + 1) * cw)
            issue_next()
            a = jnp.dot(xbuf[slot].astype(BF16), wg_ref[0, :, cols], preferred_element_type=F32)
            issue_next()
            u = jnp.dot(xbuf[slot].astype(BF16), wu_ref[0, :, cols], preferred_element_type=F32)
            hid.append((jax.nn.silu(a) * u).astype(BF16))
        y = None
        for c in range(MLP_CHUNKS):
            part = jnp.dot(hid[c], wd_ref[0, c * cw:(c + 1) * cw, :], preferred_element_type=F32)
            y = part if y is None else y + part
        y_ref[...] = y

    @pl.when(i >= n_used)
    def _():
        y_ref[...] = jnp.zeros_like(y_ref)

    @pl.when(i == n_used)
    def _():
        wait_tile(slot)


def _expert_mlp(h1, src, tile_e, n_used, wg_b, wu_b, wd_b, layer):
    t, d = h1.shape
    p = src.shape[0] - MOE_TILE
    de = wg_b.shape[2]
    exp = lambda i, src, te, nu: (layer * N_EXPERTS + te[i], 0, 0)
    return pl.pallas_call(
        _expert_mlp_kernel,
        out_shape=jax.ShapeDtypeStruct((p, d), F32),
        grid_spec=pltpu.PrefetchScalarGridSpec(
            num_scalar_prefetch=3, grid=(p // MOE_TILE,),
            in_specs=[pl.BlockSpec(memory_space=pl.ANY),
                      pl.BlockSpec((1, d, de), exp),
                      pl.BlockSpec((1, d, de), exp),
                      pl.BlockSpec((1, de, d), exp)],
            out_specs=pl.BlockSpec((MOE_TILE, d), lambda i, src, te, nu: (i, 0)),
            scratch_shapes=[pltpu.VMEM((2, MOE_TILE, d), F32), pltpu.SemaphoreType.DMA((2,))]),
        compiler_params=_cparams(("arbitrary",)),
        name="moe_expert_mlp",
    )(src, tile_e, n_used, h1, wg_b, wu_b, wd_b)


def _combine_ln_kernel(pos_ref, y_hbm, h_ref, w_ref, g_ref, b_ref, of_ref, ob_ref, buf, sem, *, alpha, t):
    i = pl.program_id(0)
    tm = h_ref.shape[0]

    def issue_tile(tile, slot):
        def issue(c, carry):
            base = pl.multiple_of(c * DMA_UNROLL, DMA_UNROLL)
            for k in range(2):
                dst = buf.at[slot, k, pl.ds(base, DMA_UNROLL)]
                for j in range(DMA_UNROLL):
                    row = pos_ref[k * t + tile * tm + base + j]
                    pltpu.make_async_copy(y_hbm.at[pl.ds(row, 1)], dst.at[pl.ds(j, 1)],
                                          sem.at[slot]).start(priority=j % 2)
            return carry

        lax.fori_loop(0, tm // DMA_UNROLL, issue, 0)

    @pl.when(i == 0)
    def _():
        issue_tile(0, 0)

    @pl.when(i + 1 < pl.num_programs(0))
    def _():
        issue_tile(i + 1, (i + 1) % 2)

    slot = i % 2
    for k in range(2):
        pltpu.make_async_copy(y_hbm.at[pl.ds(0, tm)], buf.at[slot, k], sem.at[slot]).wait()
    ffn = w_ref[0] * buf[slot, 0] + w_ref[1] * buf[slot, 1]
    out = _layer_norm_rows(alpha * h_ref[...] + ffn, g_ref[...], b_ref[...])
    of_ref[...] = out
    ob_ref[...] = out.astype(BF16)


def _combine_ln(pos, w_cols, y_sorted, h1, ln_g, ln_b, alpha):
    t, d = h1.shape
    tm = GATHER_TILE
    return pl.pallas_call(
        functools.partial(_combine_ln_kernel, alpha=alpha, t=t),
        out_shape=(jax.ShapeDtypeStruct((t, d), F32), jax.ShapeDtypeStruct((t, d), BF16)),
        grid_spec=pltpu.PrefetchScalarGridSpec(
            num_scalar_prefetch=1, grid=(t // tm,),
            in_specs=[pl.BlockSpec(memory_space=pl.ANY),
                      pl.BlockSpec((tm, d), lambda i, pos: (i, 0)),
                      pl.BlockSpec((2, tm, 1), lambda i, pos: (0, i, 0)),
                      pl.BlockSpec((1, d), lambda i, pos: (0, 0)),
                      pl.BlockSpec((1, d), lambda i, pos: (0, 0))],
            out_specs=(pl.BlockSpec((tm, d), lambda i, pos: (i, 0)),
                       pl.BlockSpec((tm, d), lambda i, pos: (i, 0))),
            scratch_shapes=[pltpu.VMEM((2, 2, tm, d), F32), pltpu.SemaphoreType.DMA((2,))]),
        compiler_params=_cparams(("arbitrary",)),
        name="moe_combine_ln2",
    )(pos, y_sorted, h1, w_cols, ln_g.reshape(1, d), ln_b.reshape(1, d))


def _moe(h1f, rt, wg_b, wu_b, wd_b, layer, ln_g, ln_b, alpha):
    t = h1f.shape[0]
    src, pos, w_cols, tile_e, n_used = _dispatch_plan(rt, t)
    y_sorted = _expert_mlp(h1f, src, tile_e, n_used, wg_b, wu_b, wd_b, layer)
    return _combine_ln(pos, w_cols, y_sorted, h1f, ln_g, ln_b, alpha)


def kernel(x, positions, w_in, w_gate, b_gate, w_branch_a, w_branch_b, w_branch_c, w_out, sinks,
           ln1_g, ln1_b, w_router_group, w_router_expert, w_exp_gate, w_exp_up, w_exp_down,
           ln2_g, ln2_b):
    batch, seq, d = x.shape
    depth = w_in.shape[0]
    t = batch * seq
    alpha = (2 * depth) ** 0.25
    cc, ss = _rope_tables(positions)
    hf = x.reshape(t, d)
    hb = hf.astype(BF16)
    w_in_b, w_gate_b, w_out_b = w_in.astype(BF16), w_gate.astype(BF16), w_out.astype(BF16)
    wa_b, wb_b, wc_b = w_branch_a.astype(BF16), w_branch_b.astype(BF16), w_branch_c.astype(BF16)
    de = w_exp_gate.shape[-1]
    wg_b = w_exp_gate.astype(BF16).reshape(depth * N_EXPERTS, d, de)
    wu_b = w_exp_up.astype(BF16).reshape(depth * N_EXPERTS, d, de)
    wd_b = w_exp_down.astype(BF16).reshape(depth * N_EXPERTS, de, d)
    for l in range(depth):
        qkv = _qkv_proj(hb, w_in_b, l, cc, ss)
        out_a = _dilated_attention(qkv, batch, seq)
        out_b = _moba_attention(qkv, batch, seq)
        out_c = _swa_attention(qkv, sinks[l], batch, seq)
        merged = _merge(hb, out_a, out_b, out_c, w_gate_b, b_gate, wa_b, wb_b, wc_b, l)
        w_router = jnp.zeros((d, HEAD_DIM), F32)
        w_router = w_router.at[:, 0:N_GROUPS].set(w_router_group[l])
        w_router = w_router.at[:, 8:8 + N_EXPERTS].set(w_router_expert[l])
        h1f, rt = _outproj_ln_route(merged, w_out_b, l, hf, ln1_g[l], ln1_b[l], w_router, alpha)
        hf, hb = _moe(h1f, rt, wg_b, wu_b, wd_b, l, ln2_g[l], ln2_b[l], alpha)
    return hf.reshape(batch, seq, d)
```

```python
import functools

import jax
import jax.numpy as jnp
from jax import lax
from jax.experimental import pallas as pl
from jax.experimental.pallas import tpu as pltpu

F32 = jnp.float32
BF16 = jnp.bfloat16

HEAD_DIM = 128
ATTN_SCALE = HEAD_DIM ** -0.5
ROPE_THETA = 10000.0
A_PATTERNS = ((128, 1), (512, 4), (2048, 16))
A_HEADS_PER_GROUP = 2
A_HEADS = A_HEADS_PER_GROUP * len(A_PATTERNS)
B_HEADS = 6
MOBA_BLOCK = 256
MOBA_TOPK = 3
C_Q_HEADS = 8
C_KV_HEADS = 2
C_GROUP = C_Q_HEADS // C_KV_HEADS
C_WINDOW = 128
BAND = 128
N_Q_HEADS = A_HEADS + B_HEADS + C_Q_HEADS
N_KV_HEADS = A_HEADS + B_HEADS + C_KV_HEADS
N_GROUPS = 4
EXPERTS_PER_GROUP = 4
N_EXPERTS = N_GROUPS * EXPERTS_PER_GROUP
PAIRS_PER_GROUP = EXPERTS_PER_GROUP * (EXPERTS_PER_GROUP - 1) // 2
N_CLASSES = N_GROUPS * PAIRS_PER_GROUP
PAIR_X = (0, 0, 0, 1, 1, 3)
PAIR_Y = (1, 2, 3, 3, 2, 2)
LN_EPS = 1e-5
NEG = -1e30

V7X_VMEM_BYTES = 64 * 1024 * 1024
VMEM_LIMIT = V7X_VMEM_BYTES - 8 * 1024 * 1024


def _cparams(sem, vmem=VMEM_LIMIT):
    return pltpu.CompilerParams(dimension_semantics=sem, vmem_limit_bytes=vmem)


ROPE_SLOT_Q, ROPE_SLOT_K, ROPE_SLOT_V = 0, 1, 2


def _rope_table_kernel(pos_ref, inv_ref, cc_ref, ss_ref):
    ang = pos_ref[...].astype(F32) * inv_ref[...]
    lane = lax.broadcasted_iota(jnp.int32, ang.shape, 1)
    s = jnp.sin(ang)
    c = jnp.cos(ang)
    s = jnp.where(lane < HEAD_DIM // 2, -s, s)
    cc_ref[ROPE_SLOT_Q] = c * ATTN_SCALE
    ss_ref[ROPE_SLOT_Q] = s * ATTN_SCALE
    cc_ref[ROPE_SLOT_K] = c
    ss_ref[ROPE_SLOT_K] = s
    cc_ref[ROPE_SLOT_V] = jnp.ones_like(c)
    ss_ref[ROPE_SLOT_V] = jnp.zeros_like(s)


def _rope_tables(positions):
    t = positions.size
    tm = 2048
    inv = ROPE_THETA ** (-jnp.arange(0, HEAD_DIM, 2, dtype=F32) / HEAD_DIM)
    inv2 = jnp.concatenate([inv, inv]).reshape(1, HEAD_DIM)
    pos = positions.reshape(t, 1)
    return pl.pallas_call(
        _rope_table_kernel,
        out_shape=(jax.ShapeDtypeStruct((3, t, HEAD_DIM), F32),) * 2,
        grid=(t // tm,),
        in_specs=[pl.BlockSpec((tm, 1), lambda i: (i, 0)),
                  pl.BlockSpec((1, HEAD_DIM), lambda i: (0, 0))],
        out_specs=(pl.BlockSpec((3, tm, HEAD_DIM), lambda i: (0, i, 0)),) * 2,
        compiler_params=_cparams(("arbitrary",)),
        name="rope_tables",
    )(pos, inv2)


def _qkv_kernel(x_ref, w_ref, cc_ref, ss_ref, o_ref, *, heads_per_tile):
    j = pl.program_id(1)
    acc = jnp.dot(x_ref[...], w_ref[...], preferred_element_type=F32)
    for hh in range(heads_per_tile):
        head = j * heads_per_tile + hh
        slot = (head >= N_Q_HEADS).astype(jnp.int32) + (head >= N_Q_HEADS + N_KV_HEADS).astype(jnp.int32)
        t = acc[:, hh * HEAD_DIM:(hh + 1) * HEAD_DIM]
        r = t * cc_ref[slot] + pltpu.roll(t, HEAD_DIM // 2, 1) * ss_ref[slot]
        o_ref[hh] = r.astype(BF16)


def _qkv_proj(hb, w_in_b, layer, cc, ss):
    t, d = hb.shape
    n = w_in_b.shape[2]
    tm, tn = 1024, 1024
    hpt = tn // HEAD_DIM
    return pl.pallas_call(
        functools.partial(_qkv_kernel, heads_per_tile=hpt),
        out_shape=jax.ShapeDtypeStruct((n // HEAD_DIM, t, HEAD_DIM), BF16),
        grid=(t // tm, n // tn),
        in_specs=[pl.BlockSpec((tm, d), lambda i, j: (i, 0)),
                  pl.BlockSpec((None, d, tn), lambda i, j: (layer, 0, j)),
                  pl.BlockSpec((3, tm, HEAD_DIM), lambda i, j: (0, i, 0)),
                  pl.BlockSpec((3, tm, HEAD_DIM), lambda i, j: (0, i, 0))],
        out_specs=pl.BlockSpec((hpt, tm, HEAD_DIM), lambda i, j: (j, i, 0)),
        compiler_params=_cparams(("arbitrary", "arbitrary")),
        name="qkv_rope",
    )(hb, w_in_b, cc, ss)


def _band_bias(max_dist):
    qi = lax.broadcasted_iota(jnp.int32, (BAND, 2 * BAND), 0)
    kj = lax.broadcasted_iota(jnp.int32, (BAND, 2 * BAND), 1)
    dist = qi + BAND - kj
    valid = (dist >= 0) & (dist <= max_dist)
    bias = jnp.where(valid, 0.0, NEG).astype(F32)
    bias_first = jnp.where(valid & (kj >= BAND), 0.0, NEG).astype(F32)
    return bias, bias_first


def _band_block(qb, kwin, vwin, bias):
    g = qb.shape[0] // BAND
    s = lax.dot_general(qb, kwin, (((1,), (1,)), ((), ())), preferred_element_type=F32)
    s = s.reshape(g, BAND, 2 * BAND) + bias[None]
    m = jnp.max(s, axis=-1, keepdims=True)
    p = jnp.exp(s - m)
    l = jnp.sum(p, axis=-1, keepdims=True)
    o = jnp.dot(p.reshape(g * BAND, 2 * BAND).astype(BF16), vwin, preferred_element_type=F32)
    o = o.reshape(g, BAND, HEAD_DIM) / l
    return o, m + jnp.log(l)


def _swa_kernel(sink_ref, q_ref, k_ref, kh_ref, v_ref, vh_ref, o_ref, *, tq):
    kv = pl.program_id(1)
    t = pl.program_id(2)
    bias, bias_first = _band_bias(C_WINDOW - 1)
    for blk in range(tq // BAND):
        qb = q_ref[:, blk * BAND:(blk + 1) * BAND, :].reshape(C_GROUP * BAND, HEAD_DIM)
        if blk == 0:
            kwin = jnp.concatenate([kh_ref[0], k_ref[0, :BAND]], axis=0)
            vwin = jnp.concatenate([vh_ref[0], v_ref[0, :BAND]], axis=0)
            b = jnp.where(t == 0, bias_first, bias)
        else:
            kwin = k_ref[0, (blk - 1) * BAND:(blk + 1) * BAND]
            vwin = v_ref[0, (blk - 1) * BAND:(blk + 1) * BAND]
            b = bias
        o, lse = _band_block(qb, kwin, vwin, b)
        for g in range(C_GROUP):
            keep = jax.nn.sigmoid(lse[g] - sink_ref[kv * C_GROUP + g])
            o_ref[blk * BAND:(blk + 1) * BAND, g * HEAD_DIM:(g + 1) * HEAD_DIM] = (o[g] * keep).astype(BF16)


def _swa_attention(qkv, sinks, batch, seq):
    tq = 512
    nt = seq // tq
    q0 = A_HEADS + B_HEADS
    k0 = N_Q_HEADS + A_HEADS + B_HEADS
    v0 = N_Q_HEADS + N_KV_HEADS + A_HEADS + B_HEADS
    rb = tq // BAND

    def halo(b, kv, t):
        return jnp.maximum(b * (seq // BAND) + t * rb - 1, 0)

    return pl.pallas_call(
        functools.partial(_swa_kernel, tq=tq),
        out_shape=jax.ShapeDtypeStruct((batch * seq, C_Q_HEADS * HEAD_DIM), BF16),
        grid=(batch, C_KV_HEADS, nt),
        in_specs=[
            pl.BlockSpec(memory_space=pltpu.SMEM),
            pl.BlockSpec((C_GROUP, tq, HEAD_DIM), lambda b, kv, t: (q0 // C_GROUP + kv, b * nt + t, 0)),
            pl.BlockSpec((1, tq, HEAD_DIM), lambda b, kv, t: (k0 + kv, b * nt + t, 0)),
            pl.BlockSpec((1, BAND, HEAD_DIM), lambda b, kv, t: (k0 + kv, halo(b, kv, t), 0)),
            pl.BlockSpec((1, tq, HEAD_DIM), lambda b, kv, t: (v0 + kv, b * nt + t, 0)),
            pl.BlockSpec((1, BAND, HEAD_DIM), lambda b, kv, t: (v0 + kv, halo(b, kv, t), 0)),
        ],
        out_specs=pl.BlockSpec((tq, C_GROUP * HEAD_DIM), lambda b, kv, t: (b * nt + t, kv)),
        compiler_params=_cparams(("arbitrary",) * 3),
        name="swa_sink",
    )(sinks, qkv, qkv, qkv, qkv, qkv)


def _dilated_kernel(*refs, seq):
    ng = len(A_PATTERNS)
    in_refs = refs[:3 * ng]
    o_ref = refs[3 * ng]
    qf, kf, vf, og, lg = refs[3 * ng + 1:]
    for g, (window, dil) in enumerate(A_PATTERNS):
        q_ref, k_ref, v_ref = in_refs[3 * g:3 * g + 3]
        qf[...] = q_ref[0].astype(F32)
        kf[...] = k_ref[0].astype(F32)
        vf[...] = v_ref[0].astype(F32)
        nblk = seq // dil // BAND
        bias, bias_first = _band_bias(window // dil)

        for r in range(dil):
            for n in range(nblk):
                own = pl.ds(r + dil * BAND * n, BAND, stride=dil)
                prev = pl.ds(r + dil * BAND * max(n - 1, 0), BAND, stride=dil)
                qb = qf[own, :].astype(BF16)
                kwin = jnp.concatenate([kf[prev, :], kf[own, :]], axis=0).astype(BF16)
                vwin = jnp.concatenate([vf[prev, :], vf[own, :]], axis=0).astype(BF16)
                o, lse = _band_block(qb, kwin, vwin, bias_first if n == 0 else bias)
                og[g, own, :] = o[0]
                lg[g, own, :] = jnp.broadcast_to(lse[0], (BAND, HEAD_DIM))

    rows = 256

    def combine(c, carry):
        sl = pl.ds(pl.multiple_of(c * rows, rows), rows)
        ls = [lg[g, sl, :] for g in range(ng)]
        m = functools.reduce(jnp.maximum, ls)
        es = [jnp.exp(l - m) for l in ls]
        num = functools.reduce(jnp.add, [e * og[g, sl, :] for g, e in enumerate(es)])
        o_ref[sl, :] = (num / functools.reduce(jnp.add, es)).astype(BF16)
        return carry

    lax.fori_loop(0, seq // rows, combine, 0)


def _dilated_attention(qkv, batch, seq):
    ng = len(A_PATTERNS)
    k0 = N_Q_HEADS
    v0 = N_Q_HEADS + N_KV_HEADS
    in_specs = []
    for g in range(ng):
        for base in (0, k0, v0):
            in_specs.append(pl.BlockSpec(
                (1, seq, HEAD_DIM),
                lambda b, j, h=base + g * A_HEADS_PER_GROUP: (h + j, b, 0)))
    return pl.pallas_call(
        functools.partial(_dilated_kernel, seq=seq),
        out_shape=jax.ShapeDtypeStruct((batch * seq, A_HEADS_PER_GROUP * HEAD_DIM), BF16),
        grid=(batch, A_HEADS_PER_GROUP),
        in_specs=in_specs,
        out_specs=pl.BlockSpec((seq, HEAD_DIM), lambda b, j: (b, j)),
        scratch_shapes=[pltpu.VMEM((seq, HEAD_DIM), F32)] * 3
        + [pltpu.VMEM((ng, seq, HEAD_DIM), F32)] * 2,
        compiler_params=_cparams(("arbitrary",) * 2),
        name="dilated_attn",
    )(*([qkv] * (3 * ng)))


MOBA_TQ = 1024


def _online_update(state, s, v):
    mc = jnp.max(s, axis=1, keepdims=True)
    if state is None:
        p = jnp.exp(s - mc)
        return mc, jnp.sum(p, axis=1, keepdims=True), jnp.dot(p.astype(BF16), v, preferred_element_type=F32)
    m, l, acc = state
    m_new = jnp.maximum(m, mc)
    alpha = jnp.exp(m - m_new)
    p = jnp.exp(s - m_new)
    l_new = alpha * l + jnp.sum(p, axis=1, keepdims=True)
    acc_new = alpha * acc + jnp.dot(p.astype(BF16), v, preferred_element_type=F32)
    return m_new, l_new, acc_new


def _moba_kernel(q_ref, k_ref, v_ref, o_ref, kaug, kmean, *, seq):
    t = pl.program_id(2)
    nblk = seq // MOBA_BLOCK
    blk = MOBA_BLOCK
    tq = MOBA_TQ
    nb = tq // blk

    @pl.when(t == 0)
    def _():
        kaug[:, :HEAD_DIM] = k_ref[0]
        row = lax.broadcasted_iota(jnp.int32, (seq, HEAD_DIM), 0)
        lane = lax.broadcasted_iota(jnp.int32, (seq, HEAD_DIM), 1)
        kaug[:, HEAD_DIM:] = jnp.where(row // blk == lane, 1.0, 0.0).astype(BF16)
        kmean[...] = jnp.zeros_like(kmean)
        for n in range(nblk):
            kb = k_ref[0, n * blk:(n + 1) * blk, :].astype(F32)
            kmean[n:n + 1, :] = jnp.mean(kb, axis=0, keepdims=True)

    q = q_ref[0]
    km = kmean[...]
    km_hi = km.astype(BF16)
    km_lo = (km - km_hi.astype(F32)).astype(BF16)
    nt = (((1,), (1,)), ((), ()))
    gate = (lax.dot_general(km_hi, q, nt, preferred_element_type=F32)
            + lax.dot_general(km_lo, q, nt, preferred_element_type=F32))
    nrow = -(-nblk // 8) * 8
    row = lax.broadcasted_iota(jnp.int32, (nrow, tq), 0).astype(F32)
    own = (t * nb + lax.broadcasted_iota(jnp.int32, (nrow, tq), 1) // blk).astype(F32)
    g = jnp.where(row < own, gate[:nrow], NEG)
    sel = jnp.zeros((nrow, tq), jnp.bool_)
    for _ in range(MOBA_TOPK):
        mx = jnp.max(g, axis=0, keepdims=True)
        first = jnp.min(jnp.where(g == mx, row, float(nrow)), axis=0, keepdims=True)
        pick = row == first
        sel = sel | pick
        g = jnp.where(pick, NEG, g)
    visible = (sel & (row < own)) | (row == own)
    bias_t = jnp.concatenate([jnp.where(visible, 0.0, NEG), jnp.zeros((HEAD_DIM - nrow, tq), F32)], axis=0)
    qaug = jnp.concatenate([q, bias_t.T.astype(BF16)], axis=1)

    ri = lax.broadcasted_iota(jnp.int32, (blk, blk), 0)
    ci = lax.broadcasted_iota(jnp.int32, (blk, blk), 1)
    causal = ri >= ci

    for c in range(seq // tq):
        @pl.when(t == c)
        def _(c=c):
            state = None
            for cc in range(c):
                keys = slice(cc * tq, (cc + 1) * tq)
                s = lax.dot_general(qaug, kaug[keys, :], nt, preferred_element_type=F32)
                state = _online_update(state, s, v_ref[0, keys, :])
            n_split = 2 if c == 0 else 1
            hb = nb // n_split
            parts = []
            for h in range(n_split):
                rows = slice(h * hb * blk, (h + 1) * hb * blk)
                keys = slice(c * tq, c * tq + (h + 1) * hb * blk)
                s = lax.dot_general(qaug[rows], kaug[keys, :], nt, preferred_element_type=F32)
                s = jnp.concatenate(
                    [jnp.concatenate(
                        [jnp.where(causal, s[a * blk:(a + 1) * blk, e * blk:(e + 1) * blk], NEG)
                         if e == a + h * hb else s[a * blk:(a + 1) * blk, e * blk:(e + 1) * blk]
                         for e in range((h + 1) * hb)], axis=1) for a in range(hb)], axis=0)
                st = None if state is None else tuple(x[rows] for x in state)
                parts.append(_online_update(st, s, v_ref[0, keys, :]))
            m, l, acc = (jnp.concatenate([p[i] for p in parts], axis=0) for i in range(3))
            o_ref[...] = (acc / l).astype(BF16)


def _moba_attention(qkv, batch, seq):
    nt = seq // MOBA_TQ
    q0 = A_HEADS
    k0 = N_Q_HEADS + A_HEADS
    v0 = N_Q_HEADS + N_KV_HEADS + A_HEADS
    return pl.pallas_call(
        functools.partial(_moba_kernel, seq=seq),
        out_shape=jax.ShapeDtypeStruct((batch * seq, B_HEADS * HEAD_DIM), BF16),
        grid=(batch, B_HEADS, nt),
        in_specs=[
            pl.BlockSpec((1, MOBA_TQ, HEAD_DIM), lambda b, h, t: (q0 + h, b * nt + t, 0)),
            pl.BlockSpec((1, seq, HEAD_DIM), lambda b, h, t: (k0 + h, b, 0)),
            pl.BlockSpec((1, seq, HEAD_DIM), lambda b, h, t: (v0 + h, b, 0)),
        ],
        out_specs=pl.BlockSpec((MOBA_TQ, HEAD_DIM), lambda b, h, t: (b * nt + t, h)),
        scratch_shapes=[pltpu.VMEM((seq, 2 * HEAD_DIM), BF16), pltpu.VMEM((HEAD_DIM, HEAD_DIM), F32)],
        compiler_params=_cparams(("arbitrary",) * 3),
        name="moba_attn",
    )(qkv, qkv, qkv)


def _merge_kernel(h_ref, oa_ref, ob_ref, oc_ref, g0_ref, g1_ref, g2_ref, b0_ref, b1_ref, b2_ref,
                  wa_ref, wb_ref, wc_ref, o_ref):
    x = h_ref[...]
    acc = None
    for g_ref, b_ref, br_ref, w_ref in ((g0_ref, b0_ref, oa_ref, wa_ref),
                                        (g1_ref, b1_ref, ob_ref, wb_ref),
                                        (g2_ref, b2_ref, oc_ref, wc_ref)):
        gate = jax.nn.sigmoid(jnp.dot(x, g_ref[...], preferred_element_type=F32) + b_ref[...])
        term = gate * jnp.dot(br_ref[...], w_ref[...], preferred_element_type=F32)
        acc = term if acc is None else acc + term
    o_ref[...] = acc.astype(BF16)


def _merge(hb, out_a, out_b, out_c, w_gate_b, b_gate, wa_b, wb_b, wc_b, layer):
    t, d = hb.shape
    tm, tn = 1024, 512
    nj = d // tn
    row = lambda w: pl.BlockSpec((tm, w), lambda j, i: (i, 0))
    gate_w = lambda x: pl.BlockSpec((None, d, tn), lambda j, i, x=x: (layer, 0, x * nj + j))
    gate_b = lambda x: pl.BlockSpec((None, 1, tn), lambda j, i, x=x: (layer, 0, x * nj + j))
    br_w = lambda k: pl.BlockSpec((None, k, tn), lambda j, i: (layer, 0, j))
    bg = b_gate.reshape(b_gate.shape[0], 1, -1)
    return pl.pallas_call(
        _merge_kernel,
        out_shape=jax.ShapeDtypeStruct((t, d), BF16),
        grid=(nj, t // tm),
        in_specs=[row(d), row(out_a.shape[1]), row(out_b.shape[1]), row(out_c.shape[1]),
                  gate_w(0), gate_w(1), gate_w(2), gate_b(0), gate_b(1), gate_b(2),
                  br_w(wa_b.shape[1]), br_w(wb_b.shape[1]), br_w(wc_b.shape[1])],
        out_specs=pl.BlockSpec((tm, tn), lambda j, i: (i, j)),
        compiler_params=_cparams(("arbitrary",) * 2),
        name="branch_merge",
    )(hb, out_a, out_b, out_c, w_gate_b, w_gate_b, w_gate_b, bg, bg, bg, wa_b, wb_b, wc_b)


def _layer_norm_rows(y, g, b):
    mu = jnp.mean(y, axis=-1, keepdims=True)
    yc = y - mu
    var = jnp.mean(yc * yc, axis=-1, keepdims=True)
    return yc * lax.rsqrt(var + LN_EPS) * g + b


def _route(lt):
    tm = lt.shape[1]
    gl = lt[0:N_GROUPS]
    gmax = jnp.max(gl, axis=0, keepdims=True)
    ge = jnp.exp(gl - gmax)
    gp = ge / jnp.sum(ge, axis=0, keepdims=True)
    g_val = jnp.max(gp, axis=0, keepdims=True)
    row = lax.broadcasted_iota(jnp.int32, (N_GROUPS, tm), 0)
    g_sel = jnp.min(jnp.where(gp == g_val, row, N_GROUPS), axis=0, keepdims=True)
    el = jnp.zeros((EXPERTS_PER_GROUP, tm), F32)
    for g in range(N_GROUPS):
        lo = 8 + g * EXPERTS_PER_GROUP
        el = jnp.where(g_sel == g, lt[lo:lo + EXPERTS_PER_GROUP], el)
    v0 = jnp.max(el, axis=0, keepdims=True)
    i0 = jnp.min(jnp.where(el == v0, row, EXPERTS_PER_GROUP), axis=0, keepdims=True)
    el1 = jnp.where(row == i0, -jnp.inf, el)
    v1 = jnp.max(el1, axis=0, keepdims=True)
    i1 = jnp.min(jnp.where(el1 == v1, row, EXPERTS_PER_GROUP), axis=0, keepdims=True)
    e1 = jnp.exp(v1 - v0)
    den = 1.0 + e1
    w0 = 1.0 / den * g_val
    w1 = e1 / den * g_val
    a = jnp.minimum(i0, i1)
    b = jnp.maximum(i0, i1)
    swap = (a == 2) & (b == 3)
    x = jnp.where(swap, b, a)
    y = jnp.where(swap, a, b)
    rank = jnp.where(a == 0, b - 1, jnp.where(a == 1, jnp.where(b == 3, 3, 4), 5))
    cls = (g_sel * PAIRS_PER_GROUP + rank).astype(F32)
    wx = jnp.where(i0 == x, w0, w1)
    wy = jnp.where(i0 == y, w0, w1)
    return cls, wx, wy


def _outproj_ln_route_kernel(m_ref, w_ref, h_ref, g_ref, b_ref, wr_ref, hf_ref, rt_ref, mix_buf,
                             wrh_ref, wrl_ref, *, alpha):
    @pl.when(pl.program_id(0) == 0)
    def _():
        mix_buf[...] = jnp.zeros_like(mix_buf)
        wr = wr_ref[...]
        wr_hi = wr.astype(BF16)
        wrh_ref[...] = wr_hi
        wrl_ref[...] = (wr - wr_hi.astype(F32)).astype(BF16)

    prev = mix_buf[...]
    mix_buf[...] = jnp.dot(m_ref[...], w_ref[...], preferred_element_type=F32)
    h1 = _layer_norm_rows(alpha * h_ref[...] + prev, g_ref[...], b_ref[...])
    h_hi = h1.astype(BF16)
    h_lo = (h1 - h_hi.astype(F32)).astype(BF16)
    logits = (jnp.dot(h_hi, wrh_ref[...], preferred_element_type=F32)
              + jnp.dot(h_lo, wrh_ref[...], preferred_element_type=F32)
              + jnp.dot(h_hi, wrl_ref[...], preferred_element_type=F32))
    cls, wx, wy = _route(logits.T)
    tm = logits.shape[0]
    rt_ref[...] = jnp.concatenate([cls, wx, wy, jnp.zeros((5, tm), F32)], axis=0)
    ext = jnp.concatenate([wx, wy, jnp.zeros((HEAD_DIM - 2, tm), F32)], axis=0).T
    d = h1.shape[1]
    hf_ref[:, :d] = h1
    hf_ref[:, d:] = ext


def _outproj_ln_route(merged, w_out_b, layer, h, ln_g, ln_b, w_router, alpha):
    t, d = h.shape
    tm = 512
    n = t // tm
    cur = lambda i: (jnp.minimum(i, n - 1), 0)
    prv = lambda i: (jnp.maximum(i - 1, 0), 0)
    return pl.pallas_call(
        functools.partial(_outproj_ln_route_kernel, alpha=alpha),
        out_shape=(jax.ShapeDtypeStruct((t, d + HEAD_DIM), F32), jax.ShapeDtypeStruct((8, t), F32)),
        grid=(n + 1,),
        in_specs=[pl.BlockSpec((tm, d), cur),
                  pl.BlockSpec((None, d, d), lambda i: (layer, 0, 0)),
                  pl.BlockSpec((tm, d), prv),
                  pl.BlockSpec((1, d), lambda i: (0, 0)),
                  pl.BlockSpec((1, d), lambda i: (0, 0)),
                  pl.BlockSpec((d, HEAD_DIM), lambda i: (0, 0))],
        out_specs=(pl.BlockSpec((tm, d + HEAD_DIM), prv),
                   pl.BlockSpec((8, tm), lambda i: (0, jnp.maximum(i - 1, 0)))),
        scratch_shapes=[pltpu.VMEM((tm, d), F32),
                        pltpu.VMEM((d, HEAD_DIM), BF16), pltpu.VMEM((d, HEAD_DIM), BF16)],
        compiler_params=_cparams(("arbitrary",)),
        name="outproj_ln1_route",
    )(merged, w_out_b, h, ln_g.reshape(1, d), ln_b.reshape(1, d), w_router)


MOE_TILE = 256
GATHER_TILE = 512
DMA_UNROLL = 8
MLP_CHUNKS = 4


def _dispatch_plan(rt, t):
    cls = rt[0].astype(jnp.int32)
    onehot = (cls[:, None] == jnp.arange(N_CLASSES, dtype=jnp.int32)[None, :]).astype(jnp.int32)
    csum = jnp.cumsum(onehot, axis=0)
    counts = csum[-1]
    rank = jnp.take_along_axis(csum, cls[:, None], axis=1)[:, 0] - 1
    padded = (counts + MOE_TILE - 1) // MOE_TILE * MOE_TILE
    ends = jnp.cumsum(padded)
    pos = ((ends - padded)[cls] + rank).astype(jnp.int32)
    p_rows = t + N_CLASSES * MOE_TILE
    tok = jnp.arange(t, dtype=jnp.int32)
    src = jnp.zeros((p_rows + MOE_TILE,), jnp.int32).at[pos].set(tok)
    n_tiles = p_rows // MOE_TILE
    n_used = (ends[-1] // MOE_TILE).astype(jnp.int32)
    tile_idx = jnp.arange(n_tiles, dtype=jnp.int32)
    tile_start = jnp.minimum(tile_idx, n_used - 1) * MOE_TILE
    tile_c = jnp.sum((ends[None, :] <= tile_start[:, None]).astype(jnp.int32), axis=1)
    tile_c = jnp.minimum(tile_c, N_CLASSES - 1)
    group = tile_c // PAIRS_PER_GROUP
    pair = tile_c % PAIRS_PER_GROUP
    tile_x = group * EXPERTS_PER_GROUP + jnp.asarray(PAIR_X, jnp.int32)[pair]
    tile_y = group * EXPERTS_PER_GROUP + jnp.asarray(PAIR_Y, jnp.int32)[pair]
    return src, pos, tile_x.astype(jnp.int32), tile_y.astype(jnp.int32), n_used.reshape(1)


def _expert_mlp_kernel(src_ref, tx_ref, ty_ref, nu_ref, h_hbm, wgx_ref, wux_ref, wdx_ref,
                       wgy_ref, wuy_ref, wdy_ref, y_ref, xbuf, sem):
    i = pl.program_id(0)
    n_used = nu_ref[0]
    slot = i % 2
    d = y_ref.shape[1]

    def issue_rows(tile, dst_slot, lo, hi):
        for j in range(lo, hi):
            row = src_ref[tile * MOE_TILE + j]
            pltpu.make_async_copy(h_hbm.at[pl.ds(row, 1)], xbuf.at[dst_slot, pl.ds(j, 1)],
                                  sem.at[dst_slot]).start(priority=j % 2)

    def wait_tile(s):
        pltpu.make_async_copy(h_hbm.at[pl.ds(0, MOE_TILE)], xbuf.at[s], sem.at[s]).wait()

    @pl.when(i == 0)
    def _():
        def first(c, carry):
            base = pl.multiple_of(c * DMA_UNROLL, DMA_UNROLL)
            for j in range(DMA_UNROLL):
                row = src_ref[base + j]
                pltpu.make_async_copy(h_hbm.at[pl.ds(row, 1)], xbuf.at[0, pl.ds(base + j, 1)],
                                      sem.at[0]).start()
            return carry
        lax.fori_loop(0, MOE_TILE // DMA_UNROLL, first, 0)

    @pl.when(i < n_used)
    def _():
        wait_tile(slot)
        nxt = 1 - slot
        de = wgx_ref.shape[2]
        cw = de // MLP_CHUNKS
        n_pieces = 4 * MLP_CHUNKS
        bounds = [MOE_TILE * k // n_pieces for k in range(n_pieces + 1)]
        piece = iter(range(n_pieces))

        def issue_next():
            k = next(piece)
            issue_rows(i + 1, nxt, bounds[k], bounds[k + 1])

        out = None
        for wg_ref, wu_ref, wd_ref, lane in ((wgx_ref, wux_ref, wdx_ref, d), (wgy_ref, wuy_ref, wdy_ref, d + 1)):
            hid = []
            for c in range(MLP_CHUNKS):
                cols = slice(c * cw, (c + 1) * cw)
                issue_next()
                a = jnp.dot(xbuf[slot, :, :d].astype(BF16), wg_ref[0, :, cols], preferred_element_type=F32)
                issue_next()
                u = jnp.dot(xbuf[slot, :, :d].astype(BF16), wu_ref[0, :, cols], preferred_element_type=F32)
                hid.append((jax.nn.silu(a) * u).astype(BF16))
            y = None
            for c in range(MLP_CHUNKS):
                part = jnp.dot(hid[c], wd_ref[0, c * cw:(c + 1) * cw, :], preferred_element_type=F32)
                y = part if y is None else y + part
            y = y * xbuf[slot, :, lane:lane + 1]
            out = y if out is None else out + y
        y_ref[...] = out

    @pl.when(i >= n_used)
    def _():
        y_ref[...] = jnp.zeros_like(y_ref)

    @pl.when(i == n_used)
    def _():
        wait_tile(slot)


def _expert_mlp(h1x, src, tile_x, tile_y, n_used, wg_b, wu_b, wd_b, layer):
    dx = h1x.shape[1]
    d = dx - HEAD_DIM
    p = src.shape[0] - MOE_TILE
    de = wg_b.shape[2]
    ex = lambda i, src, tx, ty, nu: (layer * N_EXPERTS + tx[i], 0, 0)
    ey = lambda i, src, tx, ty, nu: (layer * N_EXPERTS + ty[i], 0, 0)
    once = pl.Buffered(1)
    return pl.pallas_call(
        _expert_mlp_kernel,
        out_shape=jax.ShapeDtypeStruct((p, d), F32),
        grid_spec=pltpu.PrefetchScalarGridSpec(
            num_scalar_prefetch=4, grid=(p // MOE_TILE,),
            in_specs=[pl.BlockSpec(memory_space=pl.ANY),
                      pl.BlockSpec((1, d, de), ex, pipeline_mode=once),
                      pl.BlockSpec((1, d, de), ex, pipeline_mode=once),
                      pl.BlockSpec((1, de, d), ex, pipeline_mode=once),
                      pl.BlockSpec((1, d, de), ey),
                      pl.BlockSpec((1, d, de), ey),
                      pl.BlockSpec((1, de, d), ey)],
            out_specs=pl.BlockSpec((MOE_TILE, d), lambda i, src, tx, ty, nu: (i, 0)),
            scratch_shapes=[pltpu.VMEM((2, MOE_TILE, dx), F32), pltpu.SemaphoreType.DMA((2,))]),
        compiler_params=_cparams(("arbitrary",)),
        name="moe_expert_mlp",
    )(src, tile_x, tile_y, n_used, h1x, wg_b, wu_b, wd_b, wg_b, wu_b, wd_b)


def _combine_ln_kernel(pos_ref, y_hbm, h_ref, g_ref, b_ref, of_ref, ob_ref, buf, sem, *, alpha):
    i = pl.program_id(0)
    tm = h_ref.shape[0]

    def issue_tile(tile, slot):
        def issue(c, carry):
            base = pl.multiple_of(c * DMA_UNROLL, DMA_UNROLL)
            dst = buf.at[slot, pl.ds(base, DMA_UNROLL)]
            for j in range(DMA_UNROLL):
                row = pos_ref[tile * tm + base + j]
                pltpu.make_async_copy(y_hbm.at[pl.ds(row, 1)], dst.at[pl.ds(j, 1)],
                                      sem.at[slot]).start(priority=j % 2)
            return carry

        lax.fori_loop(0, tm // DMA_UNROLL, issue, 0)

    @pl.when(i == 0)
    def _():
        issue_tile(0, 0)

    @pl.when(i + 1 < pl.num_programs(0))
    def _():
        issue_tile(i + 1, (i + 1) % 2)

    slot = i % 2
    pltpu.make_async_copy(y_hbm.at[pl.ds(0, tm)], buf.at[slot], sem.at[slot]).wait()
    out = _layer_norm_rows(alpha * h_ref[...] + buf[slot], g_ref[...], b_ref[...])
    of_ref[...] = out
    ob_ref[...] = out.astype(BF16)


def _combine_ln(pos, y_sorted, h1x, ln_g, ln_b, alpha):
    t = h1x.shape[0]
    d = y_sorted.shape[1]
    tm = GATHER_TILE
    return pl.pallas_call(
        functools.partial(_combine_ln_kernel, alpha=alpha),
        out_shape=(jax.ShapeDtypeStruct((t, d), F32), jax.ShapeDtypeStruct((t, d), BF16)),
        grid_spec=pltpu.PrefetchScalarGridSpec(
            num_scalar_prefetch=1, grid=(t // tm,),
            in_specs=[pl.BlockSpec(memory_space=pl.ANY),
                      pl.BlockSpec((tm, d), lambda i, pos: (i, 0)),
                      pl.BlockSpec((1, d), lambda i, pos: (0, 0)),
                      pl.BlockSpec((1, d), lambda i, pos: (0, 0))],
            out_specs=(pl.BlockSpec((tm, d), lambda i, pos: (i, 0)),
                       pl.BlockSpec((tm, d), lambda i, pos: (i, 0))),
            scratch_shapes=[pltpu.VMEM((2, tm, d), F32), pltpu.SemaphoreType.DMA((2,))]),
        compiler_params=_cparams(("arbitrary",)),
        name="moe_combine_ln2",
    )(pos, y_sorted, h1x, ln_g.reshape(1, d), ln_b.reshape(1, d))


def _moe(h1x, rt, wg_b, wu_b, wd_b, layer, ln_g, ln_b, alpha):
    t = h1x.shape[0]
    src, pos, tile_x, tile_y, n_used = _dispatch_plan(rt, t)
    y_sorted = _expert_mlp(h1x, src, tile_x, tile_y, n_used, wg_b, wu_b, wd_b, layer)
    return _combine_ln(pos, y_sorted, h1x, ln_g, ln_b, alpha)


def kernel(x, positions, w_in, w_gate, b_gate, w_branch_a, w_branch_b, w_branch_c, w_out, sinks,
           ln1_g, ln1_b, w_router_group, w_router_expert, w_exp_gate, w_exp_up, w_exp_down,
           ln2_g, ln2_b):
    batch, seq, d = x.shape
    depth = w_in.shape[0]
    t = batch * seq
    alpha = (2 * depth) ** 0.25
    cc, ss = _rope_tables(positions)
    hf = x.reshape(t, d)
    hb = hf.astype(BF16)
    w_in_b, w_gate_b, w_out_b = w_in.astype(BF16), w_gate.astype(BF16), w_out.astype(BF16)
    wa_b, wb_b, wc_b = w_branch_a.astype(BF16), w_branch_b.astype(BF16), w_branch_c.astype(BF16)
    de = w_exp_gate.shape[-1]
    wg_b = w_exp_gate.astype(BF16).reshape(depth * N_EXPERTS, d, de)
    wu_b = w_exp_up.astype(BF16).reshape(depth * N_EXPERTS, d, de)
    wd_b = w_exp_down.astype(BF16).reshape(depth * N_EXPERTS, de, d)
    for l in range(depth):
        qkv = _qkv_proj(hb, w_in_b, l, cc, ss)
        out_a = _dilated_attention(qkv, batch, seq)
        out_b = _moba_attention(qkv, batch, seq)
        out_c = _swa_attention(qkv, sinks[l], batch, seq)
        merged = _merge(hb, out_a, out_b, out_c, w_gate_b, b_gate, wa_b, wb_b, wc_b, l)
        w_router = jnp.zeros((d, HEAD_DIM), F32)
        w_router = w_router.at[:, 0:N_GROUPS].set(w_router_group[l])
        w_router = w_router.at[:, 8:8 + N_EXPERTS].set(w_router_expert[l])
        h1f, rt = _outproj_ln_route(merged, w_out_b, l, hf, ln1_g[l], ln1_b[l], w_router, alpha)
        hf, hb = _moe(h1f, rt, wg_b, wu_b, wd_b, l, ln2_g[l], ln2_b[l], alpha)
    return hf.reshape(batch, seq, d)
```

```python
import functools

import jax
import jax.numpy as jnp
from jax import lax
from jax.experimental import pallas as pl
from jax.experimental.pallas import tpu as pltpu

F32 = jnp.float32
BF16 = jnp.bfloat16

HEAD_DIM = 128
ATTN_SCALE = HEAD_DIM ** -0.5
ROPE_THETA = 10000.0
A_PATTERNS = ((128, 1), (512, 4), (2048, 16))
A_HEADS_PER_GROUP = 2
A_HEADS = A_HEADS_PER_GROUP * len(A_PATTERNS)
B_HEADS = 6
MOBA_BLOCK = 256
MOBA_TOPK = 3
C_Q_HEADS = 8
C_KV_HEADS = 2
C_GROUP = C_Q_HEADS // C_KV_HEADS
C_WINDOW = 128
BAND = 128
N_Q_HEADS = A_HEADS + B_HEADS + C_Q_HEADS
N_KV_HEADS = A_HEADS + B_HEADS + C_KV_HEADS
N_GROUPS = 4
EXPERTS_PER_GROUP = 4
N_EXPERTS = N_GROUPS * EXPERTS_PER_GROUP
PAIRS_PER_GROUP = EXPERTS_PER_GROUP * (EXPERTS_PER_GROUP - 1) // 2
N_CLASSES = N_GROUPS * PAIRS_PER_GROUP
PAIR_X = (0, 0, 0, 1, 1, 3)
PAIR_Y = (1, 2, 3, 3, 2, 2)
LN_EPS = 1e-5
NEG = -1e30

V7X_VMEM_BYTES = 64 * 1024 * 1024
VMEM_LIMIT = V7X_VMEM_BYTES - 8 * 1024 * 1024


def _cparams(sem, vmem=VMEM_LIMIT):
    return pltpu.CompilerParams(dimension_semantics=sem, vmem_limit_bytes=vmem)


ROPE_SLOT_Q, ROPE_SLOT_K, ROPE_SLOT_V = 0, 1, 2


def _rope_table_kernel(pos_ref, inv_ref, cc_ref, ss_ref):
    ang = pos_ref[...].astype(F32) * inv_ref[...]
    lane = lax.broadcasted_iota(jnp.int32, ang.shape, 1)
    s = jnp.sin(ang)
    c = jnp.cos(ang)
    s = jnp.where(lane < HEAD_DIM // 2, -s, s)
    cc_ref[ROPE_SLOT_Q] = c * ATTN_SCALE
    ss_ref[ROPE_SLOT_Q] = s * ATTN_SCALE
    cc_ref[ROPE_SLOT_K] = c
    ss_ref[ROPE_SLOT_K] = s
    cc_ref[ROPE_SLOT_V] = jnp.ones_like(c)
    ss_ref[ROPE_SLOT_V] = jnp.zeros_like(s)


def _rope_tables(positions):
    t = positions.size
    tm = 2048
    inv = ROPE_THETA ** (-jnp.arange(0, HEAD_DIM, 2, dtype=F32) / HEAD_DIM)
    inv2 = jnp.concatenate([inv, inv]).reshape(1, HEAD_DIM)
    pos = positions.reshape(t, 1)
    return pl.pallas_call(
        _rope_table_kernel,
        out_shape=(jax.ShapeDtypeStruct((3, t, HEAD_DIM), F32),) * 2,
        grid=(t // tm,),
        in_specs=[pl.BlockSpec((tm, 1), lambda i: (i, 0)),
                  pl.BlockSpec((1, HEAD_DIM), lambda i: (0, 0))],
        out_specs=(pl.BlockSpec((3, tm, HEAD_DIM), lambda i: (0, i, 0)),) * 2,
        compiler_params=_cparams(("arbitrary",)),
        name="rope_tables",
    )(pos, inv2)


def _qkv_kernel(x_ref, w_ref, cc_ref, ss_ref, o_ref, *, heads_per_tile):
    j = pl.program_id(1)
    acc = jnp.dot(x_ref[...], w_ref[...], preferred_element_type=F32)
    for hh in range(heads_per_tile):
        head = j * heads_per_tile + hh
        slot = (head >= N_Q_HEADS).astype(jnp.int32) + (head >= N_Q_HEADS + N_KV_HEADS).astype(jnp.int32)
        t = acc[:, hh * HEAD_DIM:(hh + 1) * HEAD_DIM]
        r = t * cc_ref[slot] + pltpu.roll(t, HEAD_DIM // 2, 1) * ss_ref[slot]
        o_ref[hh] = r.astype(BF16)


def _qkv_proj(hb, w_in_b, layer, cc, ss):
    t, d = hb.shape
    n = w_in_b.shape[2]
    tm, tn = 1024, 1536
    hpt = tn // HEAD_DIM
    return pl.pallas_call(
        functools.partial(_qkv_kernel, heads_per_tile=hpt),
        out_shape=jax.ShapeDtypeStruct((n // HEAD_DIM, t, HEAD_DIM), BF16),
        grid=(t // tm, n // tn),
        in_specs=[pl.BlockSpec((tm, d), lambda i, j: (i, 0)),
                  pl.BlockSpec((None, d, tn), lambda i, j: (layer, 0, j)),
                  pl.BlockSpec((3, tm, HEAD_DIM), lambda i, j: (0, i, 0)),
                  pl.BlockSpec((3, tm, HEAD_DIM), lambda i, j: (0, i, 0))],
        out_specs=pl.BlockSpec((hpt, tm, HEAD_DIM), lambda i, j: (j, i, 0)),
        compiler_params=_cparams(("arbitrary", "arbitrary")),
        name="qkv_rope",
    )(hb, w_in_b, cc, ss)


def _band_bias(max_dist):
    qi = lax.broadcasted_iota(jnp.int32, (BAND, 2 * BAND), 0)
    kj = lax.broadcasted_iota(jnp.int32, (BAND, 2 * BAND), 1)
    dist = qi + BAND - kj
    valid = (dist >= 0) & (dist <= max_dist)
    bias = jnp.where(valid, 0.0, NEG).astype(F32)
    bias_first = jnp.where(valid & (kj >= BAND), 0.0, NEG).astype(F32)
    return bias, bias_first


def _band_block(qb, kwin, vwin, bias):
    g = qb.shape[0] // BAND
    s = lax.dot_general(qb, kwin, (((1,), (1,)), ((), ())), preferred_element_type=F32)
    s = s.reshape(g, BAND, 2 * BAND) + bias[None]
    m = jnp.max(s, axis=-1, keepdims=True)
    p = jnp.exp(s - m)
    l = jnp.sum(p, axis=-1, keepdims=True)
    o = jnp.dot(p.reshape(g * BAND, 2 * BAND).astype(BF16), vwin, preferred_element_type=F32)
    o = o.reshape(g, BAND, HEAD_DIM) / l
    return o, m + jnp.log(l)


def _swa_kernel(sink_ref, q_ref, k_ref, kh_ref, v_ref, vh_ref, o_ref, *, tq):
    kv = pl.program_id(1)
    t = pl.program_id(2)
    bias, bias_first = _band_bias(C_WINDOW - 1)
    for blk in range(tq // BAND):
        qb = q_ref[:, blk * BAND:(blk + 1) * BAND, :].reshape(C_GROUP * BAND, HEAD_DIM)
        if blk == 0:
            kwin = jnp.concatenate([kh_ref[0], k_ref[0, :BAND]], axis=0)
            vwin = jnp.concatenate([vh_ref[0], v_ref[0, :BAND]], axis=0)
            b = jnp.where(t == 0, bias_first, bias)
        else:
            kwin = k_ref[0, (blk - 1) * BAND:(blk + 1) * BAND]
            vwin = v_ref[0, (blk - 1) * BAND:(blk + 1) * BAND]
            b = bias
        o, lse = _band_block(qb, kwin, vwin, b)
        for g in range(C_GROUP):
            keep = jax.nn.sigmoid(lse[g] - sink_ref[kv * C_GROUP + g])
            o_ref[blk * BAND:(blk + 1) * BAND, g * HEAD_DIM:(g + 1) * HEAD_DIM] = (o[g] * keep).astype(BF16)


def _swa_attention(qkv, sinks, batch, seq):
    tq = 1024
    nt = seq // tq
    q0 = A_HEADS + B_HEADS
    k0 = N_Q_HEADS + A_HEADS + B_HEADS
    v0 = N_Q_HEADS + N_KV_HEADS + A_HEADS + B_HEADS
    rb = tq // BAND

    def halo(b, kv, t):
        return jnp.maximum(b * (seq // BAND) + t * rb - 1, 0)

    return pl.pallas_call(
        functools.partial(_swa_kernel, tq=tq),
        out_shape=jax.ShapeDtypeStruct((batch * seq, C_Q_HEADS * HEAD_DIM), BF16),
        grid=(batch, C_KV_HEADS, nt),
        in_specs=[
            pl.BlockSpec(memory_space=pltpu.SMEM),
            pl.BlockSpec((C_GROUP, tq, HEAD_DIM), lambda b, kv, t: (q0 // C_GROUP + kv, b * nt + t, 0)),
            pl.BlockSpec((1, tq, HEAD_DIM), lambda b, kv, t: (k0 + kv, b * nt + t, 0)),
            pl.BlockSpec((1, BAND, HEAD_DIM), lambda b, kv, t: (k0 + kv, halo(b, kv, t), 0)),
            pl.BlockSpec((1, tq, HEAD_DIM), lambda b, kv, t: (v0 + kv, b * nt + t, 0)),
            pl.BlockSpec((1, BAND, HEAD_DIM), lambda b, kv, t: (v0 + kv, halo(b, kv, t), 0)),
        ],
        out_specs=pl.BlockSpec((tq, C_GROUP * HEAD_DIM), lambda b, kv, t: (b * nt + t, kv)),
        compiler_params=_cparams(("arbitrary",) * 3),
        name="swa_sink",
    )(sinks, qkv, qkv, qkv, qkv, qkv)


def _dilated_kernel(*refs, seq):
    ng = len(A_PATTERNS)
    in_refs = refs[:3 * ng]
    o_ref = refs[3 * ng]
    qf, kf, vf, og, lg = refs[3 * ng + 1:]
    for g, (window, dil) in enumerate(A_PATTERNS):
        q_ref, k_ref, v_ref = in_refs[3 * g:3 * g + 3]
        qf[...] = q_ref[0].astype(F32)
        kf[...] = k_ref[0].astype(F32)
        vf[...] = v_ref[0].astype(F32)
        nblk = seq // dil // BAND
        bias, bias_first = _band_bias(window // dil)

        for r in range(dil):
            for n in range(nblk):
                own = pl.ds(r + dil * BAND * n, BAND, stride=dil)
                prev = pl.ds(r + dil * BAND * max(n - 1, 0), BAND, stride=dil)
                qb = qf[own, :].astype(BF16)
                kwin = jnp.concatenate([kf[prev, :], kf[own, :]], axis=0).astype(BF16)
                vwin = jnp.concatenate([vf[prev, :], vf[own, :]], axis=0).astype(BF16)
                o, lse = _band_block(qb, kwin, vwin, bias_first if n == 0 else bias)
                og[g, own, :] = o[0]
                lg[g, own, :] = jnp.broadcast_to(lse[0], (BAND, HEAD_DIM))

    rows = 256

    def combine(c, carry):
        sl = pl.ds(pl.multiple_of(c * rows, rows), rows)
        ls = [lg[g, sl, :] for g in range(ng)]
        m = functools.reduce(jnp.maximum, ls)
        es = [jnp.exp(l - m) for l in ls]
        num = functools.reduce(jnp.add, [e * og[g, sl, :] for g, e in enumerate(es)])
        o_ref[sl, :] = (num / functools.reduce(jnp.add, es)).astype(BF16)
        return carry

    lax.fori_loop(0, seq // rows, combine, 0)


def _dilated_attention(qkv, batch, seq):
    ng = len(A_PATTERNS)
    k0 = N_Q_HEADS
    v0 = N_Q_HEADS + N_KV_HEADS
    in_specs = []
    for g in range(ng):
        for base in (0, k0, v0):
            in_specs.append(pl.BlockSpec(
                (1, seq, HEAD_DIM),
                lambda b, j, h=base + g * A_HEADS_PER_GROUP: (h + j, b, 0)))
    return pl.pallas_call(
        functools.partial(_dilated_kernel, seq=seq),
        out_shape=jax.ShapeDtypeStruct((batch * seq, A_HEADS_PER_GROUP * HEAD_DIM), BF16),
        grid=(batch, A_HEADS_PER_GROUP),
        in_specs=in_specs,
        out_specs=pl.BlockSpec((seq, HEAD_DIM), lambda b, j: (b, j)),
        scratch_shapes=[pltpu.VMEM((seq, HEAD_DIM), F32)] * 3
        + [pltpu.VMEM((ng, seq, HEAD_DIM), F32)] * 2,
        compiler_params=_cparams(("arbitrary",) * 2),
        name="dilated_attn",
    )(*([qkv] * (3 * ng)))


MOBA_TQ = 1024


def _online_update(state, s, v):
    mc = jnp.max(s, axis=1, keepdims=True)
    if state is None:
        p = jnp.exp(s - mc)
        return mc, jnp.sum(p, axis=1, keepdims=True), jnp.dot(p.astype(BF16), v, preferred_element_type=F32)
    m, l, acc = state
    m_new = jnp.maximum(m, mc)
    alpha = jnp.exp(m - m_new)
    p = jnp.exp(s - m_new)
    l_new = alpha * l + jnp.sum(p, axis=1, keepdims=True)
    acc_new = alpha * acc + jnp.dot(p.astype(BF16), v, preferred_element_type=F32)
    return m_new, l_new, acc_new


def _moba_kernel(q_ref, k_ref, v_ref, o_ref, kaug, kmean, *, seq):
    t = pl.program_id(2)
    nblk = seq // MOBA_BLOCK
    blk = MOBA_BLOCK
    tq = MOBA_TQ
    nb = tq // blk

    @pl.when(t == 0)
    def _():
        kaug[:, :HEAD_DIM] = k_ref[0]
        row = lax.broadcasted_iota(jnp.int32, (seq, HEAD_DIM), 0)
        lane = lax.broadcasted_iota(jnp.int32, (seq, HEAD_DIM), 1)
        kaug[:, HEAD_DIM:] = jnp.where(row // blk == lane, 1.0, 0.0).astype(BF16)
        kmean[...] = jnp.zeros_like(kmean)
        for n in range(nblk):
            kb = k_ref[0, n * blk:(n + 1) * blk, :].astype(F32)
            kmean[n:n + 1, :] = jnp.mean(kb, axis=0, keepdims=True)

    q = q_ref[0]
    km = kmean[...]
    km_hi = km.astype(BF16)
    km_lo = (km - km_hi.astype(F32)).astype(BF16)
    nt = (((1,), (1,)), ((), ()))
    gate = (lax.dot_general(km_hi, q, nt, preferred_element_type=F32)
            + lax.dot_general(km_lo, q, nt, preferred_element_type=F32))
    nrow = -(-nblk // 8) * 8
    row = lax.broadcasted_iota(jnp.int32, (nrow, tq), 0).astype(F32)
    own = (t * nb + lax.broadcasted_iota(jnp.int32, (nrow, tq), 1) // blk).astype(F32)
    g = jnp.where(row < own, gate[:nrow], NEG)
    sel = jnp.zeros((nrow, tq), jnp.bool_)
    for _ in range(MOBA_TOPK):
        mx = jnp.max(g, axis=0, keepdims=True)
        first = jnp.min(jnp.where(g == mx, row, float(nrow)), axis=0, keepdims=True)
        pick = row == first
        sel = sel | pick
        g = jnp.where(pick, NEG, g)
    visible = (sel & (row < own)) | (row == own)
    bias_t = jnp.concatenate([jnp.where(visible, 0.0, NEG), jnp.zeros((HEAD_DIM - nrow, tq), F32)], axis=0)
    qaug = jnp.concatenate([q, bias_t.T.astype(BF16)], axis=1)

    ri = lax.broadcasted_iota(jnp.int32, (blk, blk), 0)
    ci = lax.broadcasted_iota(jnp.int32, (blk, blk), 1)
    causal = ri >= ci

    for c in range(seq // tq):
        @pl.when(t == c)
        def _(c=c):
            state = None
            for cc in range(c):
                keys = slice(cc * tq, (cc + 1) * tq)
                s = lax.dot_general(qaug, kaug[keys, :], nt, preferred_element_type=F32)
                state = _online_update(state, s, v_ref[0, keys, :])
            n_split = 2 if c == 0 else 1
            hb = nb // n_split
            parts = []
            for h in range(n_split):
                rows = slice(h * hb * blk, (h + 1) * hb * blk)
                keys = slice(c * tq, c * tq + (h + 1) * hb * blk)
                s = lax.dot_general(qaug[rows], kaug[keys, :], nt, preferred_element_type=F32)
                s = jnp.concatenate(
                    [jnp.concatenate(
                        [jnp.where(causal, s[a * blk:(a + 1) * blk, e * blk:(e + 1) * blk], NEG)
                         if e == a + h * hb else s[a * blk:(a + 1) * blk, e * blk:(e + 1) * blk]
                         for e in range((h + 1) * hb)], axis=1) for a in range(hb)], axis=0)
                st = None if state is None else tuple(x[rows] for x in state)
                parts.append(_online_update(st, s, v_ref[0, keys, :]))
            m, l, acc = (jnp.concatenate([p[i] for p in parts], axis=0) for i in range(3))
            o_ref[...] = (acc / l).astype(BF16)


def _moba_attention(qkv, batch, seq):
    nt = seq // MOBA_TQ
    q0 = A_HEADS
    k0 = N_Q_HEADS + A_HEADS
    v0 = N_Q_HEADS + N_KV_HEADS + A_HEADS
    return pl.pallas_call(
        functools.partial(_moba_kernel, seq=seq),
        out_shape=jax.ShapeDtypeStruct((batch * seq, B_HEADS * HEAD_DIM), BF16),
        grid=(batch, B_HEADS, nt),
        in_specs=[
            pl.BlockSpec((1, MOBA_TQ, HEAD_DIM), lambda b, h, t: (q0 + h, b * nt + t, 0)),
            pl.BlockSpec((1, seq, HEAD_DIM), lambda b, h, t: (k0 + h, b, 0)),
            pl.BlockSpec((1, seq, HEAD_DIM), lambda b, h, t: (v0 + h, b, 0)),
        ],
        out_specs=pl.BlockSpec((MOBA_TQ, HEAD_DIM), lambda b, h, t: (b * nt + t, h)),
        scratch_shapes=[pltpu.VMEM((seq, 2 * HEAD_DIM), BF16), pltpu.VMEM((HEAD_DIM, HEAD_DIM), F32)],
        compiler_params=_cparams(("arbitrary",) * 3),
        name="moba_attn",
    )(qkv, qkv, qkv)


def _merge_kernel(h_ref, oa_ref, ob_ref, oc_ref, g0_ref, g1_ref, g2_ref, b0_ref, b1_ref, b2_ref,
                  wa_ref, wb_ref, wc_ref, o_ref):
    x = h_ref[...]
    acc = None
    for g_ref, b_ref, br_ref, w_ref in ((g0_ref, b0_ref, oa_ref, wa_ref),
                                        (g1_ref, b1_ref, ob_ref, wb_ref),
                                        (g2_ref, b2_ref, oc_ref, wc_ref)):
        gate = jax.nn.sigmoid(jnp.dot(x, g_ref[...], preferred_element_type=F32) + b_ref[...])
        term = gate * jnp.dot(br_ref[...], w_ref[...], preferred_element_type=F32)
        acc = term if acc is None else acc + term
    o_ref[...] = acc.astype(BF16)


def _merge(hb, out_a, out_b, out_c, w_gate_b, b_gate, wa_b, wb_b, wc_b, layer):
    t, d = hb.shape
    tm, tn = 1024, 512
    nj = d // tn
    row = lambda w: pl.BlockSpec((tm, w), lambda j, i: (i, 0))
    gate_w = lambda x: pl.BlockSpec((None, d, tn), lambda j, i, x=x: (layer, 0, x * nj + j))
    gate_b = lambda x: pl.BlockSpec((None, 1, tn), lambda j, i, x=x: (layer, 0, x * nj + j))
    br_w = lambda k: pl.BlockSpec((None, k, tn), lambda j, i: (layer, 0, j))
    bg = b_gate.reshape(b_gate.shape[0], 1, -1)
    return pl.pallas_call(
        _merge_kernel,
        out_shape=jax.ShapeDtypeStruct((t, d), BF16),
        grid=(nj, t // tm),
        in_specs=[row(d), row(out_a.shape[1]), row(out_b.shape[1]), row(out_c.shape[1]),
                  gate_w(0), gate_w(1), gate_w(2), gate_b(0), gate_b(1), gate_b(2),
                  br_w(wa_b.shape[1]), br_w(wb_b.shape[1]), br_w(wc_b.shape[1])],
        out_specs=pl.BlockSpec((tm, tn), lambda j, i: (i, j)),
        compiler_params=_cparams(("arbitrary",) * 2),
        name="branch_merge",
    )(hb, out_a, out_b, out_c, w_gate_b, w_gate_b, w_gate_b, bg, bg, bg, wa_b, wb_b, wc_b)


def _layer_norm_rows(y, g, b):
    mu = jnp.mean(y, axis=-1, keepdims=True)
    yc = y - mu
    var = jnp.mean(yc * yc, axis=-1, keepdims=True)
    return yc * lax.rsqrt(var + LN_EPS) * g + b


def _route(lt):
    tm = lt.shape[1]
    gl = lt[0:N_GROUPS]
    gmax = jnp.max(gl, axis=0, keepdims=True)
    ge = jnp.exp(gl - gmax)
    gp = ge / jnp.sum(ge, axis=0, keepdims=True)
    g_val = jnp.max(gp, axis=0, keepdims=True)
    row = lax.broadcasted_iota(jnp.int32, (N_GROUPS, tm), 0)
    g_sel = jnp.min(jnp.where(gp == g_val, row, N_GROUPS), axis=0, keepdims=True)
    el = jnp.zeros((EXPERTS_PER_GROUP, tm), F32)
    for g in range(N_GROUPS):
        lo = 8 + g * EXPERTS_PER_GROUP
        el = jnp.where(g_sel == g, lt[lo:lo + EXPERTS_PER_GROUP], el)
    v0 = jnp.max(el, axis=0, keepdims=True)
    i0 = jnp.min(jnp.where(el == v0, row, EXPERTS_PER_GROUP), axis=0, keepdims=True)
    el1 = jnp.where(row == i0, -jnp.inf, el)
    v1 = jnp.max(el1, axis=0, keepdims=True)
    i1 = jnp.min(jnp.where(el1 == v1, row, EXPERTS_PER_GROUP), axis=0, keepdims=True)
    e1 = jnp.exp(v1 - v0)
    den = 1.0 + e1
    w0 = 1.0 / den * g_val
    w1 = e1 / den * g_val
    a = jnp.minimum(i0, i1)
    b = jnp.maximum(i0, i1)
    swap = (a == 2) & (b == 3)
    x = jnp.where(swap, b, a)
    y = jnp.where(swap, a, b)
    rank = jnp.where(a == 0, b - 1, jnp.where(a == 1, jnp.where(b == 3, 3, 4), 5))
    cls = (g_sel * PAIRS_PER_GROUP + rank).astype(F32)
    wx = jnp.where(i0 == x, w0, w1)
    wy = jnp.where(i0 == y, w0, w1)
    return cls, wx, wy


def _outproj_ln_route_kernel(m_ref, w_ref, h_ref, g_ref, b_ref, wr_ref, hf_ref, rt_ref, mix_buf,
                             wrh_ref, wrl_ref, *, alpha):
    @pl.when(pl.program_id(0) == 0)
    def _():
        mix_buf[...] = jnp.zeros_like(mix_buf)
        wr = wr_ref[...]
        wr_hi = wr.astype(BF16)
        wrh_ref[...] = wr_hi
        wrl_ref[...] = (wr - wr_hi.astype(F32)).astype(BF16)

    prev = mix_buf[...]
    mix_buf[...] = jnp.dot(m_ref[...], w_ref[...], preferred_element_type=F32)
    h1 = _layer_norm_rows(alpha * h_ref[...] + prev, g_ref[...], b_ref[...])
    h_hi = h1.astype(BF16)
    h_lo = (h1 - h_hi.astype(F32)).astype(BF16)
    logits = (jnp.dot(h_hi, wrh_ref[...], preferred_element_type=F32)
              + jnp.dot(h_lo, wrh_ref[...], preferred_element_type=F32)
              + jnp.dot(h_hi, wrl_ref[...], preferred_element_type=F32))
    cls, wx, wy = _route(logits.T)
    tm = logits.shape[0]
    rt_ref[...] = jnp.concatenate([cls, wx, wy, jnp.zeros((5, tm), F32)], axis=0)
    ext = jnp.concatenate([wx, wy, jnp.zeros((HEAD_DIM - 2, tm), F32)], axis=0).T
    d = h1.shape[1]
    hf_ref[:, :d] = h1
    hf_ref[:, d:] = ext


def _outproj_ln_route(merged, w_out_b, layer, h, ln_g, ln_b, w_router, alpha):
    t, d = h.shape
    tm = 512
    n = t // tm
    cur = lambda i: (jnp.minimum(i, n - 1), 0)
    prv = lambda i: (jnp.maximum(i - 1, 0), 0)
    return pl.pallas_call(
        functools.partial(_outproj_ln_route_kernel, alpha=alpha),
        out_shape=(jax.ShapeDtypeStruct((t, d + HEAD_DIM), F32), jax.ShapeDtypeStruct((8, t), F32)),
        grid=(n + 1,),
        in_specs=[pl.BlockSpec((tm, d), cur),
                  pl.BlockSpec((None, d, d), lambda i: (layer, 0, 0)),
                  pl.BlockSpec((tm, d), prv),
                  pl.BlockSpec((1, d), lambda i: (0, 0)),
                  pl.BlockSpec((1, d), lambda i: (0, 0)),
                  pl.BlockSpec((d, HEAD_DIM), lambda i: (0, 0))],
        out_specs=(pl.BlockSpec((tm, d + HEAD_DIM), prv),
                   pl.BlockSpec((8, tm), lambda i: (0, jnp.maximum(i - 1, 0)))),
        scratch_shapes=[pltpu.VMEM((tm, d), F32),
                        pltpu.VMEM((d, HEAD_DIM), BF16), pltpu.VMEM((d, HEAD_DIM), BF16)],
        compiler_params=_cparams(("arbitrary",)),
        name="outproj_ln1_route",
    )(merged, w_out_b, h, ln_g.reshape(1, d), ln_b.reshape(1, d), w_router)


MOE_TILE = 256
GATHER_TILE = 512
DMA_UNROLL = 8
MLP_CHUNKS = 4


def _dispatch_plan(rt, t):
    cls = rt[0].astype(jnp.int32)
    onehot = (cls[:, None] == jnp.arange(N_CLASSES, dtype=jnp.int32)[None, :]).astype(jnp.int32)
    csum = jnp.cumsum(onehot, axis=0)
    counts = csum[-1]
    rank = jnp.take_along_axis(csum, cls[:, None], axis=1)[:, 0] - 1
    padded = (counts + MOE_TILE - 1) // MOE_TILE * MOE_TILE
    ends = jnp.cumsum(padded)
    pos = ((ends - padded)[cls] + rank).astype(jnp.int32)
    p_rows = t + N_CLASSES * MOE_TILE
    tok = jnp.arange(t, dtype=jnp.int32)
    src = jnp.zeros((p_rows + MOE_TILE,), jnp.int32).at[pos].set(tok)
    n_tiles = p_rows // MOE_TILE
    n_used = (ends[-1] // MOE_TILE).astype(jnp.int32)
    tile_idx = jnp.arange(n_tiles, dtype=jnp.int32)
    tile_start = jnp.minimum(tile_idx, n_used - 1) * MOE_TILE
    tile_c = jnp.sum((ends[None, :] <= tile_start[:, None]).astype(jnp.int32), axis=1)
    tile_c = jnp.minimum(tile_c, N_CLASSES - 1)
    group = tile_c // PAIRS_PER_GROUP
    pair = tile_c % PAIRS_PER_GROUP
    tile_x = group * EXPERTS_PER_GROUP + jnp.asarray(PAIR_X, jnp.int32)[pair]
    tile_y = group * EXPERTS_PER_GROUP + jnp.asarray(PAIR_Y, jnp.int32)[pair]
    return src, pos, tile_x.astype(jnp.int32), tile_y.astype(jnp.int32), n_used.reshape(1)


def _expert_mlp_kernel(src_ref, tx_ref, ty_ref, nu_ref, h_hbm, wgx_ref, wux_ref, wdx_ref,
                       wgy_ref, wuy_ref, wdy_ref, y_ref, xbuf, sem):
    i = pl.program_id(0)
    n_used = nu_ref[0]
    slot = i % 2
    d = y_ref.shape[1]

    def issue_rows(tile, dst_slot, lo, hi):
        for j in range(lo, hi):
            row = src_ref[tile * MOE_TILE + j]
            pltpu.make_async_copy(h_hbm.at[pl.ds(row, 1)], xbuf.at[dst_slot, pl.ds(j, 1)],
                                  sem.at[dst_slot]).start(priority=j % 2)

    def wait_tile(s):
        pltpu.make_async_copy(h_hbm.at[pl.ds(0, MOE_TILE)], xbuf.at[s], sem.at[s]).wait()

    @pl.when(i == 0)
    def _():
        def first(c, carry):
            base = pl.multiple_of(c * DMA_UNROLL, DMA_UNROLL)
            for j in range(DMA_UNROLL):
                row = src_ref[base + j]
                pltpu.make_async_copy(h_hbm.at[pl.ds(row, 1)], xbuf.at[0, pl.ds(base + j, 1)],
                                      sem.at[0]).start()
            return carry
        lax.fori_loop(0, MOE_TILE // DMA_UNROLL, first, 0)

    @pl.when(i < n_used)
    def _():
        wait_tile(slot)
        nxt = 1 - slot
        de = wgx_ref.shape[2]
        cw = de // MLP_CHUNKS
        n_pieces = 4 * MLP_CHUNKS
        bounds = [MOE_TILE * k // n_pieces for k in range(n_pieces + 1)]
        piece = iter(range(n_pieces))

        def issue_next():
            k = next(piece)
            issue_rows(i + 1, nxt, bounds[k], bounds[k + 1])

        out = None
        for wg_ref, wu_ref, wd_ref, lane in ((wgx_ref, wux_ref, wdx_ref, d), (wgy_ref, wuy_ref, wdy_ref, d + 1)):
            hid = []
            for c in range(MLP_CHUNKS):
                cols = slice(c * cw, (c + 1) * cw)
                issue_next()
                a = jnp.dot(xbuf[slot, :, :d].astype(BF16), wg_ref[0, :, cols], preferred_element_type=F32)
                issue_next()
                u = jnp.dot(xbuf[slot, :, :d].astype(BF16), wu_ref[0, :, cols], preferred_element_type=F32)
                hid.append((jax.nn.silu(a) * u).astype(BF16))
            y = None
            for c in range(MLP_CHUNKS):
                part = jnp.dot(hid[c], wd_ref[0, c * cw:(c + 1) * cw, :], preferred_element_type=F32)
                y = part if y is None else y + part
            y = y * xbuf[slot, :, lane:lane + 1]
            out = y if out is None else out + y
        y_ref[...] = out

    @pl.when(i >= n_used)
    def _():
        y_ref[...] = jnp.zeros_like(y_ref)

    @pl.when(i == n_used)
    def _():
        wait_tile(slot)


def _expert_mlp(h1x, src, tile_x, tile_y, n_used, wg_b, wu_b, wd_b, layer):
    dx = h1x.shape[1]
    d = dx - HEAD_DIM
    p = src.shape[0] - MOE_TILE
    de = wg_b.shape[2]
    ex = lambda i, src, tx, ty, nu: (layer * N_EXPERTS + tx[i], 0, 0)
    ey = lambda i, src, tx, ty, nu: (layer * N_EXPERTS + ty[i], 0, 0)
    once = pl.Buffered(1)
    return pl.pallas_call(
        _expert_mlp_kernel,
        out_shape=jax.ShapeDtypeStruct((p, d), F32),
        grid_spec=pltpu.PrefetchScalarGridSpec(
            num_scalar_prefetch=4, grid=(p // MOE_TILE,),
            in_specs=[pl.BlockSpec(memory_space=pl.ANY),
                      pl.BlockSpec((1, d, de), ex, pipeline_mode=once),
                      pl.BlockSpec((1, d, de), ex, pipeline_mode=once),
                      pl.BlockSpec((1, de, d), ex, pipeline_mode=once),
                      pl.BlockSpec((1, d, de), ey),
                      pl.BlockSpec((1, d, de), ey),
                      pl.BlockSpec((1, de, d), ey)],
            out_specs=pl.BlockSpec((MOE_TILE, d), lambda i, src, tx, ty, nu: (i, 0)),
            scratch_shapes=[pltpu.VMEM((2, MOE_TILE, dx), F32), pltpu.SemaphoreType.DMA((2,))]),
        compiler_params=_cparams(("arbitrary",)),
        name="moe_expert_mlp",
    )(src, tile_x, tile_y, n_used, h1x, wg_b, wu_b, wd_b, wg_b, wu_b, wd_b)


def _combine_ln_kernel(pos_ref, y_hbm, h_ref, g_ref, b_ref, *refs, alpha):
    out_refs, (buf, sem) = refs[:-2], refs[-2:]
    i = pl.program_id(0)
    tm = h_ref.shape[0]

    def issue_tile(tile, slot):
        def issue(c, carry):
            base = pl.multiple_of(c * DMA_UNROLL, DMA_UNROLL)
            dst = buf.at[slot, pl.ds(base, DMA_UNROLL)]
            for j in range(DMA_UNROLL):
                row = pos_ref[tile * tm + base + j]
                pltpu.make_async_copy(y_hbm.at[pl.ds(row, 1)], dst.at[pl.ds(j, 1)],
                                      sem.at[slot]).start(priority=j % 2)
            return carry

        lax.fori_loop(0, tm // DMA_UNROLL, issue, 0)

    @pl.when(i == 0)
    def _():
        issue_tile(0, 0)

    @pl.when(i + 1 < pl.num_programs(0))
    def _():
        issue_tile(i + 1, (i + 1) % 2)

    slot = i % 2
    pltpu.make_async_copy(y_hbm.at[pl.ds(0, tm)], buf.at[slot], sem.at[slot]).wait()
    out = _layer_norm_rows(alpha * h_ref[...] + buf[slot], g_ref[...], b_ref[...])
    for o_ref in out_refs:
        o_ref[...] = out.astype(o_ref.dtype)


def _combine_ln(pos, y_sorted, h1x, ln_g, ln_b, alpha, out_dtypes):
    t = h1x.shape[0]
    d = y_sorted.shape[1]
    tm = GATHER_TILE
    return pl.pallas_call(
        functools.partial(_combine_ln_kernel, alpha=alpha),
        out_shape=tuple(jax.ShapeDtypeStruct((t, d), dt) for dt in out_dtypes),
        grid_spec=pltpu.PrefetchScalarGridSpec(
            num_scalar_prefetch=1, grid=(t // tm,),
            in_specs=[pl.BlockSpec(memory_space=pl.ANY),
                      pl.BlockSpec((tm, d), lambda i, pos: (i, 0)),
                      pl.BlockSpec((1, d), lambda i, pos: (0, 0)),
                      pl.BlockSpec((1, d), lambda i, pos: (0, 0))],
            out_specs=tuple(pl.BlockSpec((tm, d), lambda i, pos: (i, 0)) for _ in out_dtypes),
            scratch_shapes=[pltpu.VMEM((2, tm, d), F32), pltpu.SemaphoreType.DMA((2,))]),
        compiler_params=_cparams(("arbitrary",)),
        name="moe_combine_ln2",
    )(pos, y_sorted, h1x, ln_g.reshape(1, d), ln_b.reshape(1, d))


def _moe(h1x, rt, wg_b, wu_b, wd_b, layer, ln_g, ln_b, alpha, out_dtypes):
    t = h1x.shape[0]
    src, pos, tile_x, tile_y, n_used = _dispatch_plan(rt, t)
    y_sorted = _expert_mlp(h1x, src, tile_x, tile_y, n_used, wg_b, wu_b, wd_b, layer)
    return _combine_ln(pos, y_sorted, h1x, ln_g, ln_b, alpha, out_dtypes)


def kernel(x, positions, w_in, w_gate, b_gate, w_branch_a, w_branch_b, w_branch_c, w_out, sinks,
           ln1_g, ln1_b, w_router_group, w_router_expert, w_exp_gate, w_exp_up, w_exp_down,
           ln2_g, ln2_b):
    batch, seq, d = x.shape
    depth = w_in.shape[0]
    t = batch * seq
    alpha = (2 * depth) ** 0.25
    cc, ss = _rope_tables(positions)
    hf = x.reshape(t, d)
    hb = hf.astype(BF16)
    w_in_b, w_gate_b, w_out_b = w_in.astype(BF16), w_gate.astype(BF16), w_out.astype(BF16)
    wa_b, wb_b, wc_b = w_branch_a.astype(BF16), w_branch_b.astype(BF16), w_branch_c.astype(BF16)
    de = w_exp_gate.shape[-1]
    wg_b = w_exp_gate.astype(BF16).reshape(depth * N_EXPERTS, d, de)
    wu_b = w_exp_up.astype(BF16).reshape(depth * N_EXPERTS, d, de)
    wd_b = w_exp_down.astype(BF16).reshape(depth * N_EXPERTS, de, d)
    for l in range(depth):
        qkv = _qkv_proj(hb, w_in_b, l, cc, ss)
        out_a = _dilated_attention(qkv, batch, seq)
        out_b = _moba_attention(qkv, batch, seq)
        out_c = _swa_attention(qkv, sinks[l], batch, seq)
        merged = _merge(hb, out_a, out_b, out_c, w_gate_b, b_gate, wa_b, wb_b, wc_b, l)
        w_router = jnp.zeros((d, HEAD_DIM), F32)
        w_router = w_router.at[:, 0:N_GROUPS].set(w_router_group[l])
        w_router = w_router.at[:, 8:8 + N_EXPERTS].set(w_router_expert[l])
        h1f, rt = _outproj_ln_route(merged, w_out_b, l, hf, ln1_g[l], ln1_b[l], w_router, alpha)
        last = l == depth - 1
        outs = _moe(h1f, rt, wg_b, wu_b, wd_b, l, ln2_g[l], ln2_b[l], alpha,
                    (F32,) if last else (F32, BF16))
        hf, hb = (outs[0], None) if last else outs
    return hf.reshape(batch, seq, d)
```

```python
import functools

import jax
import jax.numpy as jnp
from jax import lax
from jax.experimental import pallas as pl
from jax.experimental.pallas import tpu as pltpu

F32 = jnp.float32
BF16 = jnp.bfloat16

HEAD_DIM = 128
ATTN_SCALE = HEAD_DIM ** -0.5
ROPE_THETA = 10000.0
A_PATTERNS = ((128, 1), (512, 4), (2048, 16))
A_HEADS_PER_GROUP = 2
A_HEADS = A_HEADS_PER_GROUP * len(A_PATTERNS)
B_HEADS = 6
MOBA_BLOCK = 256
MOBA_TOPK = 3
C_Q_HEADS = 8
C_KV_HEADS = 2
C_GROUP = C_Q_HEADS // C_KV_HEADS
C_WINDOW = 128
BAND = 128
N_Q_HEADS = A_HEADS + B_HEADS + C_Q_HEADS
N_KV_HEADS = A_HEADS + B_HEADS + C_KV_HEADS
N_GROUPS = 4
EXPERTS_PER_GROUP = 4
N_EXPERTS = N_GROUPS * EXPERTS_PER_GROUP
PAIRS_PER_GROUP = EXPERTS_PER_GROUP * (EXPERTS_PER_GROUP - 1) // 2
N_CLASSES = N_GROUPS * PAIRS_PER_GROUP
PAIR_X = (0, 0, 0, 1, 1, 3)
PAIR_Y = (1, 2, 3, 3, 2, 2)
LN_EPS = 1e-5
NEG = -1e30

V7X_VMEM_BYTES = 64 * 1024 * 1024
VMEM_LIMIT = V7X_VMEM_BYTES - 8 * 1024 * 1024


def _cparams(sem, vmem=VMEM_LIMIT):
    return pltpu.CompilerParams(dimension_semantics=sem, vmem_limit_bytes=vmem)


ROPE_SLOT_Q, ROPE_SLOT_K, ROPE_SLOT_V, ROPE_SLOT_Q_BASE2 = 0, 1, 2, 3
ROPE_SLOTS = 4
LOG2_E = 1.4426950408889634


def _rope_table_kernel(pos_ref, inv_ref, cc_ref, ss_ref):
    ang = pos_ref[...].astype(F32) * inv_ref[...]
    lane = lax.broadcasted_iota(jnp.int32, ang.shape, 1)
    s = jnp.sin(ang)
    c = jnp.cos(ang)
    s = jnp.where(lane < HEAD_DIM // 2, -s, s)
    cc_ref[ROPE_SLOT_Q] = c * ATTN_SCALE
    ss_ref[ROPE_SLOT_Q] = s * ATTN_SCALE
    cc_ref[ROPE_SLOT_K] = c
    ss_ref[ROPE_SLOT_K] = s
    cc_ref[ROPE_SLOT_V] = jnp.ones_like(c)
    ss_ref[ROPE_SLOT_V] = jnp.zeros_like(s)
    cc_ref[ROPE_SLOT_Q_BASE2] = c * (ATTN_SCALE * LOG2_E)
    ss_ref[ROPE_SLOT_Q_BASE2] = s * (ATTN_SCALE * LOG2_E)


def _rope_tables(positions):
    t = positions.size
    tm = 2048
    inv = ROPE_THETA ** (-jnp.arange(0, HEAD_DIM, 2, dtype=F32) / HEAD_DIM)
    inv2 = jnp.concatenate([inv, inv]).reshape(1, HEAD_DIM)
    pos = positions.reshape(t, 1)
    return pl.pallas_call(
        _rope_table_kernel,
        out_shape=(jax.ShapeDtypeStruct((ROPE_SLOTS, t, HEAD_DIM), F32),) * 2,
        grid=(t // tm,),
        in_specs=[pl.BlockSpec((tm, 1), lambda i: (i, 0)),
                  pl.BlockSpec((1, HEAD_DIM), lambda i: (0, 0))],
        out_specs=(pl.BlockSpec((ROPE_SLOTS, tm, HEAD_DIM), lambda i: (0, i, 0)),) * 2,
        compiler_params=_cparams(("arbitrary",)),
        name="rope_tables",
    )(pos, inv2)


def _qkv_kernel(x_ref, w_ref, cc_ref, ss_ref, o_ref, *, heads_per_tile):
    j = pl.program_id(1)
    acc = jnp.dot(x_ref[...], w_ref[...], preferred_element_type=F32)
    for hh in range(heads_per_tile):
        head = j * heads_per_tile + hh
        in_b = (head >= A_HEADS) & (head < A_HEADS + B_HEADS)
        slot = ((head >= N_Q_HEADS).astype(jnp.int32) + (head >= N_Q_HEADS + N_KV_HEADS).astype(jnp.int32)
                + ROPE_SLOT_Q_BASE2 * in_b.astype(jnp.int32))
        t = acc[:, hh * HEAD_DIM:(hh + 1) * HEAD_DIM]
        r = t * cc_ref[slot] + pltpu.roll(t, HEAD_DIM // 2, 1) * ss_ref[slot]
        o_ref[hh] = r.astype(BF16)


def _qkv_proj(hb, w_in_b, layer, cc, ss):
    t, d = hb.shape
    n = w_in_b.shape[2]
    tm, tn = 1024, 1536
    hpt = tn // HEAD_DIM
    return pl.pallas_call(
        functools.partial(_qkv_kernel, heads_per_tile=hpt),
        out_shape=jax.ShapeDtypeStruct((n // HEAD_DIM, t, HEAD_DIM), BF16),
        grid=(t // tm, n // tn),
        in_specs=[pl.BlockSpec((tm, d), lambda i, j: (i, 0)),
                  pl.BlockSpec((None, d, tn), lambda i, j: (layer, 0, j)),
                  pl.BlockSpec((ROPE_SLOTS, tm, HEAD_DIM), lambda i, j: (0, i, 0)),
                  pl.BlockSpec((ROPE_SLOTS, tm, HEAD_DIM), lambda i, j: (0, i, 0))],
        out_specs=pl.BlockSpec((hpt, tm, HEAD_DIM), lambda i, j: (j, i, 0)),
        compiler_params=_cparams(("arbitrary", "arbitrary")),
        name="qkv_rope",
    )(hb, w_in_b, cc, ss)


def _band_bias(max_dist):
    qi = lax.broadcasted_iota(jnp.int32, (BAND, 2 * BAND), 0)
    kj = lax.broadcasted_iota(jnp.int32, (BAND, 2 * BAND), 1)
    dist = qi + BAND - kj
    valid = (dist >= 0) & (dist <= max_dist)
    bias = jnp.where(valid, 0.0, NEG).astype(F32)
    bias_first = jnp.where(valid & (kj >= BAND), 0.0, NEG).astype(F32)
    return bias, bias_first


def _band_block(qb, kwin, vwin, bias):
    g = qb.shape[0] // BAND
    s = lax.dot_general(qb, kwin, (((1,), (1,)), ((), ())), preferred_element_type=F32)
    s = s.reshape(g, BAND, 2 * BAND) + bias[None]
    m = jnp.max(s, axis=-1, keepdims=True)
    p = jnp.exp(s - m)
    l = jnp.sum(p, axis=-1, keepdims=True)
    o = jnp.dot(p.reshape(g * BAND, 2 * BAND).astype(BF16), vwin, preferred_element_type=F32)
    o = o.reshape(g, BAND, HEAD_DIM) / l
    return o, m + jnp.log(l)


def _swa_kernel(sink_ref, q_ref, k_ref, kh_ref, v_ref, vh_ref, o_ref, *, tq):
    kv = pl.program_id(1)
    t = pl.program_id(2)
    bias, bias_first = _band_bias(C_WINDOW - 1)
    for blk in range(tq // BAND):
        qb = q_ref[:, blk * BAND:(blk + 1) * BAND, :].reshape(C_GROUP * BAND, HEAD_DIM)
        if blk == 0:
            kwin = jnp.concatenate([kh_ref[0], k_ref[0, :BAND]], axis=0)
            vwin = jnp.concatenate([vh_ref[0], v_ref[0, :BAND]], axis=0)
            b = jnp.where(t == 0, bias_first, bias)
        else:
            kwin = k_ref[0, (blk - 1) * BAND:(blk + 1) * BAND]
            vwin = v_ref[0, (blk - 1) * BAND:(blk + 1) * BAND]
            b = bias
        o, lse = _band_block(qb, kwin, vwin, b)
        for g in range(C_GROUP):
            keep = jax.nn.sigmoid(lse[g] - sink_ref[kv * C_GROUP + g])
            o_ref[blk * BAND:(blk + 1) * BAND, g * HEAD_DIM:(g + 1) * HEAD_DIM] = (o[g] * keep).astype(BF16)


def _swa_attention(qkv, sinks, batch, seq):
    tq = 1024
    nt = seq // tq
    q0 = A_HEADS + B_HEADS
    k0 = N_Q_HEADS + A_HEADS + B_HEADS
    v0 = N_Q_HEADS + N_KV_HEADS + A_HEADS + B_HEADS
    rb = tq // BAND

    def halo(b, kv, t):
        return jnp.maximum(b * (seq // BAND) + t * rb - 1, 0)

    return pl.pallas_call(
        functools.partial(_swa_kernel, tq=tq),
        out_shape=jax.ShapeDtypeStruct((batch * seq, C_Q_HEADS * HEAD_DIM), BF16),
        grid=(batch, C_KV_HEADS, nt),
        in_specs=[
            pl.BlockSpec(memory_space=pltpu.SMEM),
            pl.BlockSpec((C_GROUP, tq, HEAD_DIM), lambda b, kv, t: (q0 // C_GROUP + kv, b * nt + t, 0)),
            pl.BlockSpec((1, tq, HEAD_DIM), lambda b, kv, t: (k0 + kv, b * nt + t, 0)),
            pl.BlockSpec((1, BAND, HEAD_DIM), lambda b, kv, t: (k0 + kv, halo(b, kv, t), 0)),
            pl.BlockSpec((1, tq, HEAD_DIM), lambda b, kv, t: (v0 + kv, b * nt + t, 0)),
            pl.BlockSpec((1, BAND, HEAD_DIM), lambda b, kv, t: (v0 + kv, halo(b, kv, t), 0)),
        ],
        out_specs=pl.BlockSpec((tq, C_GROUP * HEAD_DIM), lambda b, kv, t: (b * nt + t, kv)),
        compiler_params=_cparams(("arbitrary",) * 3),
        name="swa_sink",
    )(sinks, qkv, qkv, qkv, qkv, qkv)


def _dilated_kernel(*refs, seq):
    ng = len(A_PATTERNS)
    in_refs = refs[:3 * ng]
    o_ref = refs[3 * ng]
    qf, kf, vf, og, lg = refs[3 * ng + 1:]
    for g, (window, dil) in enumerate(A_PATTERNS):
        q_ref, k_ref, v_ref = in_refs[3 * g:3 * g + 3]
        qf[...] = q_ref[0].astype(F32)
        kf[...] = k_ref[0].astype(F32)
        vf[...] = v_ref[0].astype(F32)
        nblk = seq // dil // BAND
        bias, bias_first = _band_bias(window // dil)

        for r in range(dil):
            for n in range(nblk):
                own = pl.ds(r + dil * BAND * n, BAND, stride=dil)
                prev = pl.ds(r + dil * BAND * max(n - 1, 0), BAND, stride=dil)
                qb = qf[own, :].astype(BF16)
                kwin = jnp.concatenate([kf[prev, :], kf[own, :]], axis=0).astype(BF16)
                vwin = jnp.concatenate([vf[prev, :], vf[own, :]], axis=0).astype(BF16)
                o, lse = _band_block(qb, kwin, vwin, bias_first if n == 0 else bias)
                og[g, own, :] = o[0]
                lg[g, own, :] = jnp.broadcast_to(lse[0], (BAND, HEAD_DIM))

    rows = 256

    def combine(c, carry):
        sl = pl.ds(pl.multiple_of(c * rows, rows), rows)
        ls = [lg[g, sl, :] for g in range(ng)]
        m = functools.reduce(jnp.maximum, ls)
        es = [jnp.exp(l - m) for l in ls]
        num = functools.reduce(jnp.add, [e * og[g, sl, :] for g, e in enumerate(es)])
        o_ref[sl, :] = (num / functools.reduce(jnp.add, es)).astype(BF16)
        return carry

    lax.fori_loop(0, seq // rows, combine, 0)


def _dilated_attention(qkv, batch, seq):
    ng = len(A_PATTERNS)
    k0 = N_Q_HEADS
    v0 = N_Q_HEADS + N_KV_HEADS
    in_specs = []
    for g in range(ng):
        for base in (0, k0, v0):
            in_specs.append(pl.BlockSpec(
                (1, seq, HEAD_DIM),
                lambda b, j, h=base + g * A_HEADS_PER_GROUP: (h + j, b, 0)))
    return pl.pallas_call(
        functools.partial(_dilated_kernel, seq=seq),
        out_shape=jax.ShapeDtypeStruct((batch * seq, A_HEADS_PER_GROUP * HEAD_DIM), BF16),
        grid=(batch, A_HEADS_PER_GROUP),
        in_specs=in_specs,
        out_specs=pl.BlockSpec((seq, HEAD_DIM), lambda b, j: (b, j)),
        scratch_shapes=[pltpu.VMEM((seq, HEAD_DIM), F32)] * 3
        + [pltpu.VMEM((ng, seq, HEAD_DIM), F32)] * 2,
        compiler_params=_cparams(("arbitrary",) * 2),
        name="dilated_attn",
    )(*([qkv] * (3 * ng)))


MOBA_TQ = 1024


def _online_update(state, s, v):
    mc = jnp.max(s, axis=1, keepdims=True)
    if state is None:
        p = jnp.exp2(s - mc)
        return mc, jnp.sum(p, axis=1, keepdims=True), jnp.dot(p.astype(BF16), v, preferred_element_type=F32)
    m, l, acc = state
    m_new = jnp.maximum(m, mc)
    alpha = jnp.exp2(m - m_new)
    p = jnp.exp2(s - m_new)
    l_new = alpha * l + jnp.sum(p, axis=1, keepdims=True)
    acc_new = alpha * acc + jnp.dot(p.astype(BF16), v, preferred_element_type=F32)
    return m_new, l_new, acc_new


def _moba_kernel(q_ref, k_ref, v_ref, o_ref, kaug, kmean, *, seq):
    t = pl.program_id(2)
    nblk = seq // MOBA_BLOCK
    blk = MOBA_BLOCK
    tq = MOBA_TQ
    nb = tq // blk

    @pl.when(t == 0)
    def _():
        kaug[:, :HEAD_DIM] = k_ref[0]
        row = lax.broadcasted_iota(jnp.int32, (seq, HEAD_DIM), 0)
        lane = lax.broadcasted_iota(jnp.int32, (seq, HEAD_DIM), 1)
        kaug[:, HEAD_DIM:] = jnp.where(row // blk == lane, 1.0, 0.0).astype(BF16)
        kmean[...] = jnp.zeros_like(kmean)
        for n in range(nblk):
            kb = k_ref[0, n * blk:(n + 1) * blk, :].astype(F32)
            kmean[n:n + 1, :] = jnp.mean(kb, axis=0, keepdims=True)

    q = q_ref[0]
    km = kmean[...]
    km_hi = km.astype(BF16)
    km_lo = (km - km_hi.astype(F32)).astype(BF16)
    nt = (((1,), (1,)), ((), ()))
    gate = (lax.dot_general(km_hi, q, nt, preferred_element_type=F32)
            + lax.dot_general(km_lo, q, nt, preferred_element_type=F32))
    nrow = -(-nblk // 8) * 8
    row = lax.broadcasted_iota(jnp.int32, (nrow, tq), 0).astype(F32)
    own = (t * nb + lax.broadcasted_iota(jnp.int32, (nrow, tq), 1) // blk).astype(F32)
    g = jnp.where(row < own, gate[:nrow], NEG)
    sel = jnp.zeros((nrow, tq), jnp.bool_)
    for _ in range(MOBA_TOPK):
        mx = jnp.max(g, axis=0, keepdims=True)
        first = jnp.min(jnp.where(g == mx, row, float(nrow)), axis=0, keepdims=True)
        pick = row == first
        sel = sel | pick
        g = jnp.where(pick, NEG, g)
    visible = (sel & (row < own)) | (row == own)
    bias_t = jnp.concatenate([jnp.where(visible, 0.0, NEG), jnp.zeros((HEAD_DIM - nrow, tq), F32)], axis=0)
    qaug = jnp.concatenate([q, bias_t.T.astype(BF16)], axis=1)

    ri = lax.broadcasted_iota(jnp.int32, (blk, blk), 0)
    ci = lax.broadcasted_iota(jnp.int32, (blk, blk), 1)
    causal = ri >= ci

    for c in range(seq // tq):
        @pl.when(t == c)
        def _(c=c):
            state = None
            for cc in range(c):
                keys = slice(cc * tq, (cc + 1) * tq)
                s = lax.dot_general(qaug, kaug[keys, :], nt, preferred_element_type=F32)
                state = _online_update(state, s, v_ref[0, keys, :])
            n_split = 2 if c == 0 else 1
            hb = nb // n_split
            parts = []
            for h in range(n_split):
                rows = slice(h * hb * blk, (h + 1) * hb * blk)
                keys = slice(c * tq, c * tq + (h + 1) * hb * blk)
                s = lax.dot_general(qaug[rows], kaug[keys, :], nt, preferred_element_type=F32)
                s = jnp.concatenate(
                    [jnp.concatenate(
                        [jnp.where(causal, s[a * blk:(a + 1) * blk, e * blk:(e + 1) * blk], NEG)
                         if e == a + h * hb else s[a * blk:(a + 1) * blk, e * blk:(e + 1) * blk]
                         for e in range((h + 1) * hb)], axis=1) for a in range(hb)], axis=0)
                st = None if state is None else tuple(x[rows] for x in state)
                parts.append(_online_update(st, s, v_ref[0, keys, :]))
            m, l, acc = (jnp.concatenate([p[i] for p in parts], axis=0) for i in range(3))
            o_ref[...] = (acc / l).astype(BF16)


def _moba_attention(qkv, batch, seq):
    nt = seq // MOBA_TQ
    q0 = A_HEADS
    k0 = N_Q_HEADS + A_HEADS
    v0 = N_Q_HEADS + N_KV_HEADS + A_HEADS
    return pl.pallas_call(
        functools.partial(_moba_kernel, seq=seq),
        out_shape=jax.ShapeDtypeStruct((batch * seq, B_HEADS * HEAD_DIM), BF16),
        grid=(batch, B_HEADS, nt),
        in_specs=[
            pl.BlockSpec((1, MOBA_TQ, HEAD_DIM), lambda b, h, t: (q0 + h, b * nt + t, 0)),
            pl.BlockSpec((1, seq, HEAD_DIM), lambda b, h, t: (k0 + h, b, 0)),
            pl.BlockSpec((1, seq, HEAD_DIM), lambda b, h, t: (v0 + h, b, 0)),
        ],
        out_specs=pl.BlockSpec((MOBA_TQ, HEAD_DIM), lambda b, h, t: (b * nt + t, h)),
        scratch_shapes=[pltpu.VMEM((seq, 2 * HEAD_DIM), BF16), pltpu.VMEM((HEAD_DIM, HEAD_DIM), F32)],
        compiler_params=_cparams(("arbitrary",) * 3),
        name="moba_attn",
    )(qkv, qkv, qkv)


def _merge_kernel(h_ref, oa_ref, ob_ref, oc_ref, g0_ref, g1_ref, g2_ref, b0_ref, b1_ref, b2_ref,
                  wa_ref, wb_ref, wc_ref, o_ref):
    x = h_ref[...]
    acc = None
    for g_ref, b_ref, br_ref, w_ref in ((g0_ref, b0_ref, oa_ref, wa_ref),
                                        (g1_ref, b1_ref, ob_ref, wb_ref),
                                        (g2_ref, b2_ref, oc_ref, wc_ref)):
        gate = jax.nn.sigmoid(jnp.dot(x, g_ref[...], preferred_element_type=F32) + b_ref[...])
        term = gate * jnp.dot(br_ref[...], w_ref[...], preferred_element_type=F32)
        acc = term if acc is None else acc + term
    o_ref[...] = acc.astype(BF16)


def _merge(hb, out_a, out_b, out_c, w_gate_b, b_gate, wa_b, wb_b, wc_b, layer):
    t, d = hb.shape
    tm, tn = 1024, 512
    nj = d // tn
    row = lambda w: pl.BlockSpec((tm, w), lambda j, i: (i, 0))
    gate_w = lambda x: pl.BlockSpec((None, d, tn), lambda j, i, x=x: (layer, 0, x * nj + j))
    gate_b = lambda x: pl.BlockSpec((None, 1, tn), lambda j, i, x=x: (layer, 0, x * nj + j))
    br_w = lambda k: pl.BlockSpec((None, k, tn), lambda j, i: (layer, 0, j))
    bg = b_gate.reshape(b_gate.shape[0], 1, -1)
    return pl.pallas_call(
        _merge_kernel,
        out_shape=jax.ShapeDtypeStruct((t, d), BF16),
        grid=(nj, t // tm),
        in_specs=[row(d), row(out_a.shape[1]), row(out_b.shape[1]), row(out_c.shape[1]),
                  gate_w(0), gate_w(1), gate_w(2), gate_b(0), gate_b(1), gate_b(2),
                  br_w(wa_b.shape[1]), br_w(wb_b.shape[1]), br_w(wc_b.shape[1])],
        out_specs=pl.BlockSpec((tm, tn), lambda j, i: (i, j)),
        compiler_params=_cparams(("arbitrary",) * 2),
        name="branch_merge",
    )(hb, out_a, out_b, out_c, w_gate_b, w_gate_b, w_gate_b, bg, bg, bg, wa_b, wb_b, wc_b)


def _layer_norm_rows(y, g, b):
    mu = jnp.mean(y, axis=-1, keepdims=True)
    yc = y - mu
    var = jnp.mean(yc * yc, axis=-1, keepdims=True)
    return yc * lax.rsqrt(var + LN_EPS) * g + b


def _route(lt):
    tm = lt.shape[1]
    gl = lt[0:N_GROUPS]
    gmax = jnp.max(gl, axis=0, keepdims=True)
    ge = jnp.exp(gl - gmax)
    gp = ge / jnp.sum(ge, axis=0, keepdims=True)
    g_val = jnp.max(gp, axis=0, keepdims=True)
    row = lax.broadcasted_iota(jnp.int32, (N_GROUPS, tm), 0)
    g_sel = jnp.min(jnp.where(gp == g_val, row, N_GROUPS), axis=0, keepdims=True)
    el = jnp.zeros((EXPERTS_PER_GROUP, tm), F32)
    for g in range(N_GROUPS):
        lo = 8 + g * EXPERTS_PER_GROUP
        el = jnp.where(g_sel == g, lt[lo:lo + EXPERTS_PER_GROUP], el)
    v0 = jnp.max(el, axis=0, keepdims=True)
    i0 = jnp.min(jnp.where(el == v0, row, EXPERTS_PER_GROUP), axis=0, keepdims=True)
    el1 = jnp.where(row == i0, -jnp.inf, el)
    v1 = jnp.max(el1, axis=0, keepdims=True)
    i1 = jnp.min(jnp.where(el1 == v1, row, EXPERTS_PER_GROUP), axis=0, keepdims=True)
    e1 = jnp.exp(v1 - v0)
    den = 1.0 + e1
    w0 = 1.0 / den * g_val
    w1 = e1 / den * g_val
    a = jnp.minimum(i0, i1)
    b = jnp.maximum(i0, i1)
    swap = (a == 2) & (b == 3)
    x = jnp.where(swap, b, a)
    y = jnp.where(swap, a, b)
    rank = jnp.where(a == 0, b - 1, jnp.where(a == 1, jnp.where(b == 3, 3, 4), 5))
    cls = (g_sel * PAIRS_PER_GROUP + rank).astype(F32)
    wx = jnp.where(i0 == x, w0, w1)
    wy = jnp.where(i0 == y, w0, w1)
    return cls, wx, wy


def _outproj_ln_route_kernel(m_ref, w_ref, h_ref, g_ref, b_ref, wr_ref, hf_ref, rt_ref, mix_buf,
                             wrs_ref, *, alpha):
    @pl.when(pl.program_id(0) == 0)
    def _():
        mix_buf[...] = jnp.zeros_like(mix_buf)
        wr = wr_ref[...]
        wr_hi = wr.astype(BF16)
        wrs_ref[:, :HEAD_DIM] = wr_hi
        wrs_ref[:, HEAD_DIM:] = (wr - wr_hi.astype(F32)).astype(BF16)

    prev = mix_buf[...]
    mix_buf[...] = jnp.dot(m_ref[...], w_ref[...], preferred_element_type=F32)
    h1 = _layer_norm_rows(alpha * h_ref[...] + prev, g_ref[...], b_ref[...])
    h_hi = h1.astype(BF16)
    h_lo = (h1 - h_hi.astype(F32)).astype(BF16)
    both = jnp.dot(h_hi, wrs_ref[...], preferred_element_type=F32)
    logits = (both[:, :HEAD_DIM] + both[:, HEAD_DIM:]
              + jnp.dot(h_lo, wrs_ref[:, :HEAD_DIM], preferred_element_type=F32))
    cls, wx, wy = _route(logits.T)
    tm = logits.shape[0]
    rt_ref[...] = jnp.concatenate([cls, wx, wy, jnp.zeros((5, tm), F32)], axis=0)
    ext = jnp.concatenate([wx, wy, jnp.zeros((HEAD_DIM - 2, tm), F32)], axis=0).T
    d = h1.shape[1]
    hf_ref[:, :d] = h1
    hf_ref[:, d:] = ext


def _outproj_ln_route(merged, w_out_b, layer, h, ln_g, ln_b, w_router, alpha):
    t, d = h.shape
    tm = 512
    n = t // tm
    cur = lambda i: (jnp.minimum(i, n - 1), 0)
    prv = lambda i: (jnp.maximum(i - 1, 0), 0)
    return pl.pallas_call(
        functools.partial(_outproj_ln_route_kernel, alpha=alpha),
        out_shape=(jax.ShapeDtypeStruct((t, d + HEAD_DIM), F32), jax.ShapeDtypeStruct((8, t), F32)),
        grid=(n + 1,),
        in_specs=[pl.BlockSpec((tm, d), cur),
                  pl.BlockSpec((None, d, d), lambda i: (layer, 0, 0)),
                  pl.BlockSpec((tm, d), prv),
                  pl.BlockSpec((1, d), lambda i: (0, 0)),
                  pl.BlockSpec((1, d), lambda i: (0, 0)),
                  pl.BlockSpec((d, HEAD_DIM), lambda i: (0, 0))],
        out_specs=(pl.BlockSpec((tm, d + HEAD_DIM), prv),
                   pl.BlockSpec((8, tm), lambda i: (0, jnp.maximum(i - 1, 0)))),
        scratch_shapes=[pltpu.VMEM((tm, d), F32), pltpu.VMEM((d, 2 * HEAD_DIM), BF16)],
        compiler_params=_cparams(("arbitrary",)),
        name="outproj_ln1_route",
    )(merged, w_out_b, h, ln_g.reshape(1, d), ln_b.reshape(1, d), w_router)


MOE_TILE = 256
GATHER_TILE = 512
DMA_UNROLL = 8
MLP_CHUNKS = 4


def _dispatch_plan(rt, t):
    cls = rt[0].astype(jnp.int32)
    onehot = (cls[:, None] == jnp.arange(N_CLASSES, dtype=jnp.int32)[None, :]).astype(jnp.int32)
    csum = jnp.cumsum(onehot, axis=0)
    counts = csum[-1]
    rank = jnp.take_along_axis(csum, cls[:, None], axis=1)[:, 0] - 1
    padded = (counts + MOE_TILE - 1) // MOE_TILE * MOE_TILE
    ends = jnp.cumsum(padded)
    pos = ((ends - padded)[cls] + rank).astype(jnp.int32)
    p_rows = t + N_CLASSES * MOE_TILE
    tok = jnp.arange(t, dtype=jnp.int32)
    src = jnp.zeros((p_rows + MOE_TILE,), jnp.int32).at[pos].set(tok)
    n_tiles = p_rows // MOE_TILE
    n_used = (ends[-1] // MOE_TILE).astype(jnp.int32)
    tile_idx = jnp.arange(n_tiles, dtype=jnp.int32)
    tile_start = jnp.minimum(tile_idx, n_used - 1) * MOE_TILE
    tile_c = jnp.sum((ends[None, :] <= tile_start[:, None]).astype(jnp.int32), axis=1)
    tile_c = jnp.minimum(tile_c, N_CLASSES - 1)
    group = tile_c // PAIRS_PER_GROUP
    pair = tile_c % PAIRS_PER_GROUP
    tile_x = group * EXPERTS_PER_GROUP + jnp.asarray(PAIR_X, jnp.int32)[pair]
    tile_y = group * EXPERTS_PER_GROUP + jnp.asarray(PAIR_Y, jnp.int32)[pair]
    return src, pos, tile_x.astype(jnp.int32), tile_y.astype(jnp.int32), n_used.reshape(1)


def _expert_mlp_kernel(src_ref, tx_ref, ty_ref, nu_ref, h_hbm, wgx_ref, wux_ref, wdx_ref,
                       wgy_ref, wuy_ref, wdy_ref, y_ref, xbuf, sem):
    i = pl.program_id(0)
    n_used = nu_ref[0]
    slot = i % 2
    d = y_ref.shape[1]

    def issue_rows(tile, dst_slot, lo, hi):
        for j in range(lo, hi):
            row = src_ref[tile * MOE_TILE + j]
            pltpu.make_async_copy(h_hbm.at[pl.ds(row, 1)], xbuf.at[dst_slot, pl.ds(j, 1)],
                                  sem.at[dst_slot]).start(priority=j % 2)

    def wait_tile(s):
        pltpu.make_async_copy(h_hbm.at[pl.ds(0, MOE_TILE)], xbuf.at[s], sem.at[s]).wait()

    @pl.when(i == 0)
    def _():
        def first(c, carry):
            base = pl.multiple_of(c * DMA_UNROLL, DMA_UNROLL)
            for j in range(DMA_UNROLL):
                row = src_ref[base + j]
                pltpu.make_async_copy(h_hbm.at[pl.ds(row, 1)], xbuf.at[0, pl.ds(base + j, 1)],
                                      sem.at[0]).start()
            return carry
        lax.fori_loop(0, MOE_TILE // DMA_UNROLL, first, 0)

    @pl.when(i < n_used)
    def _():
        wait_tile(slot)
        nxt = 1 - slot
        de = wgx_ref.shape[2]
        cw = de // MLP_CHUNKS
        n_pieces = 4 * MLP_CHUNKS
        bounds = [MOE_TILE * k // n_pieces for k in range(n_pieces + 1)]
        piece = iter(range(n_pieces))

        def issue_next():
            k = next(piece)
            issue_rows(i + 1, nxt, bounds[k], bounds[k + 1])

        out = None
        for wg_ref, wu_ref, wd_ref, lane in ((wgx_ref, wux_ref, wdx_ref, d), (wgy_ref, wuy_ref, wdy_ref, d + 1)):
            hid = []
            for c in range(MLP_CHUNKS):
                cols = slice(c * cw, (c + 1) * cw)
                issue_next()
                a = jnp.dot(xbuf[slot, :, :d].astype(BF16), wg_ref[0, :, cols], preferred_element_type=F32)
                issue_next()
                u = jnp.dot(xbuf[slot, :, :d].astype(BF16), wu_ref[0, :, cols], preferred_element_type=F32)
                hid.append((jax.nn.silu(a) * u).astype(BF16))
            y = None
            for c in range(MLP_CHUNKS):
                part = jnp.dot(hid[c], wd_ref[0, c * cw:(c + 1) * cw, :], preferred_element_type=F32)
                y = part if y is None else y + part
            y = y * xbuf[slot, :, lane:lane + 1]
            out = y if out is None else out + y
        y_ref[...] = out

    @pl.when(i >= n_used)
    def _():
        y_ref[...] = jnp.zeros_like(y_ref)

    @pl.when(i == n_used)
    def _():
        wait_tile(slot)


def _expert_mlp(h1x, src, tile_x, tile_y, n_used, wg_b, wu_b, wd_b, layer):
    dx = h1x.shape[1]
    d = dx - HEAD_DIM
    p = src.shape[0] - MOE_TILE
    de = wg_b.shape[2]
    ex = lambda i, src, tx, ty, nu: (layer * N_EXPERTS + tx[i], 0, 0)
    ey = lambda i, src, tx, ty, nu: (layer * N_EXPERTS + ty[i], 0, 0)
    once = pl.Buffered(1)
    return pl.pallas_call(
        _expert_mlp_kernel,
        out_shape=jax.ShapeDtypeStruct((p, d), F32),
        grid_spec=pltpu.PrefetchScalarGridSpec(
            num_scalar_prefetch=4, grid=(p // MOE_TILE,),
            in_specs=[pl.BlockSpec(memory_space=pl.ANY),
                      pl.BlockSpec((1, d, de), ex, pipeline_mode=once),
                      pl.BlockSpec((1, d, de), ex, pipeline_mode=once),
                      pl.BlockSpec((1, de, d), ex, pipeline_mode=once),
                      pl.BlockSpec((1, d, de), ey),
                      pl.BlockSpec((1, d, de), ey),
                      pl.BlockSpec((1, de, d), ey)],
            out_specs=pl.BlockSpec((MOE_TILE, d), lambda i, src, tx, ty, nu: (i, 0)),
            scratch_shapes=[pltpu.VMEM((2, MOE_TILE, dx), F32), pltpu.SemaphoreType.DMA((2,))]),
        compiler_params=_cparams(("arbitrary",)),
        name="moe_expert_mlp",
    )(src, tile_x, tile_y, n_used, h1x, wg_b, wu_b, wd_b, wg_b, wu_b, wd_b)


def _combine_ln_kernel(pos_ref, y_hbm, h_ref, g_ref, b_ref, *refs, alpha):
    out_refs, (buf, sem) = refs[:-2], refs[-2:]
    i = pl.program_id(0)
    tm = h_ref.shape[0]

    def issue_tile(tile, slot):
        def issue(c, carry):
            base = pl.multiple_of(c * DMA_UNROLL, DMA_UNROLL)
            dst = buf.at[slot, pl.ds(base, DMA_UNROLL)]
            for j in range(DMA_UNROLL):
                row = pos_ref[tile * tm + base + j]
                pltpu.make_async_copy(y_hbm.at[pl.ds(row, 1)], dst.at[pl.ds(j, 1)],
                                      sem.at[slot]).start(priority=j % 2)
            return carry

        lax.fori_loop(0, tm // DMA_UNROLL, issue, 0)

    @pl.when(i == 0)
    def _():
        issue_tile(0, 0)

    @pl.when(i + 1 < pl.num_programs(0))
    def _():
        issue_tile(i + 1, (i + 1) % 2)

    slot = i % 2
    pltpu.make_async_copy(y_hbm.at[pl.ds(0, tm)], buf.at[slot], sem.at[slot]).wait()
    out = _layer_norm_rows(alpha * h_ref[...] + buf[slot], g_ref[...], b_ref[...])
    for o_ref in out_refs:
        o_ref[...] = out.astype(o_ref.dtype)


def _combine_ln(pos, y_sorted, h1x, ln_g, ln_b, alpha, out_dtypes):
    t = h1x.shape[0]
    d = y_sorted.shape[1]
    tm = GATHER_TILE
    return pl.pallas_call(
        functools.partial(_combine_ln_kernel, alpha=alpha),
        out_shape=tuple(jax.ShapeDtypeStruct((t, d), dt) for dt in out_dtypes),
        grid_spec=pltpu.PrefetchScalarGridSpec(
            num_scalar_prefetch=1, grid=(t // tm,),
            in_specs=[pl.BlockSpec(memory_space=pl.ANY),
                      pl.BlockSpec((tm, d), lambda i, pos: (i, 0)),
                      pl.BlockSpec((1, d), lambda i, pos: (0, 0)),
                      pl.BlockSpec((1, d), lambda i, pos: (0, 0))],
            out_specs=tuple(pl.BlockSpec((tm, d), lambda i, pos: (i, 0)) for _ in out_dtypes),
            scratch_shapes=[pltpu.VMEM((2, tm, d), F32), pltpu.SemaphoreType.DMA((2,))]),
        compiler_params=_cparams(("arbitrary",)),
        name="moe_combine_ln2",
    )(pos, y_sorted, h1x, ln_g.reshape(1, d), ln_b.reshape(1, d))


def _moe(h1x, rt, wg_b, wu_b, wd_b, layer, ln_g, ln_b, alpha, out_dtypes):
    t = h1x.shape[0]
    src, pos, tile_x, tile_y, n_used = _dispatch_plan(rt, t)
    y_sorted = _expert_mlp(h1x, src, tile_x, tile_y, n_used, wg_b, wu_b, wd_b, layer)
    return _combine_ln(pos, y_sorted, h1x, ln_g, ln_b, alpha, out_dtypes)


def kernel(x, positions, w_in, w_gate, b_gate, w_branch_a, w_branch_b, w_branch_c, w_out, sinks,
           ln1_g, ln1_b, w_router_group, w_router_expert, w_exp_gate, w_exp_up, w_exp_down,
           ln2_g, ln2_b):
    batch, seq, d = x.shape
    depth = w_in.shape[0]
    t = batch * seq
    alpha = (2 * depth) ** 0.25
    cc, ss = _rope_tables(positions)
    hf = x.reshape(t, d)
    hb = hf.astype(BF16)
    w_in_b, w_gate_b, w_out_b = w_in.astype(BF16), w_gate.astype(BF16), w_out.astype(BF16)
    wa_b, wb_b, wc_b = w_branch_a.astype(BF16), w_branch_b.astype(BF16), w_branch_c.astype(BF16)
    de = w_exp_gate.shape[-1]
    wg_b = w_exp_gate.astype(BF16).reshape(depth * N_EXPERTS, d, de)
    wu_b = w_exp_up.astype(BF16).reshape(depth * N_EXPERTS, d, de)
    wd_b = w_exp_down.astype(BF16).reshape(depth * N_EXPERTS, de, d)
    for l in range(depth):
        qkv = _qkv_proj(hb, w_in_b, l, cc, ss)
        out_a = _dilated_attention(qkv, batch, seq)
        out_b = _moba_attention(qkv, batch, seq)
        out_c = _swa_attention(qkv, sinks[l], batch, seq)
        merged = _merge(hb, out_a, out_b, out_c, w_gate_b, b_gate, wa_b, wb_b, wc_b, l)
        w_router = jnp.zeros((d, HEAD_DIM), F32)
        w_router = w_router.at[:, 0:N_GROUPS].set(w_router_group[l])
        w_router = w_router.at[:, 8:8 + N_EXPERTS].set(w_router_expert[l])
        h1f, rt = _outproj_ln_route(merged, w_out_b, l, hf, ln1_g[l], ln1_b[l], w_router, alpha)
        last = l == depth - 1
        outs = _moe(h1f, rt, wg_b, wu_b, wd_b, l, ln2_g[l], ln2_b[l], alpha,
                    (F32,) if last else (F32, BF16))
        hf, hb = (outs[0], None) if last else outs
    return hf.reshape(batch, seq, d)
```

```python
import functools

import jax
import jax.numpy as jnp
from jax import lax
from jax.experimental import pallas as pl
from jax.experimental.pallas import tpu as pltpu

F32 = jnp.float32
BF16 = jnp.bfloat16

HEAD_DIM = 128
ATTN_SCALE = HEAD_DIM ** -0.5
ROPE_THETA = 10000.0
A_PATTERNS = ((128, 1), (512, 4), (2048, 16))
A_HEADS_PER_GROUP = 2
A_HEADS = A_HEADS_PER_GROUP * len(A_PATTERNS)
B_HEADS = 6
MOBA_BLOCK = 256
MOBA_TOPK = 3
C_Q_HEADS = 8
C_KV_HEADS = 2
C_GROUP = C_Q_HEADS // C_KV_HEADS
C_WINDOW = 128
BAND = 128
N_Q_HEADS = A_HEADS + B_HEADS + C_Q_HEADS
N_KV_HEADS = A_HEADS + B_HEADS + C_KV_HEADS
N_GROUPS = 4
EXPERTS_PER_GROUP = 4
N_EXPERTS = N_GROUPS * EXPERTS_PER_GROUP
PAIRS_PER_GROUP = EXPERTS_PER_GROUP * (EXPERTS_PER_GROUP - 1) // 2
N_CLASSES = N_GROUPS * PAIRS_PER_GROUP
PAIR_X = (0, 0, 0, 1, 1, 3)
PAIR_Y = (1, 2, 3, 3, 2, 2)
LN_EPS = 1e-5
NEG = -1e30

V7X_VMEM_BYTES = 64 * 1024 * 1024
VMEM_LIMIT = V7X_VMEM_BYTES - 8 * 1024 * 1024


def _cparams(sem, vmem=VMEM_LIMIT):
    return pltpu.CompilerParams(dimension_semantics=sem, vmem_limit_bytes=vmem)


ROPE_SLOT_Q, ROPE_SLOT_K, ROPE_SLOT_V, ROPE_SLOT_Q_BASE2 = 0, 1, 2, 3
ROPE_SLOTS = 4
LOG2_E = 1.4426950408889634


def _rope_table_kernel(pos_ref, inv_ref, cc_ref, ss_ref):
    ang = pos_ref[...].astype(F32) * inv_ref[...]
    lane = lax.broadcasted_iota(jnp.int32, ang.shape, 1)
    s = jnp.sin(ang)
    c = jnp.cos(ang)
    s = jnp.where(lane < HEAD_DIM // 2, -s, s)
    cc_ref[ROPE_SLOT_Q] = c * ATTN_SCALE
    ss_ref[ROPE_SLOT_Q] = s * ATTN_SCALE
    cc_ref[ROPE_SLOT_K] = c
    ss_ref[ROPE_SLOT_K] = s
    cc_ref[ROPE_SLOT_V] = jnp.ones_like(c)
    ss_ref[ROPE_SLOT_V] = jnp.zeros_like(s)
    cc_ref[ROPE_SLOT_Q_BASE2] = c * (ATTN_SCALE * LOG2_E)
    ss_ref[ROPE_SLOT_Q_BASE2] = s * (ATTN_SCALE * LOG2_E)


def _rope_tables(positions):
    t = positions.size
    tm = 2048
    inv = ROPE_THETA ** (-jnp.arange(0, HEAD_DIM, 2, dtype=F32) / HEAD_DIM)
    inv2 = jnp.concatenate([inv, inv]).reshape(1, HEAD_DIM)
    pos = positions.reshape(t, 1)
    return pl.pallas_call(
        _rope_table_kernel,
        out_shape=(jax.ShapeDtypeStruct((ROPE_SLOTS, t, HEAD_DIM), F32),) * 2,
        grid=(t // tm,),
        in_specs=[pl.BlockSpec((tm, 1), lambda i: (i, 0)),
                  pl.BlockSpec((1, HEAD_DIM), lambda i: (0, 0))],
        out_specs=(pl.BlockSpec((ROPE_SLOTS, tm, HEAD_DIM), lambda i: (0, i, 0)),) * 2,
        compiler_params=_cparams(("arbitrary",)),
        name="rope_tables",
    )(pos, inv2)


def _qkv_kernel(x_ref, w_ref, cc_ref, ss_ref, o_ref, *, heads_per_tile):
    j = pl.program_id(1)
    acc = jnp.dot(x_ref[...], w_ref[...], preferred_element_type=F32)
    for hh in range(heads_per_tile):
        head = j * heads_per_tile + hh
        in_b = (head >= A_HEADS) & (head < A_HEADS + B_HEADS)
        slot = ((head >= N_Q_HEADS).astype(jnp.int32) + (head >= N_Q_HEADS + N_KV_HEADS).astype(jnp.int32)
                + ROPE_SLOT_Q_BASE2 * in_b.astype(jnp.int32))
        t = acc[:, hh * HEAD_DIM:(hh + 1) * HEAD_DIM]
        r = t * cc_ref[slot] + pltpu.roll(t, HEAD_DIM // 2, 1) * ss_ref[slot]
        o_ref[hh] = r.astype(BF16)


def _qkv_proj(hb, w_in_b, layer, cc, ss):
    t, d = hb.shape
    n = w_in_b.shape[2]
    tm, tn = 1024, 1536
    hpt = tn // HEAD_DIM
    return pl.pallas_call(
        functools.partial(_qkv_kernel, heads_per_tile=hpt),
        out_shape=jax.ShapeDtypeStruct((n // HEAD_DIM, t, HEAD_DIM), BF16),
        grid=(t // tm, n // tn),
        in_specs=[pl.BlockSpec((tm, d), lambda i, j: (i, 0)),
                  pl.BlockSpec((None, d, tn), lambda i, j: (layer, 0, j)),
                  pl.BlockSpec((ROPE_SLOTS, tm, HEAD_DIM), lambda i, j: (0, i, 0)),
                  pl.BlockSpec((ROPE_SLOTS, tm, HEAD_DIM), lambda i, j: (0, i, 0))],
        out_specs=pl.BlockSpec((hpt, tm, HEAD_DIM), lambda i, j: (j, i, 0)),
        compiler_params=_cparams(("arbitrary", "arbitrary")),
        name="qkv_rope",
    )(hb, w_in_b, cc, ss)


def _band_bias(max_dist):
    qi = lax.broadcasted_iota(jnp.int32, (BAND, 2 * BAND), 0)
    kj = lax.broadcasted_iota(jnp.int32, (BAND, 2 * BAND), 1)
    dist = qi + BAND - kj
    valid = (dist >= 0) & (dist <= max_dist)
    bias = jnp.where(valid, 0.0, NEG).astype(F32)
    bias_first = jnp.where(valid & (kj >= BAND), 0.0, NEG).astype(F32)
    return bias, bias_first


def _band_block(qb, kwin, vwin, bias):
    g = qb.shape[0] // BAND
    s = lax.dot_general(qb, kwin, (((1,), (1,)), ((), ())), preferred_element_type=F32)
    s = s.reshape(g, BAND, 2 * BAND) + bias[None]
    m = jnp.max(s, axis=-1, keepdims=True)
    p = jnp.exp(s - m)
    l = jnp.sum(p, axis=-1, keepdims=True)
    o = jnp.dot(p.reshape(g * BAND, 2 * BAND).astype(BF16), vwin, preferred_element_type=F32)
    o = o.reshape(g, BAND, HEAD_DIM) / l
    return o, m + jnp.log(l)


def _swa_kernel(sink_ref, q_ref, k_ref, kh_ref, v_ref, vh_ref, o_ref, *, tq):
    kv = pl.program_id(1)
    t = pl.program_id(2)
    bias, bias_first = _band_bias(C_WINDOW - 1)
    for blk in range(tq // BAND):
        qb = q_ref[:, blk * BAND:(blk + 1) * BAND, :].reshape(C_GROUP * BAND, HEAD_DIM)
        if blk == 0:
            kwin = jnp.concatenate([kh_ref[0], k_ref[0, :BAND]], axis=0)
            vwin = jnp.concatenate([vh_ref[0], v_ref[0, :BAND]], axis=0)
            b = jnp.where(t == 0, bias_first, bias)
        else:
            kwin = k_ref[0, (blk - 1) * BAND:(blk + 1) * BAND]
            vwin = v_ref[0, (blk - 1) * BAND:(blk + 1) * BAND]
            b = bias
        o, lse = _band_block(qb, kwin, vwin, b)
        for g in range(C_GROUP):
            keep = jax.nn.sigmoid(lse[g] - sink_ref[kv * C_GROUP + g])
            o_ref[blk * BAND:(blk + 1) * BAND, g * HEAD_DIM:(g + 1) * HEAD_DIM] = (o[g] * keep).astype(BF16)


def _swa_attention(qkv, sinks, batch, seq):
    tq = 1024
    nt = seq // tq
    q0 = A_HEADS + B_HEADS
    k0 = N_Q_HEADS + A_HEADS + B_HEADS
    v0 = N_Q_HEADS + N_KV_HEADS + A_HEADS + B_HEADS
    rb = tq // BAND

    def halo(b, kv, t):
        return jnp.maximum(b * (seq // BAND) + t * rb - 1, 0)

    return pl.pallas_call(
        functools.partial(_swa_kernel, tq=tq),
        out_shape=jax.ShapeDtypeStruct((batch * seq, C_Q_HEADS * HEAD_DIM), BF16),
        grid=(batch, C_KV_HEADS, nt),
        in_specs=[
            pl.BlockSpec(memory_space=pltpu.SMEM),
            pl.BlockSpec((C_GROUP, tq, HEAD_DIM), lambda b, kv, t: (q0 // C_GROUP + kv, b * nt + t, 0)),
            pl.BlockSpec((1, tq, HEAD_DIM), lambda b, kv, t: (k0 + kv, b * nt + t, 0)),
            pl.BlockSpec((1, BAND, HEAD_DIM), lambda b, kv, t: (k0 + kv, halo(b, kv, t), 0)),
            pl.BlockSpec((1, tq, HEAD_DIM), lambda b, kv, t: (v0 + kv, b * nt + t, 0)),
            pl.BlockSpec((1, BAND, HEAD_DIM), lambda b, kv, t: (v0 + kv, halo(b, kv, t), 0)),
        ],
        out_specs=pl.BlockSpec((tq, C_GROUP * HEAD_DIM), lambda b, kv, t: (b * nt + t, kv)),
        compiler_params=_cparams(("arbitrary",) * 3),
        name="swa_sink",
    )(sinks, qkv, qkv, qkv, qkv, qkv)


def _dilated_kernel(*refs, seq):
    ng = len(A_PATTERNS)
    in_refs = refs[:3 * ng]
    o_ref = refs[3 * ng]
    qf, kf, vf, og, lg = refs[3 * ng + 1:]
    for g, (window, dil) in enumerate(A_PATTERNS):
        q_ref, k_ref, v_ref = in_refs[3 * g:3 * g + 3]
        qf[...] = q_ref[0].astype(F32)
        kf[...] = k_ref[0].astype(F32)
        vf[...] = v_ref[0].astype(F32)
        nblk = seq // dil // BAND
        bias, bias_first = _band_bias(window // dil)

        for r in range(dil):
            for n in range(nblk):
                own = pl.ds(r + dil * BAND * n, BAND, stride=dil)
                prev = pl.ds(r + dil * BAND * max(n - 1, 0), BAND, stride=dil)
                qb = qf[own, :].astype(BF16)
                kwin = jnp.concatenate([kf[prev, :], kf[own, :]], axis=0).astype(BF16)
                vwin = jnp.concatenate([vf[prev, :], vf[own, :]], axis=0).astype(BF16)
                o, lse = _band_block(qb, kwin, vwin, bias_first if n == 0 else bias)
                og[g, own, :] = o[0]
                lg[g, own, :] = jnp.broadcast_to(lse[0], (BAND, HEAD_DIM))

    rows = 256

    def combine(c, carry):
        sl = pl.ds(pl.multiple_of(c * rows, rows), rows)
        ls = [lg[g, sl, :] for g in range(ng)]
        m = functools.reduce(jnp.maximum, ls)
        es = [jnp.exp(l - m) for l in ls]
        num = functools.reduce(jnp.add, [e * og[g, sl, :] for g, e in enumerate(es)])
        o_ref[sl, :] = (num / functools.reduce(jnp.add, es)).astype(BF16)
        return carry

    lax.fori_loop(0, seq // rows, combine, 0)


def _dilated_attention(qkv, batch, seq):
    ng = len(A_PATTERNS)
    k0 = N_Q_HEADS
    v0 = N_Q_HEADS + N_KV_HEADS
    in_specs = []
    for g in range(ng):
        for base in (0, k0, v0):
            in_specs.append(pl.BlockSpec(
                (1, seq, HEAD_DIM),
                lambda b, j, h=base + g * A_HEADS_PER_GROUP: (h + j, b, 0)))
    return pl.pallas_call(
        functools.partial(_dilated_kernel, seq=seq),
        out_shape=jax.ShapeDtypeStruct((batch * seq, A_HEADS_PER_GROUP * HEAD_DIM), BF16),
        grid=(batch, A_HEADS_PER_GROUP),
        in_specs=in_specs,
        out_specs=pl.BlockSpec((seq, HEAD_DIM), lambda b, j: (b, j)),
        scratch_shapes=[pltpu.VMEM((seq, HEAD_DIM), F32)] * 3
        + [pltpu.VMEM((ng, seq, HEAD_DIM), F32)] * 2,
        compiler_params=_cparams(("arbitrary",) * 2),
        name="dilated_attn",
    )(*([qkv] * (3 * ng)))


MOBA_TQ = 1024


def _online_update(state, s, v):
    mc = jnp.max(s, axis=1, keepdims=True)
    if state is None:
        p = jnp.exp2(s - mc)
        return mc, jnp.sum(p, axis=1, keepdims=True), jnp.dot(p.astype(BF16), v, preferred_element_type=F32)
    m, l, acc = state
    m_new = jnp.maximum(m, mc)
    alpha = jnp.exp2(m - m_new)
    p = jnp.exp2(s - m_new)
    l_new = alpha * l + jnp.sum(p, axis=1, keepdims=True)
    acc_new = alpha * acc + jnp.dot(p.astype(BF16), v, preferred_element_type=F32)
    return m_new, l_new, acc_new


def _moba_kernel(q_ref, k_ref, v_ref, o_ref, kaug, kmean, *, seq):
    t = pl.program_id(2)
    nblk = seq // MOBA_BLOCK
    blk = MOBA_BLOCK
    tq = MOBA_TQ
    nb = tq // blk

    @pl.when(t == 0)
    def _():
        kaug[:, :HEAD_DIM] = k_ref[0]
        row = lax.broadcasted_iota(jnp.int32, (seq, HEAD_DIM), 0)
        lane = lax.broadcasted_iota(jnp.int32, (seq, HEAD_DIM), 1)
        kaug[:, HEAD_DIM:] = jnp.where(row // blk == lane, 1.0, 0.0).astype(BF16)
        kmean[...] = jnp.zeros_like(kmean)
        for n in range(nblk):
            kb = k_ref[0, n * blk:(n + 1) * blk, :].astype(F32)
            kmean[n:n + 1, :] = jnp.mean(kb, axis=0, keepdims=True)

    q = q_ref[0]
    km = kmean[...]
    km_hi = km.astype(BF16)
    km_lo = (km - km_hi.astype(F32)).astype(BF16)
    nt = (((1,), (1,)), ((), ()))
    gate = (lax.dot_general(km_hi, q, nt, preferred_element_type=F32)
            + lax.dot_general(km_lo, q, nt, preferred_element_type=F32))
    nrow = -(-nblk // 8) * 8
    row = lax.broadcasted_iota(jnp.int32, (nrow, tq), 0).astype(F32)
    own = (t * nb + lax.broadcasted_iota(jnp.int32, (nrow, tq), 1) // blk).astype(F32)
    g = jnp.where(row < own, gate[:nrow], NEG)
    sel = jnp.zeros((nrow, tq), jnp.bool_)
    for _ in range(MOBA_TOPK):
        mx = jnp.max(g, axis=0, keepdims=True)
        first = jnp.min(jnp.where(g == mx, row, float(nrow)), axis=0, keepdims=True)
        pick = row == first
        sel = sel | pick
        g = jnp.where(pick, NEG, g)
    visible = (sel & (row < own)) | (row == own)
    bias_t = jnp.concatenate([jnp.where(visible, 0.0, NEG), jnp.zeros((HEAD_DIM - nrow, tq), F32)], axis=0)
    qaug = jnp.concatenate([q, bias_t.T.astype(BF16)], axis=1)

    ri = lax.broadcasted_iota(jnp.int32, (blk, blk), 0)
    ci = lax.broadcasted_iota(jnp.int32, (blk, blk), 1)
    causal = ri >= ci

    for c in range(seq // tq):
        @pl.when(t == c)
        def _(c=c):
            state = None
            for cc in range(c):
                keys = slice(cc * tq, (cc + 1) * tq)
                s = lax.dot_general(qaug, kaug[keys, :], nt, preferred_element_type=F32)
                state = _online_update(state, s, v_ref[0, keys, :])
            n_split = 2 if c == 0 else 1
            hb = nb // n_split
            parts = []
            for h in range(n_split):
                rows = slice(h * hb * blk, (h + 1) * hb * blk)
                keys = slice(c * tq, c * tq + (h + 1) * hb * blk)
                s = lax.dot_general(qaug[rows], kaug[keys, :], nt, preferred_element_type=F32)
                s = jnp.concatenate(
                    [jnp.concatenate(
                        [jnp.where(causal, s[a * blk:(a + 1) * blk, e * blk:(e + 1) * blk], NEG)
                         if e == a + h * hb else s[a * blk:(a + 1) * blk, e * blk:(e + 1) * blk]
                         for e in range((h + 1) * hb)], axis=1) for a in range(hb)], axis=0)
                st = None if state is None else tuple(x[rows] for x in state)
                parts.append(_online_update(st, s, v_ref[0, keys, :]))
            m, l, acc = (jnp.concatenate([p[i] for p in parts], axis=0) for i in range(3))
            o_ref[...] = (acc / l).astype(BF16)


def _moba_attention(qkv, batch, seq):
    nt = seq // MOBA_TQ
    q0 = A_HEADS
    k0 = N_Q_HEADS + A_HEADS
    v0 = N_Q_HEADS + N_KV_HEADS + A_HEADS
    return pl.pallas_call(
        functools.partial(_moba_kernel, seq=seq),
        out_shape=jax.ShapeDtypeStruct((batch * seq, B_HEADS * HEAD_DIM), BF16),
        grid=(batch, B_HEADS, nt),
        in_specs=[
            pl.BlockSpec((1, MOBA_TQ, HEAD_DIM), lambda b, h, t: (q0 + h, b * nt + t, 0)),
            pl.BlockSpec((1, seq, HEAD_DIM), lambda b, h, t: (k0 + h, b, 0)),
            pl.BlockSpec((1, seq, HEAD_DIM), lambda b, h, t: (v0 + h, b, 0)),
        ],
        out_specs=pl.BlockSpec((MOBA_TQ, HEAD_DIM), lambda b, h, t: (b * nt + t, h)),
        scratch_shapes=[pltpu.VMEM((seq, 2 * HEAD_DIM), BF16), pltpu.VMEM((HEAD_DIM, HEAD_DIM), F32)],
        compiler_params=_cparams(("arbitrary",) * 3),
        name="moba_attn",
    )(qkv, qkv, qkv)


def _merge_kernel(h_ref, oa_ref, ob_ref, oc_ref, g0_ref, g1_ref, g2_ref, b0_ref, b1_ref, b2_ref,
                  wa_ref, wb_ref, wc_ref, eg_ref, eu_ref, ed_ref, o_ref, egb_ref, eub_ref, edb_ref):
    egb_ref[...] = eg_ref[...].astype(BF16)
    eub_ref[...] = eu_ref[...].astype(BF16)
    edb_ref[...] = ed_ref[...].astype(BF16)
    x = h_ref[...]
    acc = None
    for g_ref, b_ref, br_ref, w_ref in ((g0_ref, b0_ref, oa_ref, wa_ref),
                                        (g1_ref, b1_ref, ob_ref, wb_ref),
                                        (g2_ref, b2_ref, oc_ref, wc_ref)):
        gate = jax.nn.sigmoid(jnp.dot(x, g_ref[...], preferred_element_type=F32) + b_ref[...])
        term = gate * jnp.dot(br_ref[...], w_ref[...], preferred_element_type=F32)
        acc = term if acc is None else acc + term
    o_ref[...] = acc.astype(BF16)


def _merge(hb, out_a, out_b, out_c, w_gate_b, b_gate, wa_b, wb_b, wc_b, w_exp, layer):
    t, d = hb.shape
    tm, tn = 1024, 512
    nj = d // tn
    ni = t // tm
    depth = w_gate_b.shape[0]
    slab_in, slab_out, slab_shape = [], [], []
    for w in w_exp:
        rows = w.shape[0] // depth
        sr = rows // (nj * ni)
        slab_in.append(pl.BlockSpec((sr, w.shape[1]), lambda j, i: (layer * nj * ni + j * ni + i, 0)))
        slab_out.append(pl.BlockSpec((sr, w.shape[1]), lambda j, i: (j * ni + i, 0)))
        slab_shape.append(jax.ShapeDtypeStruct((rows, w.shape[1]), BF16))
    row = lambda w: pl.BlockSpec((tm, w), lambda j, i: (i, 0))
    gate_w = lambda x: pl.BlockSpec((None, d, tn), lambda j, i, x=x: (layer, 0, x * nj + j))
    gate_b = lambda x: pl.BlockSpec((None, 1, tn), lambda j, i, x=x: (layer, 0, x * nj + j))
    br_w = lambda k: pl.BlockSpec((None, k, tn), lambda j, i: (layer, 0, j))
    bg = b_gate.reshape(b_gate.shape[0], 1, -1)
    return pl.pallas_call(
        _merge_kernel,
        out_shape=(jax.ShapeDtypeStruct((t, d), BF16), *slab_shape),
        grid=(nj, ni),
        in_specs=[row(d), row(out_a.shape[1]), row(out_b.shape[1]), row(out_c.shape[1]),
                  gate_w(0), gate_w(1), gate_w(2), gate_b(0), gate_b(1), gate_b(2),
                  br_w(wa_b.shape[1]), br_w(wb_b.shape[1]), br_w(wc_b.shape[1]), *slab_in],
        out_specs=(pl.BlockSpec((tm, tn), lambda j, i: (i, j)), *slab_out),
        compiler_params=_cparams(("arbitrary",) * 2),
        name="branch_merge",
    )(hb, out_a, out_b, out_c, w_gate_b, w_gate_b, w_gate_b, bg, bg, bg, wa_b, wb_b, wc_b, *w_exp)


def _layer_norm_rows(y, g, b):
    mu = jnp.mean(y, axis=-1, keepdims=True)
    yc = y - mu
    var = jnp.mean(yc * yc, axis=-1, keepdims=True)
    return yc * lax.rsqrt(var + LN_EPS) * g + b


def _route(lt):
    tm = lt.shape[1]
    gl = lt[0:N_GROUPS]
    gmax = jnp.max(gl, axis=0, keepdims=True)
    ge = jnp.exp(gl - gmax)
    gp = ge / jnp.sum(ge, axis=0, keepdims=True)
    g_val = jnp.max(gp, axis=0, keepdims=True)
    row = lax.broadcasted_iota(jnp.int32, (N_GROUPS, tm), 0)
    g_sel = jnp.min(jnp.where(gp == g_val, row, N_GROUPS), axis=0, keepdims=True)
    el = jnp.zeros((EXPERTS_PER_GROUP, tm), F32)
    for g in range(N_GROUPS):
        lo = 8 + g * EXPERTS_PER_GROUP
        el = jnp.where(g_sel == g, lt[lo:lo + EXPERTS_PER_GROUP], el)
    v0 = jnp.max(el, axis=0, keepdims=True)
    i0 = jnp.min(jnp.where(el == v0, row, EXPERTS_PER_GROUP), axis=0, keepdims=True)
    el1 = jnp.where(row == i0, -jnp.inf, el)
    v1 = jnp.max(el1, axis=0, keepdims=True)
    i1 = jnp.min(jnp.where(el1 == v1, row, EXPERTS_PER_GROUP), axis=0, keepdims=True)
    e1 = jnp.exp(v1 - v0)
    den = 1.0 + e1
    w0 = 1.0 / den * g_val
    w1 = e1 / den * g_val
    a = jnp.minimum(i0, i1)
    b = jnp.maximum(i0, i1)
    swap = (a == 2) & (b == 3)
    x = jnp.where(swap, b, a)
    y = jnp.where(swap, a, b)
    rank = jnp.where(a == 0, b - 1, jnp.where(a == 1, jnp.where(b == 3, 3, 4), 5))
    cls = (g_sel * PAIRS_PER_GROUP + rank).astype(F32)
    wx = jnp.where(i0 == x, w0, w1)
    wy = jnp.where(i0 == y, w0, w1)
    return cls, wx, wy


def _outproj_ln_route_kernel(m_ref, w_ref, h_ref, g_ref, b_ref, wr_ref, hf_ref, rt_ref, mix_buf,
                             wrs_ref, *, alpha):
    @pl.when(pl.program_id(0) == 0)
    def _():
        mix_buf[...] = jnp.zeros_like(mix_buf)
        wr = wr_ref[...]
        wr_hi = wr.astype(BF16)
        wrs_ref[:, :HEAD_DIM] = wr_hi
        wrs_ref[:, HEAD_DIM:] = (wr - wr_hi.astype(F32)).astype(BF16)

    prev = mix_buf[...]
    mix_buf[...] = jnp.dot(m_ref[...], w_ref[...], preferred_element_type=F32)
    h1 = _layer_norm_rows(alpha * h_ref[...] + prev, g_ref[...], b_ref[...])
    h_hi = h1.astype(BF16)
    h_lo = (h1 - h_hi.astype(F32)).astype(BF16)
    both = jnp.dot(h_hi, wrs_ref[...], preferred_element_type=F32)
    logits = (both[:, :HEAD_DIM] + both[:, HEAD_DIM:]
              + jnp.dot(h_lo, wrs_ref[:, :HEAD_DIM], preferred_element_type=F32))
    cls, wx, wy = _route(logits.T)
    tm = logits.shape[0]
    rt_ref[...] = jnp.concatenate([cls, wx, wy, jnp.zeros((5, tm), F32)], axis=0)
    ext = jnp.concatenate([wx, wy, jnp.zeros((HEAD_DIM - 2, tm), F32)], axis=0).T
    d = h1.shape[1]
    hf_ref[:, :d] = h1
    hf_ref[:, d:] = ext


def _outproj_ln_route(merged, w_out_b, layer, h, ln_g, ln_b, w_router, alpha):
    t, d = h.shape
    tm = 512
    n = t // tm
    cur = lambda i: (jnp.minimum(i, n - 1), 0)
    prv = lambda i: (jnp.maximum(i - 1, 0), 0)
    return pl.pallas_call(
        functools.partial(_outproj_ln_route_kernel, alpha=alpha),
        out_shape=(jax.ShapeDtypeStruct((t, d + HEAD_DIM), F32), jax.ShapeDtypeStruct((8, t), F32)),
        grid=(n + 1,),
        in_specs=[pl.BlockSpec((tm, d), cur),
                  pl.BlockSpec((None, d, d), lambda i: (layer, 0, 0)),
                  pl.BlockSpec((tm, d), prv),
                  pl.BlockSpec((1, d), lambda i: (0, 0)),
                  pl.BlockSpec((1, d), lambda i: (0, 0)),
                  pl.BlockSpec((d, HEAD_DIM), lambda i: (0, 0))],
        out_specs=(pl.BlockSpec((tm, d + HEAD_DIM), prv),
                   pl.BlockSpec((8, tm), lambda i: (0, jnp.maximum(i - 1, 0)))),
        scratch_shapes=[pltpu.VMEM((tm, d), F32), pltpu.VMEM((d, 2 * HEAD_DIM), BF16)],
        compiler_params=_cparams(("arbitrary",)),
        name="outproj_ln1_route",
    )(merged, w_out_b, h, ln_g.reshape(1, d), ln_b.reshape(1, d), w_router)


MOE_TILE = 256
GATHER_TILE = 512
DMA_UNROLL = 8
MLP_CHUNKS = 4


def _dispatch_plan(rt, t):
    cls = rt[0].astype(jnp.int32)
    onehot = (cls[:, None] == jnp.arange(N_CLASSES, dtype=jnp.int32)[None, :]).astype(jnp.int32)
    csum = jnp.cumsum(onehot, axis=0)
    counts = csum[-1]
    rank = jnp.take_along_axis(csum, cls[:, None], axis=1)[:, 0] - 1
    padded = (counts + MOE_TILE - 1) // MOE_TILE * MOE_TILE
    ends = jnp.cumsum(padded)
    pos = ((ends - padded)[cls] + rank).astype(jnp.int32)
    p_rows = t + N_CLASSES * MOE_TILE
    tok = jnp.arange(t, dtype=jnp.int32)
    src = jnp.zeros((p_rows + MOE_TILE,), jnp.int32).at[pos].set(tok)
    n_tiles = p_rows // MOE_TILE
    n_used = (ends[-1] // MOE_TILE).astype(jnp.int32)
    tile_idx = jnp.arange(n_tiles, dtype=jnp.int32)
    tile_start = jnp.minimum(tile_idx, n_used - 1) * MOE_TILE
    tile_c = jnp.sum((ends[None, :] <= tile_start[:, None]).astype(jnp.int32), axis=1)
    tile_c = jnp.minimum(tile_c, N_CLASSES - 1)
    group = tile_c // PAIRS_PER_GROUP
    pair = tile_c % PAIRS_PER_GROUP
    tile_x = group * EXPERTS_PER_GROUP + jnp.asarray(PAIR_X, jnp.int32)[pair]
    tile_y = group * EXPERTS_PER_GROUP + jnp.asarray(PAIR_Y, jnp.int32)[pair]
    return src, pos, tile_x.astype(jnp.int32), tile_y.astype(jnp.int32), n_used.reshape(1)


def _expert_mlp_kernel(src_ref, tx_ref, ty_ref, nu_ref, h_hbm, wgx_ref, wux_ref, wdx_ref,
                       wgy_ref, wuy_ref, wdy_ref, y_ref, xbuf, sem):
    i = pl.program_id(0)
    n_used = nu_ref[0]
    slot = i % 2
    d = y_ref.shape[1]

    def issue_rows(tile, dst_slot, lo, hi):
        for j in range(lo, hi):
            row = src_ref[tile * MOE_TILE + j]
            pltpu.make_async_copy(h_hbm.at[pl.ds(row, 1)], xbuf.at[dst_slot, pl.ds(j, 1)],
                                  sem.at[dst_slot]).start(priority=j % 2)

    def wait_tile(s):
        pltpu.make_async_copy(h_hbm.at[pl.ds(0, MOE_TILE)], xbuf.at[s], sem.at[s]).wait()

    @pl.when(i == 0)
    def _():
        def first(c, carry):
            base = pl.multiple_of(c * DMA_UNROLL, DMA_UNROLL)
            for j in range(DMA_UNROLL):
                row = src_ref[base + j]
                pltpu.make_async_copy(h_hbm.at[pl.ds(row, 1)], xbuf.at[0, pl.ds(base + j, 1)],
                                      sem.at[0]).start()
            return carry
        lax.fori_loop(0, MOE_TILE // DMA_UNROLL, first, 0)

    @pl.when(i < n_used)
    def _():
        wait_tile(slot)
        nxt = 1 - slot
        de = wgx_ref.shape[2]
        cw = de // MLP_CHUNKS
        n_pieces = 4 * MLP_CHUNKS
        bounds = [MOE_TILE * k // n_pieces for k in range(n_pieces + 1)]
        piece = iter(range(n_pieces))

        def issue_next():
            k = next(piece)
            issue_rows(i + 1, nxt, bounds[k], bounds[k + 1])

        out = None
        for wg_ref, wu_ref, wd_ref, lane in ((wgx_ref, wux_ref, wdx_ref, d), (wgy_ref, wuy_ref, wdy_ref, d + 1)):
            hid = []
            for c in range(MLP_CHUNKS):
                cols = slice(c * cw, (c + 1) * cw)
                issue_next()
                a = jnp.dot(xbuf[slot, :, :d].astype(BF16), wg_ref[0, :, cols], preferred_element_type=F32)
                issue_next()
                u = jnp.dot(xbuf[slot, :, :d].astype(BF16), wu_ref[0, :, cols], preferred_element_type=F32)
                hid.append((jax.nn.silu(a) * u).astype(BF16))
            y = None
            for c in range(MLP_CHUNKS):
                part = jnp.dot(hid[c], wd_ref[0, c * cw:(c + 1) * cw, :], preferred_element_type=F32)
                y = part if y is None else y + part
            y = y * xbuf[slot, :, lane:lane + 1]
            out = y if out is None else out + y
        y_ref[...] = out

    @pl.when(i >= n_used)
    def _():
        y_ref[...] = jnp.zeros_like(y_ref)

    @pl.when(i == n_used)
    def _():
        wait_tile(slot)


def _expert_mlp(h1x, src, tile_x, tile_y, n_used, wg_b, wu_b, wd_b, layer):
    dx = h1x.shape[1]
    d = dx - HEAD_DIM
    p = src.shape[0] - MOE_TILE
    de = wg_b.shape[2]
    ex = lambda i, src, tx, ty, nu: (layer * N_EXPERTS + tx[i], 0, 0)
    ey = lambda i, src, tx, ty, nu: (layer * N_EXPERTS + ty[i], 0, 0)
    once = pl.Buffered(1)
    return pl.pallas_call(
        _expert_mlp_kernel,
        out_shape=jax.ShapeDtypeStruct((p, d), F32),
        grid_spec=pltpu.PrefetchScalarGridSpec(
            num_scalar_prefetch=4, grid=(p // MOE_TILE,),
            in_specs=[pl.BlockSpec(memory_space=pl.ANY),
                      pl.BlockSpec((1, d, de), ex, pipeline_mode=once),
                      pl.BlockSpec((1, d, de), ex, pipeline_mode=once),
                      pl.BlockSpec((1, de, d), ex, pipeline_mode=once),
                      pl.BlockSpec((1, d, de), ey),
                      pl.BlockSpec((1, d, de), ey),
                      pl.BlockSpec((1, de, d), ey)],
            out_specs=pl.BlockSpec((MOE_TILE, d), lambda i, src, tx, ty, nu: (i, 0)),
            scratch_shapes=[pltpu.VMEM((2, MOE_TILE, dx), F32), pltpu.SemaphoreType.DMA((2,))]),
        compiler_params=_cparams(("arbitrary",)),
        name="moe_expert_mlp",
    )(src, tile_x, tile_y, n_used, h1x, wg_b, wu_b, wd_b, wg_b, wu_b, wd_b)


def _combine_ln_kernel(pos_ref, y_hbm, h_ref, g_ref, b_ref, *refs, alpha):
    out_refs, (buf, sem) = refs[:-2], refs[-2:]
    i = pl.program_id(0)
    tm = h_ref.shape[0]

    def issue_tile(tile, slot):
        def issue(c, carry):
            base = pl.multiple_of(c * DMA_UNROLL, DMA_UNROLL)
            dst = buf.at[slot, pl.ds(base, DMA_UNROLL)]
            for j in range(DMA_UNROLL):
                row = pos_ref[tile * tm + base + j]
                pltpu.make_async_copy(y_hbm.at[pl.ds(row, 1)], dst.at[pl.ds(j, 1)],
                                      sem.at[slot]).start(priority=j % 2)
            return carry

        lax.fori_loop(0, tm // DMA_UNROLL, issue, 0)

    @pl.when(i == 0)
    def _():
        issue_tile(0, 0)

    @pl.when(i + 1 < pl.num_programs(0))
    def _():
        issue_tile(i + 1, (i + 1) % 2)

    slot = i % 2
    pltpu.make_async_copy(y_hbm.at[pl.ds(0, tm)], buf.at[slot], sem.at[slot]).wait()
    out = _layer_norm_rows(alpha * h_ref[...] + buf[slot], g_ref[...], b_ref[...])
    for o_ref in out_refs:
        o_ref[...] = out.astype(o_ref.dtype)


def _combine_ln(pos, y_sorted, h1x, ln_g, ln_b, alpha, out_dtypes):
    t = h1x.shape[0]
    d = y_sorted.shape[1]
    tm = GATHER_TILE
    return pl.pallas_call(
        functools.partial(_combine_ln_kernel, alpha=alpha),
        out_shape=tuple(jax.ShapeDtypeStruct((t, d), dt) for dt in out_dtypes),
        grid_spec=pltpu.PrefetchScalarGridSpec(
            num_scalar_prefetch=1, grid=(t // tm,),
            in_specs=[pl.BlockSpec(memory_space=pl.ANY),
                      pl.BlockSpec((tm, d), lambda i, pos: (i, 0)),
                      pl.BlockSpec((1, d), lambda i, pos: (0, 0)),
                      pl.BlockSpec((1, d), lambda i, pos: (0, 0))],
            out_specs=tuple(pl.BlockSpec((tm, d), lambda i, pos: (i, 0)) for _ in out_dtypes),
            scratch_shapes=[pltpu.VMEM((2, tm, d), F32), pltpu.SemaphoreType.DMA((2,))]),
        compiler_params=_cparams(("arbitrary",)),
        name="moe_combine_ln2",
    )(pos, y_sorted, h1x, ln_g.reshape(1, d), ln_b.reshape(1, d))


def _moe(h1x, rt, wg_b, wu_b, wd_b, layer, ln_g, ln_b, alpha, out_dtypes):
    t = h1x.shape[0]
    src, pos, tile_x, tile_y, n_used = _dispatch_plan(rt, t)
    y_sorted = _expert_mlp(h1x, src, tile_x, tile_y, n_used, wg_b, wu_b, wd_b, layer)
    return _combine_ln(pos, y_sorted, h1x, ln_g, ln_b, alpha, out_dtypes)


def kernel(x, positions, w_in, w_gate, b_gate, w_branch_a, w_branch_b, w_branch_c, w_out, sinks,
           ln1_g, ln1_b, w_router_group, w_router_expert, w_exp_gate, w_exp_up, w_exp_down,
           ln2_g, ln2_b):
    batch, seq, d = x.shape
    depth = w_in.shape[0]
    t = batch * seq
    alpha = (2 * depth) ** 0.25
    cc, ss = _rope_tables(positions)
    hf = x.reshape(t, d)
    hb = hf.astype(BF16)
    w_in_b, w_gate_b, w_out_b = w_in.astype(BF16), w_gate.astype(BF16), w_out.astype(BF16)
    wa_b, wb_b, wc_b = w_branch_a.astype(BF16), w_branch_b.astype(BF16), w_branch_c.astype(BF16)
    de = w_exp_gate.shape[-1]
    w_exp = (w_exp_gate.reshape(depth * N_EXPERTS * d, de), w_exp_up.reshape(depth * N_EXPERTS * d, de),
             w_exp_down.reshape(depth * N_EXPERTS * de, d))
    for l in range(depth):
        qkv = _qkv_proj(hb, w_in_b, l, cc, ss)
        out_a = _dilated_attention(qkv, batch, seq)
        out_b = _moba_attention(qkv, batch, seq)
        out_c = _swa_attention(qkv, sinks[l], batch, seq)
        merged, wg_b, wu_b, wd_b = _merge(hb, out_a, out_b, out_c, w_gate_b, b_gate, wa_b, wb_b, wc_b,
                                          w_exp, l)
        wg_b, wu_b = wg_b.reshape(N_EXPERTS, d, de), wu_b.reshape(N_EXPERTS, d, de)
        wd_b = wd_b.reshape(N_EXPERTS, de, d)
        w_router = jnp.zeros((d, HEAD_DIM), F32)
        w_router = w_router.at[:, 0:N_GROUPS].set(w_router_group[l])
        w_router = w_router.at[:, 8:8 + N_EXPERTS].set(w_router_expert[l])
        h1f, rt = _outproj_ln_route(merged, w_out_b, l, hf, ln1_g[l], ln1_b[l], w_router, alpha)
        last = l == depth - 1
        outs = _moe(h1f, rt, wg_b, wu_b, wd_b, 0, ln2_g[l], ln2_b[l], alpha,
                    (F32,) if last else (F32, BF16))
        hf, hb = (outs[0], None) if last else outs
    return hf.reshape(batch, seq, d)
```

```python
import functools

import jax
import jax.numpy as jnp
from jax import lax
from jax.experimental import pallas as pl
from jax.experimental.pallas import tpu as pltpu

F32 = jnp.float32
BF16 = jnp.bfloat16

HEAD_DIM = 128
ATTN_SCALE = HEAD_DIM ** -0.5
ROPE_THETA = 10000.0
A_PATTERNS = ((128, 1), (512, 4), (2048, 16))
A_HEADS_PER_GROUP = 2
A_HEADS = A_HEADS_PER_GROUP * len(A_PATTERNS)
B_HEADS = 6
MOBA_BLOCK = 256
MOBA_TOPK = 3
C_Q_HEADS = 8
C_KV_HEADS = 2
C_GROUP = C_Q_HEADS // C_KV_HEADS
C_WINDOW = 128
BAND = 128
N_Q_HEADS = A_HEADS + B_HEADS + C_Q_HEADS
N_KV_HEADS = A_HEADS + B_HEADS + C_KV_HEADS
N_GROUPS = 4
EXPERTS_PER_GROUP = 4
N_EXPERTS = N_GROUPS * EXPERTS_PER_GROUP
PAIRS_PER_GROUP = EXPERTS_PER_GROUP * (EXPERTS_PER_GROUP - 1) // 2
N_CLASSES = N_GROUPS * PAIRS_PER_GROUP
PAIR_X = (0, 0, 0, 1, 1, 3)
PAIR_Y = (1, 2, 3, 3, 2, 2)
LN_EPS = 1e-5
NEG = -1e30

V7X_VMEM_BYTES = 64 * 1024 * 1024
VMEM_LIMIT = V7X_VMEM_BYTES - 8 * 1024 * 1024
MLP_VMEM_LIMIT = V7X_VMEM_BYTES - 2 * 1024 * 1024


def _cparams(sem, vmem=VMEM_LIMIT):
    return pltpu.CompilerParams(dimension_semantics=sem, vmem_limit_bytes=vmem)


ROPE_SLOT_Q, ROPE_SLOT_K, ROPE_SLOT_V, ROPE_SLOT_Q_BASE2 = 0, 1, 2, 3
ROPE_SLOTS = 4
LOG2_E = 1.4426950408889634


def _rope_table_kernel(pos_ref, inv_ref, x_ref, cc_ref, ss_ref, xb_ref):
    xb_ref[...] = x_ref[...].astype(BF16)
    ang = pos_ref[...].astype(F32) * inv_ref[...]
    lane = lax.broadcasted_iota(jnp.int32, ang.shape, 1)
    s = jnp.sin(ang)
    c = jnp.cos(ang)
    s = jnp.where(lane < HEAD_DIM // 2, -s, s)
    cc_ref[ROPE_SLOT_Q] = c * ATTN_SCALE
    ss_ref[ROPE_SLOT_Q] = s * ATTN_SCALE
    cc_ref[ROPE_SLOT_K] = c
    ss_ref[ROPE_SLOT_K] = s
    cc_ref[ROPE_SLOT_V] = jnp.ones_like(c)
    ss_ref[ROPE_SLOT_V] = jnp.zeros_like(s)
    cc_ref[ROPE_SLOT_Q_BASE2] = c * (ATTN_SCALE * LOG2_E)
    ss_ref[ROPE_SLOT_Q_BASE2] = s * (ATTN_SCALE * LOG2_E)


def _rope_tables(positions, x):
    t, d = x.shape
    tm = 512
    inv = ROPE_THETA ** (-jnp.arange(0, HEAD_DIM, 2, dtype=F32) / HEAD_DIM)
    inv2 = jnp.concatenate([inv, inv]).reshape(1, HEAD_DIM)
    pos = positions.reshape(t, 1)
    table = jax.ShapeDtypeStruct((ROPE_SLOTS, t, HEAD_DIM), F32)
    table_spec = pl.BlockSpec((ROPE_SLOTS, tm, HEAD_DIM), lambda i: (0, i, 0))
    return pl.pallas_call(
        _rope_table_kernel,
        out_shape=(table, table, jax.ShapeDtypeStruct((t, d), BF16)),
        grid=(t // tm,),
        in_specs=[pl.BlockSpec((tm, 1), lambda i: (i, 0)),
                  pl.BlockSpec((1, HEAD_DIM), lambda i: (0, 0)),
                  pl.BlockSpec((tm, d), lambda i: (i, 0))],
        out_specs=(table_spec, table_spec, pl.BlockSpec((tm, d), lambda i: (i, 0))),
        compiler_params=_cparams(("arbitrary",)),
        name="rope_tables",
    )(pos, inv2, x)


def _side_cast_specs(sides, n_steps, step_of):
    ins, outs, shapes = [], [], []
    for w, rows, layer in sides:
        sr = rows // n_steps
        ins.append(pl.BlockSpec((sr, w.shape[1]), lambda *g, layer=layer: (layer * n_steps + step_of(*g), 0)))
        outs.append(pl.BlockSpec((sr, w.shape[1]), lambda *g: (step_of(*g), 0)))
        shapes.append(jax.ShapeDtypeStruct((rows, w.shape[1]), BF16))
    return ins, outs, shapes


def _cast_sides(in_refs, out_refs):
    for i_ref, o_ref in zip(in_refs, out_refs, strict=True):
        o_ref[...] = i_ref[...].astype(BF16)


def _qkv_kernel(x_ref, w_ref, cc_ref, ss_ref, *refs, heads_per_tile):
    n_side = len(refs) // 2
    o_ref = refs[n_side]
    _cast_sides(refs[:n_side], refs[n_side + 1:])
    j = pl.program_id(1)
    acc = jnp.dot(x_ref[...], w_ref[...], preferred_element_type=F32)
    for hh in range(heads_per_tile):
        head = j * heads_per_tile + hh
        in_b = (head >= A_HEADS) & (head < A_HEADS + B_HEADS)
        slot = ((head >= N_Q_HEADS).astype(jnp.int32) + (head >= N_Q_HEADS + N_KV_HEADS).astype(jnp.int32)
                + ROPE_SLOT_Q_BASE2 * in_b.astype(jnp.int32))
        t = acc[:, hh * HEAD_DIM:(hh + 1) * HEAD_DIM]
        r = t * cc_ref[slot] + pltpu.roll(t, HEAD_DIM // 2, 1) * ss_ref[slot]
        o_ref[hh] = r.astype(BF16)


def _qkv_proj(hb, w_in_b, cc, ss, sides=()):
    t, d = hb.shape
    n = w_in_b.shape[1]
    tm, tn = 1024, 1536
    hpt = tn // HEAD_DIM
    ni, nj = t // tm, n // tn
    side_in, side_out, side_shape = _side_cast_specs(sides, ni * nj, lambda i, j: i * nj + j)
    return pl.pallas_call(
        functools.partial(_qkv_kernel, heads_per_tile=hpt),
        out_shape=(jax.ShapeDtypeStruct((n // HEAD_DIM, t, HEAD_DIM), BF16), *side_shape),
        grid=(ni, nj),
        in_specs=[pl.BlockSpec((tm, d), lambda i, j: (i, 0)),
                  pl.BlockSpec((d, tn), lambda i, j: (0, j)),
                  pl.BlockSpec((ROPE_SLOTS, tm, HEAD_DIM), lambda i, j: (0, i, 0)),
                  pl.BlockSpec((ROPE_SLOTS, tm, HEAD_DIM), lambda i, j: (0, i, 0)), *side_in],
        out_specs=(pl.BlockSpec((hpt, tm, HEAD_DIM), lambda i, j: (j, i, 0)), *side_out),
        compiler_params=_cparams(("arbitrary", "arbitrary")),
        name="qkv_rope",
    )(hb, w_in_b, cc, ss, *[w for w, _, _ in sides])


def _band_bias(max_dist):
    qi = lax.broadcasted_iota(jnp.int32, (BAND, 2 * BAND), 0)
    kj = lax.broadcasted_iota(jnp.int32, (BAND, 2 * BAND), 1)
    dist = qi + BAND - kj
    valid = (dist >= 0) & (dist <= max_dist)
    bias = jnp.where(valid, 0.0, NEG).astype(F32)
    bias_first = jnp.where(valid & (kj >= BAND), 0.0, NEG).astype(F32)
    return bias, bias_first


def _band_block(qb, kwin, vwin, bias):
    g = qb.shape[0] // BAND
    s = lax.dot_general(qb, kwin, (((1,), (1,)), ((), ())), preferred_element_type=F32)
    s = s.reshape(g, BAND, 2 * BAND) + bias[None]
    m = jnp.max(s, axis=-1, keepdims=True)
    p = jnp.exp(s - m)
    l = jnp.sum(p, axis=-1, keepdims=True)
    o = jnp.dot(p.reshape(g * BAND, 2 * BAND).astype(BF16), vwin, preferred_element_type=F32)
    o = o.reshape(g, BAND, HEAD_DIM) / l
    return o, m + jnp.log(l)


def _swa_kernel(sink_ref, q_ref, k_ref, kh_ref, v_ref, vh_ref, o_ref, *, tq):
    kv = pl.program_id(1)
    t = pl.program_id(2)
    bias, bias_first = _band_bias(C_WINDOW - 1)
    for blk in range(tq // BAND):
        qb = q_ref[:, blk * BAND:(blk + 1) * BAND, :].reshape(C_GROUP * BAND, HEAD_DIM)
        if blk == 0:
            kwin = jnp.concatenate([kh_ref[0], k_ref[0, :BAND]], axis=0)
            vwin = jnp.concatenate([vh_ref[0], v_ref[0, :BAND]], axis=0)
            b = jnp.where(t == 0, bias_first, bias)
        else:
            kwin = k_ref[0, (blk - 1) * BAND:(blk + 1) * BAND]
            vwin = v_ref[0, (blk - 1) * BAND:(blk + 1) * BAND]
            b = bias
        o, lse = _band_block(qb, kwin, vwin, b)
        for g in range(C_GROUP):
            keep = jax.nn.sigmoid(lse[g] - sink_ref[kv * C_GROUP + g])
            o_ref[blk * BAND:(blk + 1) * BAND, g * HEAD_DIM:(g + 1) * HEAD_DIM] = (o[g] * keep).astype(BF16)


def _swa_attention(qkv, sinks, batch, seq):
    tq = 1024
    nt = seq // tq
    q0 = A_HEADS + B_HEADS
    k0 = N_Q_HEADS + A_HEADS + B_HEADS
    v0 = N_Q_HEADS + N_KV_HEADS + A_HEADS + B_HEADS
    rb = tq // BAND

    def halo(b, kv, t):
        return jnp.maximum(b * (seq // BAND) + t * rb - 1, 0)

    return pl.pallas_call(
        functools.partial(_swa_kernel, tq=tq),
        out_shape=jax.ShapeDtypeStruct((batch * seq, C_Q_HEADS * HEAD_DIM), BF16),
        grid=(batch, C_KV_HEADS, nt),
        in_specs=[
            pl.BlockSpec(memory_space=pltpu.SMEM),
            pl.BlockSpec((C_GROUP, tq, HEAD_DIM), lambda b, kv, t: (q0 // C_GROUP + kv, b * nt + t, 0)),
            pl.BlockSpec((1, tq, HEAD_DIM), lambda b, kv, t: (k0 + kv, b * nt + t, 0)),
            pl.BlockSpec((1, BAND, HEAD_DIM), lambda b, kv, t: (k0 + kv, halo(b, kv, t), 0)),
            pl.BlockSpec((1, tq, HEAD_DIM), lambda b, kv, t: (v0 + kv, b * nt + t, 0)),
            pl.BlockSpec((1, BAND, HEAD_DIM), lambda b, kv, t: (v0 + kv, halo(b, kv, t), 0)),
        ],
        out_specs=pl.BlockSpec((tq, C_GROUP * HEAD_DIM), lambda b, kv, t: (b * nt + t, kv)),
        compiler_params=_cparams(("arbitrary",) * 3),
        name="swa_sink",
    )(sinks, qkv, qkv, qkv, qkv, qkv)


def _dilated_kernel(*refs, seq):
    ng = len(A_PATTERNS)
    in_refs = refs[:3 * ng]
    o_ref = refs[3 * ng]
    qf, kf, vf, og, lg = refs[3 * ng + 1:]
    for g, (window, dil) in enumerate(A_PATTERNS):
        q_ref, k_ref, v_ref = in_refs[3 * g:3 * g + 3]
        qf[...] = q_ref[0].astype(F32)
        kf[...] = k_ref[0].astype(F32)
        vf[...] = v_ref[0].astype(F32)
        nblk = seq // dil // BAND
        bias, bias_first = _band_bias(window // dil)

        for r in range(dil):
            for n in range(nblk):
                own = pl.ds(r + dil * BAND * n, BAND, stride=dil)
                prev = pl.ds(r + dil * BAND * max(n - 1, 0), BAND, stride=dil)
                qb = qf[own, :].astype(BF16)
                kwin = jnp.concatenate([kf[prev, :], kf[own, :]], axis=0).astype(BF16)
                vwin = jnp.concatenate([vf[prev, :], vf[own, :]], axis=0).astype(BF16)
                o, lse = _band_block(qb, kwin, vwin, bias_first if n == 0 else bias)
                og[g, own, :] = o[0]
                lg[g, own, :] = jnp.broadcast_to(lse[0], (BAND, HEAD_DIM))

    rows = 256

    def combine(c, carry):
        sl = pl.ds(pl.multiple_of(c * rows, rows), rows)
        ls = [lg[g, sl, :] for g in range(ng)]
        m = functools.reduce(jnp.maximum, ls)
        es = [jnp.exp(l - m) for l in ls]
        num = functools.reduce(jnp.add, [e * og[g, sl, :] for g, e in enumerate(es)])
        o_ref[sl, :] = (num / functools.reduce(jnp.add, es)).astype(BF16)
        return carry

    lax.fori_loop(0, seq // rows, combine, 0)


def _dilated_attention(qkv, batch, seq):
    ng = len(A_PATTERNS)
    k0 = N_Q_HEADS
    v0 = N_Q_HEADS + N_KV_HEADS
    in_specs = []
    for g in range(ng):
        for base in (0, k0, v0):
            in_specs.append(pl.BlockSpec(
                (1, seq, HEAD_DIM),
                lambda b, j, h=base + g * A_HEADS_PER_GROUP: (h + j, b, 0)))
    return pl.pallas_call(
        functools.partial(_dilated_kernel, seq=seq),
        out_shape=jax.ShapeDtypeStruct((batch * seq, A_HEADS_PER_GROUP * HEAD_DIM), BF16),
        grid=(batch, A_HEADS_PER_GROUP),
        in_specs=in_specs,
        out_specs=pl.BlockSpec((seq, HEAD_DIM), lambda b, j: (b, j)),
        scratch_shapes=[pltpu.VMEM((seq, HEAD_DIM), F32)] * 3
        + [pltpu.VMEM((ng, seq, HEAD_DIM), F32)] * 2,
        compiler_params=_cparams(("arbitrary",) * 2),
        name="dilated_attn",
    )(*([qkv] * (3 * ng)))


MOBA_TQ = 1024
MOBA_HEADS = 2


def _online_update(state, s, v):
    mc = jnp.max(s, axis=1, keepdims=True)
    if state is None:
        p = jnp.exp2(s - mc)
        return mc, jnp.sum(p, axis=1, keepdims=True), jnp.dot(p.astype(BF16), v, preferred_element_type=F32)
    m, l, acc = state
    m_new = jnp.maximum(m, mc)
    alpha = jnp.exp2(m - m_new)
    p = jnp.exp2(s - m_new)
    l_new = alpha * l + jnp.sum(p, axis=1, keepdims=True)
    acc_new = alpha * acc + jnp.dot(p.astype(BF16), v, preferred_element_type=F32)
    return m_new, l_new, acc_new


def _moba_kernel(q_ref, k_ref, v_ref, o_ref, kaug, kmean, *, seq):
    t = pl.program_id(2)
    nblk = seq // MOBA_BLOCK
    blk = MOBA_BLOCK
    tq = MOBA_TQ
    nb = tq // blk
    nh = MOBA_HEADS

    @pl.when(t == 0)
    def _():
        row = lax.broadcasted_iota(jnp.int32, (seq, HEAD_DIM), 0)
        lane = lax.broadcasted_iota(jnp.int32, (seq, HEAD_DIM), 1)
        onehot = jnp.where(row // blk == lane, 1.0, 0.0).astype(BF16)
        for h in range(nh):
            kaug[h, :, :HEAD_DIM] = k_ref[h]
            kaug[h, :, HEAD_DIM:] = onehot
            kmean[h] = jnp.zeros((HEAD_DIM, HEAD_DIM), F32)
            for n in range(nblk):
                kb = k_ref[h, n * blk:(n + 1) * blk, :].astype(F32)
                kmean[h, n:n + 1, :] = jnp.mean(kb, axis=0, keepdims=True)

    nt = (((1,), (1,)), ((), ()))
    nrow = -(-nblk // 8) * 8
    row = lax.broadcasted_iota(jnp.int32, (nrow, tq), 0).astype(F32)
    own = (t * nb + lax.broadcasted_iota(jnp.int32, (nrow, tq), 1) // blk).astype(F32)
    qaugs = []
    for h in range(nh):
        q = q_ref[h]
        km = kmean[h]
        km_hi = km.astype(BF16)
        km_lo = (km - km_hi.astype(F32)).astype(BF16)
        gate = (lax.dot_general(km_hi, q, nt, preferred_element_type=F32)
                + lax.dot_general(km_lo, q, nt, preferred_element_type=F32))
        g = jnp.where(row < own, gate[:nrow], NEG)
        sel = jnp.zeros((nrow, tq), jnp.bool_)
        for _ in range(MOBA_TOPK):
            mx = jnp.max(g, axis=0, keepdims=True)
            first = jnp.min(jnp.where(g == mx, row, float(nrow)), axis=0, keepdims=True)
            pick = row == first
            sel = sel | pick
            g = jnp.where(pick, NEG, g)
        visible = (sel & (row < own)) | (row == own)
        bias_t = jnp.concatenate([jnp.where(visible, 0.0, NEG), jnp.zeros((HEAD_DIM - nrow, tq), F32)], axis=0)
        qaugs.append(jnp.concatenate([q, bias_t.T.astype(BF16)], axis=1))

    ri = lax.broadcasted_iota(jnp.int32, (blk, blk), 0)
    ci = lax.broadcasted_iota(jnp.int32, (blk, blk), 1)
    causal = ri >= ci

    for c in range(seq // tq):
        @pl.when(t == c)
        def _(c=c):
            states = [None] * nh
            for cc in range(c):
                keys = slice(cc * tq, (cc + 1) * tq)
                for h in range(nh):
                    s = lax.dot_general(qaugs[h], kaug[h, keys, :], nt, preferred_element_type=F32)
                    states[h] = _online_update(states[h], s, v_ref[h, keys, :])
            n_split = 2 if c == 0 else 1
            hb = nb // n_split
            for h in range(nh):
                parts = []
                for p in range(n_split):
                    rows = slice(p * hb * blk, (p + 1) * hb * blk)
                    keys = slice(c * tq, c * tq + (p + 1) * hb * blk)
                    s = lax.dot_general(qaugs[h][rows], kaug[h, keys, :], nt, preferred_element_type=F32)
                    s = jnp.concatenate(
                        [jnp.concatenate(
                            [jnp.where(causal, s[a * blk:(a + 1) * blk, e * blk:(e + 1) * blk], NEG)
                             if e == a + p * hb else s[a * blk:(a + 1) * blk, e * blk:(e + 1) * blk]
                             for e in range((p + 1) * hb)], axis=1) for a in range(hb)], axis=0)
                    st = None if states[h] is None else tuple(x[rows] for x in states[h])
                    parts.append(_online_update(st, s, v_ref[h, keys, :]))
                m, l, acc = (jnp.concatenate([pp[i] for pp in parts], axis=0) for i in range(3))
                o_ref[:, h * HEAD_DIM:(h + 1) * HEAD_DIM] = (acc / l).astype(BF16)


def _moba_attention(qkv, batch, seq):
    nt = seq // MOBA_TQ
    nh = MOBA_HEADS
    q0 = A_HEADS
    k0 = N_Q_HEADS + A_HEADS
    v0 = N_Q_HEADS + N_KV_HEADS + A_HEADS
    return pl.pallas_call(
        functools.partial(_moba_kernel, seq=seq),
        out_shape=jax.ShapeDtypeStruct((batch * seq, B_HEADS * HEAD_DIM), BF16),
        grid=(batch, B_HEADS // nh, nt),
        in_specs=[
            pl.BlockSpec((nh, MOBA_TQ, HEAD_DIM), lambda b, h, t: (q0 // nh + h, b * nt + t, 0)),
            pl.BlockSpec((nh, seq, HEAD_DIM), lambda b, h, t: (k0 // nh + h, b, 0)),
            pl.BlockSpec((nh, seq, HEAD_DIM), lambda b, h, t: (v0 // nh + h, b, 0)),
        ],
        out_specs=pl.BlockSpec((MOBA_TQ, nh * HEAD_DIM), lambda b, h, t: (b * nt + t, h)),
        scratch_shapes=[pltpu.VMEM((nh, seq, 2 * HEAD_DIM), BF16), pltpu.VMEM((nh, HEAD_DIM, HEAD_DIM), F32)],
        compiler_params=_cparams(("arbitrary",) * 3),
        name="moba_attn",
    )(qkv, qkv, qkv)


def _merge_kernel(h_ref, oa_ref, ob_ref, oc_ref, g0_ref, g1_ref, g2_ref, b0_ref, b1_ref, b2_ref,
                  wa_ref, wb_ref, wc_ref, *refs):
    n_side = len(refs) // 2
    o_ref = refs[n_side]
    _cast_sides(refs[:n_side], refs[n_side + 1:])
    x = h_ref[...]
    acc = None
    for g_ref, b_ref, br_ref, w_ref in ((g0_ref, b0_ref, oa_ref, wa_ref),
                                        (g1_ref, b1_ref, ob_ref, wb_ref),
                                        (g2_ref, b2_ref, oc_ref, wc_ref)):
        gate = jax.nn.sigmoid(jnp.dot(x, g_ref[...], preferred_element_type=F32) + b_ref[...])
        term = gate * jnp.dot(br_ref[...], w_ref[...], preferred_element_type=F32)
        acc = term if acc is None else acc + term
    o_ref[...] = acc.astype(BF16)


def _merge(hb, out_a, out_b, out_c, w_gate_b, b_gate, wa_b, wb_b, wc_b, layer, sides=()):
    t, d = hb.shape
    tm, tn = 1024, 512
    nj = d // tn
    ni = t // tm
    side_in, side_out, side_shape = _side_cast_specs(sides, nj * ni, lambda j, i: j * ni + i)
    row = lambda w: pl.BlockSpec((tm, w), lambda j, i: (i, 0))
    gate_w = lambda x: pl.BlockSpec((d, tn), lambda j, i, x=x: (0, x * nj + j))
    gate_b = lambda x: pl.BlockSpec((None, 1, tn), lambda j, i, x=x: (layer, 0, x * nj + j))
    br_w = lambda k: pl.BlockSpec((None, k, tn), lambda j, i: (layer, 0, j))
    bg = b_gate.reshape(b_gate.shape[0], 1, -1)
    return pl.pallas_call(
        _merge_kernel,
        out_shape=(jax.ShapeDtypeStruct((t, d), BF16), *side_shape),
        grid=(nj, ni),
        in_specs=[row(d), row(out_a.shape[1]), row(out_b.shape[1]), row(out_c.shape[1]),
                  gate_w(0), gate_w(1), gate_w(2), gate_b(0), gate_b(1), gate_b(2),
                  br_w(wa_b.shape[1]), br_w(wb_b.shape[1]), br_w(wc_b.shape[1]), *side_in],
        out_specs=(pl.BlockSpec((tm, tn), lambda j, i: (i, j)), *side_out),
        compiler_params=_cparams(("arbitrary",) * 2),
        name="branch_merge",
    )(hb, out_a, out_b, out_c, w_gate_b, w_gate_b, w_gate_b, bg, bg, bg, wa_b, wb_b, wc_b,
      *[w for w, _, _ in sides])


def _layer_norm_rows(y, g, b):
    mu = jnp.mean(y, axis=-1, keepdims=True)
    yc = y - mu
    var = jnp.mean(yc * yc, axis=-1, keepdims=True)
    return yc * lax.rsqrt(var + LN_EPS) * g + b


def _route(lt):
    tm = lt.shape[1]
    gl = lt[0:N_GROUPS]
    gmax = jnp.max(gl, axis=0, keepdims=True)
    ge = jnp.exp(gl - gmax)
    gp = ge / jnp.sum(ge, axis=0, keepdims=True)
    g_val = jnp.max(gp, axis=0, keepdims=True)
    row = lax.broadcasted_iota(jnp.int32, (N_GROUPS, tm), 0)
    g_sel = jnp.min(jnp.where(gp == g_val, row, N_GROUPS), axis=0, keepdims=True)
    el = jnp.zeros((EXPERTS_PER_GROUP, tm), F32)
    for g in range(N_GROUPS):
        lo = 8 + g * EXPERTS_PER_GROUP
        el = jnp.where(g_sel == g, lt[lo:lo + EXPERTS_PER_GROUP], el)
    v0 = jnp.max(el, axis=0, keepdims=True)
    i0 = jnp.min(jnp.where(el == v0, row, EXPERTS_PER_GROUP), axis=0, keepdims=True)
    el1 = jnp.where(row == i0, -jnp.inf, el)
    v1 = jnp.max(el1, axis=0, keepdims=True)
    i1 = jnp.min(jnp.where(el1 == v1, row, EXPERTS_PER_GROUP), axis=0, keepdims=True)
    e1 = jnp.exp(v1 - v0)
    den = 1.0 + e1
    w0 = 1.0 / den * g_val
    w1 = e1 / den * g_val
    a = jnp.minimum(i0, i1)
    b = jnp.maximum(i0, i1)
    swap = (a == 2) & (b == 3)
    x = jnp.where(swap, b, a)
    y = jnp.where(swap, a, b)
    rank = jnp.where(a == 0, b - 1, jnp.where(a == 1, jnp.where(b == 3, 3, 4), 5))
    cls = (g_sel * PAIRS_PER_GROUP + rank).astype(F32)
    wx = jnp.where(i0 == x, w0, w1)
    wy = jnp.where(i0 == y, w0, w1)
    return cls, wx, wy


def _outproj_ln_route_kernel(m_ref, w_ref, h_ref, g_ref, b_ref, wr_ref, hf_ref, rt_ref, mix_buf,
                             wrs_ref, *, alpha):
    @pl.when(pl.program_id(0) == 0)
    def _():
        mix_buf[...] = jnp.zeros_like(mix_buf)
        wr = wr_ref[...]
        wr_hi = wr.astype(BF16)
        wrs_ref[:, :HEAD_DIM] = wr_hi
        wrs_ref[:, HEAD_DIM:] = (wr - wr_hi.astype(F32)).astype(BF16)

    prev = mix_buf[...]
    mix_buf[...] = jnp.dot(m_ref[...], w_ref[...], preferred_element_type=F32)
    h1 = _layer_norm_rows(alpha * h_ref[...] + prev, g_ref[...], b_ref[...])
    h_hi = h1.astype(BF16)
    h_lo = (h1 - h_hi.astype(F32)).astype(BF16)
    both = jnp.dot(h_hi, wrs_ref[...], preferred_element_type=F32)
    logits = (both[:, :HEAD_DIM] + both[:, HEAD_DIM:]
              + jnp.dot(h_lo, wrs_ref[:, :HEAD_DIM], preferred_element_type=F32))
    cls, wx, wy = _route(logits.T)
    tm = logits.shape[0]
    rt_ref[...] = jnp.concatenate([cls, wx, wy, jnp.zeros((5, tm), F32)], axis=0)
    ext = jnp.concatenate([wx, wy, jnp.zeros((HEAD_DIM - 2, tm), F32)], axis=0).T
    d = h1.shape[1]
    hf_ref[:, :d] = h1
    hf_ref[:, d:] = ext


def _outproj_ln_route(merged, w_out_b, h, ln_g, ln_b, w_router, alpha):
    t, d = h.shape
    tm = 512
    n = t // tm
    cur = lambda i: (jnp.minimum(i, n - 1), 0)
    prv = lambda i: (jnp.maximum(i - 1, 0), 0)
    return pl.pallas_call(
        functools.partial(_outproj_ln_route_kernel, alpha=alpha),
        out_shape=(jax.ShapeDtypeStruct((t, d + HEAD_DIM), F32), jax.ShapeDtypeStruct((8, t), F32)),
        grid=(n + 1,),
        in_specs=[pl.BlockSpec((tm, d), cur),
                  pl.BlockSpec((d, d), lambda i: (0, 0)),
                  pl.BlockSpec((tm, d), prv),
                  pl.BlockSpec((1, d), lambda i: (0, 0)),
                  pl.BlockSpec((1, d), lambda i: (0, 0)),
                  pl.BlockSpec((d, HEAD_DIM), lambda i: (0, 0))],
        out_specs=(pl.BlockSpec((tm, d + HEAD_DIM), prv),
                   pl.BlockSpec((8, tm), lambda i: (0, jnp.maximum(i - 1, 0)))),
        scratch_shapes=[pltpu.VMEM((tm, d), F32), pltpu.VMEM((d, 2 * HEAD_DIM), BF16)],
        compiler_params=_cparams(("arbitrary",)),
        name="outproj_ln1_route",
    )(merged, w_out_b, h, ln_g.reshape(1, d), ln_b.reshape(1, d), w_router)


MOE_TILE = 256
GATHER_TILE = 512
DMA_UNROLL = 8
MLP_CHUNKS = 4


def _dispatch_plan(rt, t):
    cls = rt[0].astype(jnp.int32)
    onehot = (cls[:, None] == jnp.arange(N_CLASSES, dtype=jnp.int32)[None, :]).astype(jnp.int32)
    csum = jnp.cumsum(onehot, axis=0)
    counts = csum[-1]
    rank = jnp.take_along_axis(csum, cls[:, None], axis=1)[:, 0] - 1
    padded = (counts + MOE_TILE - 1) // MOE_TILE * MOE_TILE
    ends = jnp.cumsum(padded)
    pos = ((ends - padded)[cls] + rank).astype(jnp.int32)
    p_rows = t + N_CLASSES * MOE_TILE
    tok = jnp.arange(t, dtype=jnp.int32)
    src = jnp.zeros((p_rows + MOE_TILE,), jnp.int32).at[pos].set(tok)
    n_tiles = p_rows // MOE_TILE
    n_used = (ends[-1] // MOE_TILE).astype(jnp.int32)
    tile_idx = jnp.arange(n_tiles, dtype=jnp.int32)
    tile_start = jnp.minimum(tile_idx, n_used - 1) * MOE_TILE
    tile_c = jnp.sum((ends[None, :] <= tile_start[:, None]).astype(jnp.int32), axis=1)
    tile_c = jnp.minimum(tile_c, N_CLASSES - 1)
    group = tile_c // PAIRS_PER_GROUP
    pair = tile_c % PAIRS_PER_GROUP
    tile_x = group * EXPERTS_PER_GROUP + jnp.asarray(PAIR_X, jnp.int32)[pair]
    tile_y = group * EXPERTS_PER_GROUP + jnp.asarray(PAIR_Y, jnp.int32)[pair]
    return src, pos, tile_x.astype(jnp.int32), tile_y.astype(jnp.int32), n_used.reshape(1)


def _expert_mlp_kernel(src_ref, tx_ref, ty_ref, nu_ref, h_hbm, wgx_ref, wux_ref, wdx_ref,
                       wgy_ref, wuy_ref, wdy_ref, y_ref, xbuf, sem):
    i = pl.program_id(0)
    n_used = nu_ref[0]
    slot = i % 2
    d = y_ref.shape[1]

    def issue_rows(tile, dst_slot, lo, hi):
        for j in range(lo, hi):
            row = src_ref[tile * MOE_TILE + j]
            pltpu.make_async_copy(h_hbm.at[pl.ds(row, 1)], xbuf.at[dst_slot, pl.ds(j, 1)],
                                  sem.at[dst_slot]).start(priority=j % 2)

    def wait_tile(s):
        pltpu.make_async_copy(h_hbm.at[pl.ds(0, MOE_TILE)], xbuf.at[s], sem.at[s]).wait()

    @pl.when(i == 0)
    def _():
        def first(c, carry):
            base = pl.multiple_of(c * DMA_UNROLL, DMA_UNROLL)
            for j in range(DMA_UNROLL):
                row = src_ref[base + j]
                pltpu.make_async_copy(h_hbm.at[pl.ds(row, 1)], xbuf.at[0, pl.ds(base + j, 1)],
                                      sem.at[0]).start()
            return carry
        lax.fori_loop(0, MOE_TILE // DMA_UNROLL, first, 0)

    @pl.when(i < n_used)
    def _():
        wait_tile(slot)
        nxt = 1 - slot
        de = wgx_ref.shape[2]
        cw = de // MLP_CHUNKS
        n_pieces = 4 * MLP_CHUNKS
        bounds = [MOE_TILE * k // n_pieces for k in range(n_pieces + 1)]
        piece = iter(range(n_pieces))

        def issue_next():
            k = next(piece)
            issue_rows(i + 1, nxt, bounds[k], bounds[k + 1])

        out = None
        for wg_ref, wu_ref, wd_ref, lane in ((wgx_ref, wux_ref, wdx_ref, d), (wgy_ref, wuy_ref, wdy_ref, d + 1)):
            hid = []
            for c in range(MLP_CHUNKS):
                cols = slice(c * cw, (c + 1) * cw)
                issue_next()
                a = jnp.dot(xbuf[slot, :, :d].astype(BF16), wg_ref[0, :, cols], preferred_element_type=F32)
                issue_next()
                u = jnp.dot(xbuf[slot, :, :d].astype(BF16), wu_ref[0, :, cols], preferred_element_type=F32)
                hid.append((jax.nn.silu(a) * u).astype(BF16))
            y = None
            for c in range(MLP_CHUNKS):
                part = jnp.dot(hid[c], wd_ref[0, c * cw:(c + 1) * cw, :], preferred_element_type=F32)
                y = part if y is None else y + part
            y = y * xbuf[slot, :, lane:lane + 1]
            out = y if out is None else out + y
        y_ref[...] = out

    @pl.when(i >= n_used)
    def _():
        y_ref[...] = jnp.zeros_like(y_ref)

    @pl.when(i == n_used)
    def _():
        wait_tile(slot)


def _expert_mlp(h1x, src, tile_x, tile_y, n_used, wg_b, wu_b, wd_b, layer):
    dx = h1x.shape[1]
    d = dx - HEAD_DIM
    p = src.shape[0] - MOE_TILE
    de = wg_b.shape[2]
    ex = lambda i, src, tx, ty, nu: (layer * N_EXPERTS + tx[i], 0, 0)
    ey = lambda i, src, tx, ty, nu: (layer * N_EXPERTS + ty[i], 0, 0)
    return pl.pallas_call(
        _expert_mlp_kernel,
        out_shape=jax.ShapeDtypeStruct((p, d), F32),
        grid_spec=pltpu.PrefetchScalarGridSpec(
            num_scalar_prefetch=4, grid=(p // MOE_TILE,),
            in_specs=[pl.BlockSpec(memory_space=pl.ANY),
                      pl.BlockSpec((1, d, de), ex),
                      pl.BlockSpec((1, d, de), ex),
                      pl.BlockSpec((1, de, d), ex),
                      pl.BlockSpec((1, d, de), ey),
                      pl.BlockSpec((1, d, de), ey),
                      pl.BlockSpec((1, de, d), ey)],
            out_specs=pl.BlockSpec((MOE_TILE, d), lambda i, src, tx, ty, nu: (i, 0)),
            scratch_shapes=[pltpu.VMEM((2, MOE_TILE, dx), F32), pltpu.SemaphoreType.DMA((2,))]),
        compiler_params=_cparams(("arbitrary",), vmem=MLP_VMEM_LIMIT),
        name="moe_expert_mlp",
    )(src, tile_x, tile_y, n_used, h1x, wg_b, wu_b, wd_b, wg_b, wu_b, wd_b)


def _combine_ln_kernel(pos_ref, y_hbm, h_ref, g_ref, b_ref, *refs, alpha):
    out_refs, (buf, sem) = refs[:-2], refs[-2:]
    i = pl.program_id(0)
    tm = h_ref.shape[0]

    def issue_tile(tile, slot):
        def issue(c, carry):
            base = pl.multiple_of(c * DMA_UNROLL, DMA_UNROLL)
            dst = buf.at[slot, pl.ds(base, DMA_UNROLL)]
            for j in range(DMA_UNROLL):
                row = pos_ref[tile * tm + base + j]
                pltpu.make_async_copy(y_hbm.at[pl.ds(row, 1)], dst.at[pl.ds(j, 1)],
                                      sem.at[slot]).start(priority=j % 2)
            return carry

        lax.fori_loop(0, tm // DMA_UNROLL, issue, 0)

    @pl.when(i == 0)
    def _():
        issue_tile(0, 0)

    @pl.when(i + 1 < pl.num_programs(0))
    def _():
        issue_tile(i + 1, (i + 1) % 2)

    slot = i % 2
    pltpu.make_async_copy(y_hbm.at[pl.ds(0, tm)], buf.at[slot], sem.at[slot]).wait()
    out = _layer_norm_rows(alpha * h_ref[...] + buf[slot], g_ref[...], b_ref[...])
    for o_ref in out_refs:
        o_ref[...] = out.astype(o_ref.dtype)


def _combine_ln(pos, y_sorted, h1x, ln_g, ln_b, alpha, out_dtypes):
    t = h1x.shape[0]
    d = y_sorted.shape[1]
    tm = GATHER_TILE
    return pl.pallas_call(
        functools.partial(_combine_ln_kernel, alpha=alpha),
        out_shape=tuple(jax.ShapeDtypeStruct((t, d), dt) for dt in out_dtypes),
        grid_spec=pltpu.PrefetchScalarGridSpec(
            num_scalar_prefetch=1, grid=(t // tm,),
            in_specs=[pl.BlockSpec(memory_space=pl.ANY),
                      pl.BlockSpec((tm, d), lambda i, pos: (i, 0)),
                      pl.BlockSpec((1, d), lambda i, pos: (0, 0)),
                      pl.BlockSpec((1, d), lambda i, pos: (0, 0))],
            out_specs=tuple(pl.BlockSpec((tm, d), lambda i, pos: (i, 0)) for _ in out_dtypes),
            scratch_shapes=[pltpu.VMEM((2, tm, d), F32), pltpu.SemaphoreType.DMA((2,))]),
        compiler_params=_cparams(("arbitrary",)),
        name="moe_combine_ln2",
    )(pos, y_sorted, h1x, ln_g.reshape(1, d), ln_b.reshape(1, d))


def _moe(h1x, rt, wg_b, wu_b, wd_b, layer, ln_g, ln_b, alpha, out_dtypes):
    t = h1x.shape[0]
    src, pos, tile_x, tile_y, n_used = _dispatch_plan(rt, t)
    y_sorted = _expert_mlp(h1x, src, tile_x, tile_y, n_used, wg_b, wu_b, wd_b, layer)
    return _combine_ln(pos, y_sorted, h1x, ln_g, ln_b, alpha, out_dtypes)


def kernel(x, positions, w_in, w_gate, b_gate, w_branch_a, w_branch_b, w_branch_c, w_out, sinks,
           ln1_g, ln1_b, w_router_group, w_router_expert, w_exp_gate, w_exp_up, w_exp_down,
           ln2_g, ln2_b):
    batch, seq, d = x.shape
    depth = w_in.shape[0]
    t = batch * seq
    alpha = (2 * depth) ** 0.25
    hf = x.reshape(t, d)
    cc, ss, hb = _rope_tables(positions, hf)
    wa_b, wb_b, wc_b = w_branch_a.astype(BF16), w_branch_b.astype(BF16), w_branch_c.astype(BF16)
    w_in_b = w_in[0].astype(BF16)
    w_in2, w_gate2, w_out2 = (w.reshape(depth * d, w.shape[-1]) for w in (w_in, w_gate, w_out))
    de = w_exp_gate.shape[-1]
    w_exp = ((w_exp_gate.reshape(depth * N_EXPERTS * d, de), N_EXPERTS * d),
             (w_exp_up.reshape(depth * N_EXPERTS * d, de), N_EXPERTS * d),
             (w_exp_down.reshape(depth * N_EXPERTS * de, d), N_EXPERTS * de))
    for l in range(depth):
        last = l == depth - 1
        qkv, w_gate_b, w_out_b = _qkv_proj(hb, w_in_b, cc, ss, sides=((w_gate2, d, l), (w_out2, d, l)))
        out_a = _dilated_attention(qkv, batch, seq)
        out_b = _moba_attention(qkv, batch, seq)
        out_c = _swa_attention(qkv, sinks[l], batch, seq)
        sides = [(w, rows, l) for w, rows in w_exp] + ([] if last else [(w_in2, d, l + 1)])
        merged, wg_b, wu_b, wd_b, *nxt = _merge(hb, out_a, out_b, out_c, w_gate_b, b_gate, wa_b, wb_b, wc_b,
                                                l, sides)
        if not last:
            w_in_b = nxt[0]
        wg_b, wu_b = wg_b.reshape(N_EXPERTS, d, de), wu_b.reshape(N_EXPERTS, d, de)
        wd_b = wd_b.reshape(N_EXPERTS, de, d)
        w_router = jnp.zeros((d, HEAD_DIM), F32)
        w_router = w_router.at[:, 0:N_GROUPS].set(w_router_group[l])
        w_router = w_router.at[:, 8:8 + N_EXPERTS].set(w_router_expert[l])
        h1f, rt = _outproj_ln_route(merged, w_out_b, hf, ln1_g[l], ln1_b[l], w_router, alpha)
        outs = _moe(h1f, rt, wg_b, wu_b, wd_b, 0, ln2_g[l], ln2_b[l], alpha,
                    (F32,) if last else (F32, BF16))
        hf, hb = (outs[0], None) if last else outs
    return hf.reshape(batch, seq, d)
```

```python
import functools

import jax
import jax.numpy as jnp
from jax import lax
from jax.experimental import pallas as pl
from jax.experimental.pallas import tpu as pltpu

F32 = jnp.float32
BF16 = jnp.bfloat16

HEAD_DIM = 128
ATTN_SCALE = HEAD_DIM ** -0.5
ROPE_THETA = 10000.0
A_PATTERNS = ((128, 1), (512, 4), (2048, 16))
A_HEADS_PER_GROUP = 2
A_HEADS = A_HEADS_PER_GROUP * len(A_PATTERNS)
B_HEADS = 6
MOBA_BLOCK = 256
MOBA_TOPK = 3
C_Q_HEADS = 8
C_KV_HEADS = 2
C_GROUP = C_Q_HEADS // C_KV_HEADS
C_WINDOW = 128
BAND = 128
N_Q_HEADS = A_HEADS + B_HEADS + C_Q_HEADS
N_KV_HEADS = A_HEADS + B_HEADS + C_KV_HEADS
N_GROUPS = 4
EXPERTS_PER_GROUP = 4
N_EXPERTS = N_GROUPS * EXPERTS_PER_GROUP
PAIRS_PER_GROUP = EXPERTS_PER_GROUP * (EXPERTS_PER_GROUP - 1) // 2
N_CLASSES = N_GROUPS * PAIRS_PER_GROUP
PAIR_X = (0, 0, 0, 1, 1, 3)
PAIR_Y = (1, 2, 3, 3, 2, 2)
ROUTER_EXPERT_COL = 8
LN_EPS = 1e-5
NEG = -1e30

V7X_SUBLANES = 8
V7X_VMEM_BYTES = 64 * 1024 * 1024
VMEM_LIMIT = V7X_VMEM_BYTES - 8 * 1024 * 1024
MLP_VMEM_LIMIT = V7X_VMEM_BYTES - 2 * 1024 * 1024


def _cparams(sem, vmem=VMEM_LIMIT):
    return pltpu.CompilerParams(dimension_semantics=sem, vmem_limit_bytes=vmem)


ROPE_SLOT_Q, ROPE_SLOT_K, ROPE_SLOT_V, ROPE_SLOT_Q_BASE2 = 0, 1, 2, 3
ROPE_SLOTS = 4
LOG2_E = 1.4426950408889634


def _rope_table_kernel(pos_ref, inv_ref, x_ref, cc_ref, ss_ref, xb_ref):
    xb_ref[...] = x_ref[...].astype(BF16)
    ang = pos_ref[...].astype(F32) * inv_ref[...]
    lane = lax.broadcasted_iota(jnp.int32, ang.shape, 1)
    s = jnp.sin(ang)
    c = jnp.cos(ang)
    s = jnp.where(lane < HEAD_DIM // 2, -s, s)
    cc_ref[ROPE_SLOT_Q] = c * ATTN_SCALE
    ss_ref[ROPE_SLOT_Q] = s * ATTN_SCALE
    cc_ref[ROPE_SLOT_K] = c
    ss_ref[ROPE_SLOT_K] = s
    cc_ref[ROPE_SLOT_V] = jnp.ones_like(c)
    ss_ref[ROPE_SLOT_V] = jnp.zeros_like(s)
    cc_ref[ROPE_SLOT_Q_BASE2] = c * (ATTN_SCALE * LOG2_E)
    ss_ref[ROPE_SLOT_Q_BASE2] = s * (ATTN_SCALE * LOG2_E)


def _rope_tables(positions, x):
    t, d = x.shape
    tm = 512
    inv = ROPE_THETA ** (-jnp.arange(0, HEAD_DIM, 2, dtype=F32) / HEAD_DIM)
    inv2 = jnp.concatenate([inv, inv]).reshape(1, HEAD_DIM)
    pos = positions.reshape(t, 1)
    table = jax.ShapeDtypeStruct((ROPE_SLOTS, t, HEAD_DIM), F32)
    table_spec = pl.BlockSpec((ROPE_SLOTS, tm, HEAD_DIM), lambda i: (0, i, 0))
    return pl.pallas_call(
        _rope_table_kernel,
        out_shape=(table, table, jax.ShapeDtypeStruct((t, d), BF16)),
        grid=(t // tm,),
        in_specs=[pl.BlockSpec((tm, 1), lambda i: (i, 0)),
                  pl.BlockSpec((1, HEAD_DIM), lambda i: (0, 0)),
                  pl.BlockSpec((tm, d), lambda i: (i, 0))],
        out_specs=(table_spec, table_spec, pl.BlockSpec((tm, d), lambda i: (i, 0))),
        compiler_params=_cparams(("arbitrary",)),
        name="rope_tables",
    )(pos, inv2, x)


def _side_cast_specs(sides, n_steps, step_of):
    ins, outs, shapes = [], [], []
    for w, rows, layer in sides:
        assert rows % n_steps == 0, (rows, n_steps)
        sr = rows // n_steps
        ins.append(pl.BlockSpec((sr, w.shape[1]), lambda *g, layer=layer: (layer * n_steps + step_of(*g), 0)))
        outs.append(pl.BlockSpec((sr, w.shape[1]), lambda *g: (step_of(*g), 0)))
        shapes.append(jax.ShapeDtypeStruct((rows, w.shape[1]), BF16))
    return ins, outs, shapes


def _cast_sides(in_refs, out_refs):
    for i_ref, o_ref in zip(in_refs, out_refs, strict=True):
        o_ref[...] = i_ref[...].astype(BF16)


def _qkv_kernel(x_ref, w_ref, cc_ref, ss_ref, *refs, heads_per_tile):
    n_side = len(refs) // 2
    o_ref = refs[n_side]
    _cast_sides(refs[:n_side], refs[n_side + 1:])
    j = pl.program_id(1)
    acc = jnp.dot(x_ref[...], w_ref[...], preferred_element_type=F32)
    for hh in range(heads_per_tile):
        head = j * heads_per_tile + hh
        in_b = (head >= A_HEADS) & (head < A_HEADS + B_HEADS)
        slot = ((head >= N_Q_HEADS).astype(jnp.int32) + (head >= N_Q_HEADS + N_KV_HEADS).astype(jnp.int32)
                + ROPE_SLOT_Q_BASE2 * in_b.astype(jnp.int32))
        t = acc[:, hh * HEAD_DIM:(hh + 1) * HEAD_DIM]
        r = t * cc_ref[slot] + pltpu.roll(t, HEAD_DIM // 2, 1) * ss_ref[slot]
        o_ref[hh] = r.astype(BF16)


def _qkv_proj(hb, w_in_b, cc, ss, sides=()):
    t, d = hb.shape
    n = w_in_b.shape[1]
    tm, tn = 1024, 1536
    hpt = tn // HEAD_DIM
    ni, nj = t // tm, n // tn
    side_in, side_out, side_shape = _side_cast_specs(sides, ni * nj, lambda i, j: i * nj + j)
    return pl.pallas_call(
        functools.partial(_qkv_kernel, heads_per_tile=hpt),
        out_shape=(jax.ShapeDtypeStruct((n // HEAD_DIM, t, HEAD_DIM), BF16), *side_shape),
        grid=(ni, nj),
        in_specs=[pl.BlockSpec((tm, d), lambda i, j: (i, 0)),
                  pl.BlockSpec((d, tn), lambda i, j: (0, j)),
                  pl.BlockSpec((ROPE_SLOTS, tm, HEAD_DIM), lambda i, j: (0, i, 0)),
                  pl.BlockSpec((ROPE_SLOTS, tm, HEAD_DIM), lambda i, j: (0, i, 0)), *side_in],
        out_specs=(pl.BlockSpec((hpt, tm, HEAD_DIM), lambda i, j: (j, i, 0)), *side_out),
        compiler_params=_cparams(("arbitrary", "arbitrary")),
        name="qkv_rope",
    )(hb, w_in_b, cc, ss, *[w for w, _, _ in sides])


def _band_bias(max_dist):
    qi = lax.broadcasted_iota(jnp.int32, (BAND, 2 * BAND), 0)
    kj = lax.broadcasted_iota(jnp.int32, (BAND, 2 * BAND), 1)
    dist = qi + BAND - kj
    valid = (dist >= 0) & (dist <= max_dist)
    bias = jnp.where(valid, 0.0, NEG).astype(F32)
    bias_first = jnp.where(valid & (kj >= BAND), 0.0, NEG).astype(F32)
    return bias, bias_first


def _band_block(qb, kwin, vwin, bias):
    g = qb.shape[0] // BAND
    s = lax.dot_general(qb, kwin, (((1,), (1,)), ((), ())), preferred_element_type=F32)
    s = s.reshape(g, BAND, 2 * BAND) + bias[None]
    m = jnp.max(s, axis=-1, keepdims=True)
    p = jnp.exp(s - m)
    l = jnp.sum(p, axis=-1, keepdims=True)
    o = jnp.dot(p.reshape(g * BAND, 2 * BAND).astype(BF16), vwin, preferred_element_type=F32)
    o = o.reshape(g, BAND, HEAD_DIM) / l
    return o, m + jnp.log(l)


def _swa_kernel(sink_ref, q_ref, k_ref, kh_ref, v_ref, vh_ref, o_ref, *, tq):
    kv = pl.program_id(1)
    t = pl.program_id(2)
    bias, bias_first = _band_bias(C_WINDOW - 1)
    for blk in range(tq // BAND):
        qb = q_ref[:, blk * BAND:(blk + 1) * BAND, :].reshape(C_GROUP * BAND, HEAD_DIM)
        if blk == 0:
            kwin = jnp.concatenate([kh_ref[0], k_ref[0, :BAND]], axis=0)
            vwin = jnp.concatenate([vh_ref[0], v_ref[0, :BAND]], axis=0)
            b = jnp.where(t == 0, bias_first, bias)
        else:
            kwin = k_ref[0, (blk - 1) * BAND:(blk + 1) * BAND]
            vwin = v_ref[0, (blk - 1) * BAND:(blk + 1) * BAND]
            b = bias
        o, lse = _band_block(qb, kwin, vwin, b)
        for g in range(C_GROUP):
            keep = jax.nn.sigmoid(lse[g] - sink_ref[kv * C_GROUP + g])
            o_ref[blk * BAND:(blk + 1) * BAND, g * HEAD_DIM:(g + 1) * HEAD_DIM] = (o[g] * keep).astype(BF16)


def _swa_attention(qkv, sinks, batch, seq):
    tq = 2048
    nt = seq // tq
    q0 = A_HEADS + B_HEADS
    k0 = N_Q_HEADS + A_HEADS + B_HEADS
    v0 = N_Q_HEADS + N_KV_HEADS + A_HEADS + B_HEADS
    rb = tq // BAND

    def halo(b, kv, t):
        return jnp.maximum(b * (seq // BAND) + t * rb - 1, 0)

    return pl.pallas_call(
        functools.partial(_swa_kernel, tq=tq),
        out_shape=jax.ShapeDtypeStruct((batch * seq, C_Q_HEADS * HEAD_DIM), BF16),
        grid=(batch, C_KV_HEADS, nt),
        in_specs=[
            pl.BlockSpec(memory_space=pltpu.SMEM),
            pl.BlockSpec((C_GROUP, tq, HEAD_DIM), lambda b, kv, t: (q0 // C_GROUP + kv, b * nt + t, 0)),
            pl.BlockSpec((1, tq, HEAD_DIM), lambda b, kv, t: (k0 + kv, b * nt + t, 0)),
            pl.BlockSpec((1, BAND, HEAD_DIM), lambda b, kv, t: (k0 + kv, halo(b, kv, t), 0)),
            pl.BlockSpec((1, tq, HEAD_DIM), lambda b, kv, t: (v0 + kv, b * nt + t, 0)),
            pl.BlockSpec((1, BAND, HEAD_DIM), lambda b, kv, t: (v0 + kv, halo(b, kv, t), 0)),
        ],
        out_specs=pl.BlockSpec((tq, C_GROUP * HEAD_DIM), lambda b, kv, t: (b * nt + t, kv)),
        compiler_params=_cparams(("arbitrary",) * 3),
        name="swa_sink",
    )(sinks, qkv, qkv, qkv, qkv, qkv)


def _dilated_kernel(*refs, seq):
    ng = len(A_PATTERNS)
    in_refs = refs[:3 * ng]
    o_ref = refs[3 * ng]
    qf, kf, vf, og, lg = refs[3 * ng + 1:]
    for g, (window, dil) in enumerate(A_PATTERNS):
        q_ref, k_ref, v_ref = in_refs[3 * g:3 * g + 3]
        qf[...] = q_ref[0].astype(F32)
        kf[...] = k_ref[0].astype(F32)
        vf[...] = v_ref[0].astype(F32)
        nblk = seq // dil // BAND
        bias, bias_first = _band_bias(window // dil)

        for r in range(dil):
            for n in range(nblk):
                own = pl.ds(r + dil * BAND * n, BAND, stride=dil)
                prev = pl.ds(r + dil * BAND * max(n - 1, 0), BAND, stride=dil)
                qb = qf[own, :].astype(BF16)
                kwin = jnp.concatenate([kf[prev, :], kf[own, :]], axis=0).astype(BF16)
                vwin = jnp.concatenate([vf[prev, :], vf[own, :]], axis=0).astype(BF16)
                o, lse = _band_block(qb, kwin, vwin, bias_first if n == 0 else bias)
                og[g, own, :] = o[0]
                lg[g, own, :] = jnp.broadcast_to(lse[0], (BAND, HEAD_DIM))

    rows = 256

    def combine(c, carry):
        sl = pl.ds(pl.multiple_of(c * rows, rows), rows)
        ls = [lg[g, sl, :] for g in range(ng)]
        m = functools.reduce(jnp.maximum, ls)
        es = [jnp.exp(l - m) for l in ls]
        num = functools.reduce(jnp.add, [e * og[g, sl, :] for g, e in enumerate(es)])
        o_ref[sl, :] = (num / functools.reduce(jnp.add, es)).astype(BF16)
        return carry

    lax.fori_loop(0, seq // rows, combine, 0)


def _dilated_attention(qkv, batch, seq):
    ng = len(A_PATTERNS)
    k0 = N_Q_HEADS
    v0 = N_Q_HEADS + N_KV_HEADS
    in_specs = []
    for g in range(ng):
        for base in (0, k0, v0):
            in_specs.append(pl.BlockSpec(
                (1, seq, HEAD_DIM),
                lambda b, j, h=base + g * A_HEADS_PER_GROUP: (h + j, b, 0)))
    return pl.pallas_call(
        functools.partial(_dilated_kernel, seq=seq),
        out_shape=jax.ShapeDtypeStruct((batch * seq, A_HEADS_PER_GROUP * HEAD_DIM), BF16),
        grid=(batch, A_HEADS_PER_GROUP),
        in_specs=in_specs,
        out_specs=pl.BlockSpec((seq, HEAD_DIM), lambda b, j: (b, j)),
        scratch_shapes=[pltpu.VMEM((seq, HEAD_DIM), F32)] * 3
        + [pltpu.VMEM((ng, seq, HEAD_DIM), F32)] * 2,
        compiler_params=_cparams(("arbitrary",) * 2),
        name="dilated_attn",
    )(*([qkv] * (3 * ng)))


MOBA_TQ = 1024
MOBA_HEADS = 2


def _online_update(state, s, v):
    mc = jnp.max(s, axis=1, keepdims=True)
    if state is None:
        p = jnp.exp2(s - mc)
        return mc, jnp.sum(p, axis=1, keepdims=True), jnp.dot(p.astype(BF16), v, preferred_element_type=F32)
    m, l, acc = state
    m_new = jnp.maximum(m, mc)
    alpha = jnp.exp2(m - m_new)
    p = jnp.exp2(s - m_new)
    l_new = alpha * l + jnp.sum(p, axis=1, keepdims=True)
    acc_new = alpha * acc + jnp.dot(p.astype(BF16), v, preferred_element_type=F32)
    return m_new, l_new, acc_new


def _moba_kernel(q_ref, k_ref, v_ref, o_ref, kaug, kmean, *, seq):
    t = pl.program_id(2)
    nblk = seq // MOBA_BLOCK
    blk = MOBA_BLOCK
    tq = MOBA_TQ
    nb = tq // blk
    nh = MOBA_HEADS

    @pl.when(t == 0)
    def _():
        row = lax.broadcasted_iota(jnp.int32, (seq, HEAD_DIM), 0)
        lane = lax.broadcasted_iota(jnp.int32, (seq, HEAD_DIM), 1)
        onehot = jnp.where(row // blk == lane, 1.0, 0.0).astype(BF16)
        for h in range(nh):
            kaug[h, :, :HEAD_DIM] = k_ref[h]
            kaug[h, :, HEAD_DIM:] = onehot
            kmean[h] = jnp.zeros((HEAD_DIM, HEAD_DIM), F32)
            for n in range(nblk):
                kb = k_ref[h, n * blk:(n + 1) * blk, :].astype(F32)
                kmean[h, n:n + 1, :] = jnp.mean(kb, axis=0, keepdims=True)

    nt = (((1,), (1,)), ((), ()))
    nrow = -(-nblk // V7X_SUBLANES) * V7X_SUBLANES
    row = lax.broadcasted_iota(jnp.int32, (nrow, tq), 0).astype(F32)
    own = (t * nb + lax.broadcasted_iota(jnp.int32, (nrow, tq), 1) // blk).astype(F32)
    qaugs = []
    for h in range(nh):
        q = q_ref[h]
        km = kmean[h]
        km_hi = km.astype(BF16)
        km_lo = (km - km_hi.astype(F32)).astype(BF16)
        gate = (lax.dot_general(km_hi, q, nt, preferred_element_type=F32)
                + lax.dot_general(km_lo, q, nt, preferred_element_type=F32))
        g = jnp.where(row < own, gate[:nrow], NEG)
        sel = jnp.zeros((nrow, tq), jnp.bool_)
        for _ in range(MOBA_TOPK):
            mx = jnp.max(g, axis=0, keepdims=True)
            first = jnp.min(jnp.where(g == mx, row, float(nrow)), axis=0, keepdims=True)
            pick = row == first
            sel = sel | pick
            g = jnp.where(pick, NEG, g)
        visible = (sel & (row < own)) | (row == own)
        bias_t = jnp.concatenate([jnp.where(visible, 0.0, NEG), jnp.zeros((HEAD_DIM - nrow, tq), F32)], axis=0)
        qaugs.append(jnp.concatenate([q, bias_t.T.astype(BF16)], axis=1))

    ri = lax.broadcasted_iota(jnp.int32, (blk, blk), 0)
    ci = lax.broadcasted_iota(jnp.int32, (blk, blk), 1)
    causal = ri >= ci

    for c in range(seq // tq):
        @pl.when(t == c)
        def _(c=c):
            states = [None] * nh
            for cc in range(c):
                keys = slice(cc * tq, (cc + 1) * tq)
                for h in range(nh):
                    s = lax.dot_general(qaugs[h], kaug[h, keys, :], nt, preferred_element_type=F32)
                    states[h] = _online_update(states[h], s, v_ref[h, keys, :])
            n_split = 2 if c == 0 else 1
            hb = nb // n_split
            for h in range(nh):
                parts = []
                for p in range(n_split):
                    rows = slice(p * hb * blk, (p + 1) * hb * blk)
                    keys = slice(c * tq, c * tq + (p + 1) * hb * blk)
                    s = lax.dot_general(qaugs[h][rows], kaug[h, keys, :], nt, preferred_element_type=F32)
                    s = jnp.concatenate(
                        [jnp.concatenate(
                            [jnp.where(causal, s[a * blk:(a + 1) * blk, e * blk:(e + 1) * blk], NEG)
                             if e == a + p * hb else s[a * blk:(a + 1) * blk, e * blk:(e + 1) * blk]
                             for e in range((p + 1) * hb)], axis=1) for a in range(hb)], axis=0)
                    st = None if states[h] is None else tuple(x[rows] for x in states[h])
                    parts.append(_online_update(st, s, v_ref[h, keys, :]))
                m, l, acc = (jnp.concatenate([pp[i] for pp in parts], axis=0) for i in range(3))
                o_ref[:, h * HEAD_DIM:(h + 1) * HEAD_DIM] = (acc / l).astype(BF16)


def _moba_attention(qkv, batch, seq):
    nt = seq // MOBA_TQ
    nh = MOBA_HEADS
    q0 = A_HEADS
    k0 = N_Q_HEADS + A_HEADS
    v0 = N_Q_HEADS + N_KV_HEADS + A_HEADS
    return pl.pallas_call(
        functools.partial(_moba_kernel, seq=seq),
        out_shape=jax.ShapeDtypeStruct((batch * seq, B_HEADS * HEAD_DIM), BF16),
        grid=(batch, B_HEADS // nh, nt),
        in_specs=[
            pl.BlockSpec((nh, MOBA_TQ, HEAD_DIM), lambda b, h, t: (q0 // nh + h, b * nt + t, 0)),
            pl.BlockSpec((nh, seq, HEAD_DIM), lambda b, h, t: (k0 // nh + h, b, 0)),
            pl.BlockSpec((nh, seq, HEAD_DIM), lambda b, h, t: (v0 // nh + h, b, 0)),
        ],
        out_specs=pl.BlockSpec((MOBA_TQ, nh * HEAD_DIM), lambda b, h, t: (b * nt + t, h)),
        scratch_shapes=[pltpu.VMEM((nh, seq, 2 * HEAD_DIM), BF16), pltpu.VMEM((nh, HEAD_DIM, HEAD_DIM), F32)],
        compiler_params=_cparams(("arbitrary",) * 3),
        name="moba_attn",
    )(qkv, qkv, qkv)


def _merge_kernel(h_ref, oa_ref, ob_ref, oc_ref, g0_ref, g1_ref, g2_ref, b0_ref, b1_ref, b2_ref,
                  wa_ref, wb_ref, wc_ref, *refs):
    n_side = len(refs) // 2
    o_ref = refs[n_side]
    _cast_sides(refs[:n_side], refs[n_side + 1:])
    x = h_ref[...]
    acc = None
    for g_ref, b_ref, br_ref, w_ref in ((g0_ref, b0_ref, oa_ref, wa_ref),
                                        (g1_ref, b1_ref, ob_ref, wb_ref),
                                        (g2_ref, b2_ref, oc_ref, wc_ref)):
        gate = jax.nn.sigmoid(jnp.dot(x, g_ref[...], preferred_element_type=F32) + b_ref[...])
        term = gate * jnp.dot(br_ref[...], w_ref[...], preferred_element_type=F32)
        acc = term if acc is None else acc + term
    o_ref[...] = acc.astype(BF16)


def _merge(hb, out_a, out_b, out_c, w_gate_b, b_gate, wa_b, wb_b, wc_b, layer, sides=()):
    t, d = hb.shape
    tm, tn = 1024, 512
    nj = d // tn
    ni = t // tm
    side_in, side_out, side_shape = _side_cast_specs(sides, nj * ni, lambda j, i: j * ni + i)
    row = lambda w: pl.BlockSpec((tm, w), lambda j, i: (i, 0))
    gate_w = lambda x: pl.BlockSpec((d, tn), lambda j, i, x=x: (0, x * nj + j))
    gate_b = lambda x: pl.BlockSpec((None, 1, tn), lambda j, i, x=x: (layer, 0, x * nj + j))
    br_w = lambda k: pl.BlockSpec((None, k, tn), lambda j, i: (layer, 0, j))
    bg = b_gate.reshape(b_gate.shape[0], 1, -1)
    return pl.pallas_call(
        _merge_kernel,
        out_shape=(jax.ShapeDtypeStruct((t, d), BF16), *side_shape),
        grid=(nj, ni),
        in_specs=[row(d), row(out_a.shape[1]), row(out_b.shape[1]), row(out_c.shape[1]),
                  gate_w(0), gate_w(1), gate_w(2), gate_b(0), gate_b(1), gate_b(2),
                  br_w(wa_b.shape[1]), br_w(wb_b.shape[1]), br_w(wc_b.shape[1]), *side_in],
        out_specs=(pl.BlockSpec((tm, tn), lambda j, i: (i, j)), *side_out),
        compiler_params=_cparams(("arbitrary",) * 2),
        name="branch_merge",
    )(hb, out_a, out_b, out_c, w_gate_b, w_gate_b, w_gate_b, bg, bg, bg, wa_b, wb_b, wc_b,
      *[w for w, _, _ in sides])


def _layer_norm_rows(y, g, b):
    mu = jnp.mean(y, axis=-1, keepdims=True)
    yc = y - mu
    var = jnp.mean(yc * yc, axis=-1, keepdims=True)
    return yc * lax.rsqrt(var + LN_EPS) * g + b


def _route(lt):
    tm = lt.shape[1]
    gl = lt[0:N_GROUPS]
    gmax = jnp.max(gl, axis=0, keepdims=True)
    ge = jnp.exp(gl - gmax)
    gp = ge / jnp.sum(ge, axis=0, keepdims=True)
    g_val = jnp.max(gp, axis=0, keepdims=True)
    row = lax.broadcasted_iota(jnp.int32, (N_GROUPS, tm), 0)
    g_sel = jnp.min(jnp.where(gp == g_val, row, N_GROUPS), axis=0, keepdims=True)
    el = jnp.zeros((EXPERTS_PER_GROUP, tm), F32)
    for g in range(N_GROUPS):
        lo = ROUTER_EXPERT_COL + g * EXPERTS_PER_GROUP
        el = jnp.where(g_sel == g, lt[lo:lo + EXPERTS_PER_GROUP], el)
    v0 = jnp.max(el, axis=0, keepdims=True)
    i0 = jnp.min(jnp.where(el == v0, row, EXPERTS_PER_GROUP), axis=0, keepdims=True)
    el1 = jnp.where(row == i0, -jnp.inf, el)
    v1 = jnp.max(el1, axis=0, keepdims=True)
    i1 = jnp.min(jnp.where(el1 == v1, row, EXPERTS_PER_GROUP), axis=0, keepdims=True)
    e1 = jnp.exp(v1 - v0)
    den = 1.0 + e1
    w0 = 1.0 / den * g_val
    w1 = e1 / den * g_val
    a = jnp.minimum(i0, i1)
    b = jnp.maximum(i0, i1)
    swap = (a == 2) & (b == 3)
    x = jnp.where(swap, b, a)
    y = jnp.where(swap, a, b)
    rank = jnp.where(a == 0, b - 1, jnp.where(a == 1, jnp.where(b == 3, 3, 4), 5))
    cls = (g_sel * PAIRS_PER_GROUP + rank).astype(F32)
    wx = jnp.where(i0 == x, w0, w1)
    wy = jnp.where(i0 == y, w0, w1)
    return cls, wx, wy


def _outproj_ln_route_kernel(m_ref, w_ref, h_ref, g_ref, b_ref, wr_ref, hf_ref, rt_ref, mix_buf,
                             wrs_ref, *, alpha):
    @pl.when(pl.program_id(0) == 0)
    def _():
        mix_buf[...] = jnp.zeros_like(mix_buf)
        wr = wr_ref[...]
        wr_hi = wr.astype(BF16)
        wrs_ref[:, :HEAD_DIM] = wr_hi
        wrs_ref[:, HEAD_DIM:] = (wr - wr_hi.astype(F32)).astype(BF16)

    prev = mix_buf[...]
    mix_buf[...] = jnp.dot(m_ref[...], w_ref[...], preferred_element_type=F32)
    h1 = _layer_norm_rows(alpha * h_ref[...] + prev, g_ref[...], b_ref[...])
    h_hi = h1.astype(BF16)
    h_lo = (h1 - h_hi.astype(F32)).astype(BF16)
    both = jnp.dot(h_hi, wrs_ref[...], preferred_element_type=F32)
    logits = (both[:, :HEAD_DIM] + both[:, HEAD_DIM:]
              + jnp.dot(h_lo, wrs_ref[:, :HEAD_DIM], preferred_element_type=F32))
    cls, wx, wy = _route(logits.T)
    tm = logits.shape[0]
    rt_ref[...] = jnp.concatenate([cls, wx, wy, jnp.zeros((5, tm), F32)], axis=0)
    ext = jnp.concatenate([wx, wy, jnp.zeros((HEAD_DIM - 2, tm), F32)], axis=0).T
    d = h1.shape[1]
    hf_ref[:, :d] = h1
    hf_ref[:, d:] = ext


def _outproj_ln_route(merged, w_out_b, h, ln_g, ln_b, w_router, alpha):
    t, d = h.shape
    tm = 512
    n = t // tm
    cur = lambda i: (jnp.minimum(i, n - 1), 0)
    prv = lambda i: (jnp.maximum(i - 1, 0), 0)
    return pl.pallas_call(
        functools.partial(_outproj_ln_route_kernel, alpha=alpha),
        out_shape=(jax.ShapeDtypeStruct((t, d + HEAD_DIM), F32), jax.ShapeDtypeStruct((8, t), F32)),
        grid=(n + 1,),
        in_specs=[pl.BlockSpec((tm, d), cur),
                  pl.BlockSpec((d, d), lambda i: (0, 0)),
                  pl.BlockSpec((tm, d), prv),
                  pl.BlockSpec((1, d), lambda i: (0, 0)),
                  pl.BlockSpec((1, d), lambda i: (0, 0)),
                  pl.BlockSpec((d, HEAD_DIM), lambda i: (0, 0))],
        out_specs=(pl.BlockSpec((tm, d + HEAD_DIM), prv),
                   pl.BlockSpec((8, tm), lambda i: (0, jnp.maximum(i - 1, 0)))),
        scratch_shapes=[pltpu.VMEM((tm, d), F32), pltpu.VMEM((d, 2 * HEAD_DIM), BF16)],
        compiler_params=_cparams(("arbitrary",)),
        name="outproj_ln1_route",
    )(merged, w_out_b, h, ln_g.reshape(1, d), ln_b.reshape(1, d), w_router)


MOE_TILE = 256
GATHER_TILE = 512
DMA_UNROLL = 8
MLP_CHUNKS = 4


def _dispatch_plan(rt, t):
    cls = rt[0].astype(jnp.int32)
    onehot = (cls[:, None] == jnp.arange(N_CLASSES, dtype=jnp.int32)[None, :]).astype(jnp.int32)
    csum = jnp.cumsum(onehot, axis=0)
    counts = csum[-1]
    rank = jnp.take_along_axis(csum, cls[:, None], axis=1)[:, 0] - 1
    padded = (counts + MOE_TILE - 1) // MOE_TILE * MOE_TILE
    ends = jnp.cumsum(padded)
    pos = ((ends - padded)[cls] + rank).astype(jnp.int32)
    p_rows = t + N_CLASSES * MOE_TILE
    tok = jnp.arange(t, dtype=jnp.int32)
    src = jnp.zeros((p_rows + MOE_TILE,), jnp.int32).at[pos].set(tok)
    n_tiles = p_rows // MOE_TILE
    n_used = (ends[-1] // MOE_TILE).astype(jnp.int32)
    tile_idx = jnp.arange(n_tiles, dtype=jnp.int32)
    tile_start = jnp.minimum(tile_idx, n_used - 1) * MOE_TILE
    tile_c = jnp.sum((ends[None, :] <= tile_start[:, None]).astype(jnp.int32), axis=1)
    tile_c = jnp.minimum(tile_c, N_CLASSES - 1)
    group = tile_c // PAIRS_PER_GROUP
    pair = tile_c % PAIRS_PER_GROUP
    tile_x = group * EXPERTS_PER_GROUP + jnp.asarray(PAIR_X, jnp.int32)[pair]
    tile_y = group * EXPERTS_PER_GROUP + jnp.asarray(PAIR_Y, jnp.int32)[pair]
    return src, pos, tile_x.astype(jnp.int32), tile_y.astype(jnp.int32), n_used.reshape(1)


def _expert_mlp_kernel(src_ref, tx_ref, ty_ref, nu_ref, h_hbm, wgx_ref, wux_ref, wdx_ref,
                       wgy_ref, wuy_ref, wdy_ref, y_ref, xbuf, sem):
    i = pl.program_id(0)
    n_used = nu_ref[0]
    slot = i % 2
    d = y_ref.shape[1]

    def issue_rows(tile, dst_slot, lo, hi):
        for j in range(lo, hi):
            row = src_ref[tile * MOE_TILE + j]
            pltpu.make_async_copy(h_hbm.at[pl.ds(row, 1)], xbuf.at[dst_slot, pl.ds(j, 1)],
                                  sem.at[dst_slot]).start()

    def wait_tile(s):
        pltpu.make_async_copy(h_hbm.at[pl.ds(0, MOE_TILE)], xbuf.at[s], sem.at[s]).wait()

    @pl.when(i == 0)
    def _():
        def first(c, carry):
            base = pl.multiple_of(c * DMA_UNROLL, DMA_UNROLL)
            for j in range(DMA_UNROLL):
                row = src_ref[base + j]
                pltpu.make_async_copy(h_hbm.at[pl.ds(row, 1)], xbuf.at[0, pl.ds(base + j, 1)],
                                      sem.at[0]).start()
            return carry
        lax.fori_loop(0, MOE_TILE // DMA_UNROLL, first, 0)

    @pl.when(i < n_used)
    def _():
        wait_tile(slot)
        nxt = 1 - slot
        de = wgx_ref.shape[2]
        cw = de // MLP_CHUNKS
        n_pieces = 4 * MLP_CHUNKS
        bounds = [MOE_TILE * k // n_pieces for k in range(n_pieces + 1)]
        piece = iter(range(n_pieces))

        def issue_next():
            k = next(piece)
            issue_rows(i + 1, nxt, bounds[k], bounds[k + 1])

        out = None
        for wg_ref, wu_ref, wd_ref, lane in ((wgx_ref, wux_ref, wdx_ref, d), (wgy_ref, wuy_ref, wdy_ref, d + 1)):
            hid = []
            for c in range(MLP_CHUNKS):
                cols = slice(c * cw, (c + 1) * cw)
                issue_next()
                a = jnp.dot(xbuf[slot, :, :d].astype(BF16), wg_ref[0, :, cols], preferred_element_type=F32)
                issue_next()
                u = jnp.dot(xbuf[slot, :, :d].astype(BF16), wu_ref[0, :, cols], preferred_element_type=F32)
                hid.append((jax.nn.silu(a) * u).astype(BF16))
            y = None
            for c in range(MLP_CHUNKS):
                part = jnp.dot(hid[c], wd_ref[0, c * cw:(c + 1) * cw, :], preferred_element_type=F32)
                y = part if y is None else y + part
            y = y * xbuf[slot, :, lane:lane + 1]
            out = y if out is None else out + y
        y_ref[...] = out

    @pl.when(i >= n_used)
    def _():
        y_ref[...] = jnp.zeros_like(y_ref)

    @pl.when(i == n_used)
    def _():
        wait_tile(slot)


def _expert_mlp(h1x, src, tile_x, tile_y, n_used, wg_b, wu_b, wd_b, layer):
    dx = h1x.shape[1]
    d = dx - HEAD_DIM
    p = src.shape[0] - MOE_TILE
    de = wg_b.shape[2]
    ex = lambda i, src, tx, ty, nu: (layer * N_EXPERTS + tx[i], 0, 0)
    ey = lambda i, src, tx, ty, nu: (layer * N_EXPERTS + ty[i], 0, 0)
    return pl.pallas_call(
        _expert_mlp_kernel,
        out_shape=jax.ShapeDtypeStruct((p, d), F32),
        grid_spec=pltpu.PrefetchScalarGridSpec(
            num_scalar_prefetch=4, grid=(p // MOE_TILE,),
            in_specs=[pl.BlockSpec(memory_space=pl.ANY),
                      pl.BlockSpec((1, d, de), ex),
                      pl.BlockSpec((1, d, de), ex),
                      pl.BlockSpec((1, de, d), ex),
                      pl.BlockSpec((1, d, de), ey),
                      pl.BlockSpec((1, d, de), ey),
                      pl.BlockSpec((1, de, d), ey)],
            out_specs=pl.BlockSpec((MOE_TILE, d), lambda i, src, tx, ty, nu: (i, 0)),
            scratch_shapes=[pltpu.VMEM((2, MOE_TILE, dx), F32), pltpu.SemaphoreType.DMA((2,))]),
        compiler_params=_cparams(("arbitrary",), vmem=MLP_VMEM_LIMIT),
        name="moe_expert_mlp",
    )(src, tile_x, tile_y, n_used, h1x, wg_b, wu_b, wd_b, wg_b, wu_b, wd_b)


def _combine_ln_kernel(pos_ref, y_hbm, h_ref, g_ref, b_ref, *refs, alpha):
    out_refs, (buf, sem) = refs[:-2], refs[-2:]
    i = pl.program_id(0)
    tm = h_ref.shape[0]

    def issue_tile(tile, slot):
        def issue(c, carry):
            base = pl.multiple_of(c * DMA_UNROLL, DMA_UNROLL)
            dst = buf.at[slot, pl.ds(base, DMA_UNROLL)]
            for j in range(DMA_UNROLL):
                row = pos_ref[tile * tm + base + j]
                pltpu.make_async_copy(y_hbm.at[pl.ds(row, 1)], dst.at[pl.ds(j, 1)],
                                      sem.at[slot]).start()
            return carry

        lax.fori_loop(0, tm // DMA_UNROLL, issue, 0)

    @pl.when(i == 0)
    def _():
        issue_tile(0, 0)

    @pl.when(i + 1 < pl.num_programs(0))
    def _():
        issue_tile(i + 1, (i + 1) % 2)

    slot = i % 2
    pltpu.make_async_copy(y_hbm.at[pl.ds(0, tm)], buf.at[slot], sem.at[slot]).wait()
    out = _layer_norm_rows(alpha * h_ref[...] + buf[slot], g_ref[...], b_ref[...])
    for o_ref in out_refs:
        o_ref[...] = out.astype(o_ref.dtype)


def _combine_ln(pos, y_sorted, h1x, ln_g, ln_b, alpha, out_dtypes):
    t = h1x.shape[0]
    d = y_sorted.shape[1]
    tm = GATHER_TILE
    return pl.pallas_call(
        functools.partial(_combine_ln_kernel, alpha=alpha),
        out_shape=tuple(jax.ShapeDtypeStruct((t, d), dt) for dt in out_dtypes),
        grid_spec=pltpu.PrefetchScalarGridSpec(
            num_scalar_prefetch=1, grid=(t // tm,),
            in_specs=[pl.BlockSpec(memory_space=pl.ANY),
                      pl.BlockSpec((tm, d), lambda i, pos: (i, 0)),
                      pl.BlockSpec((1, d), lambda i, pos: (0, 0)),
                      pl.BlockSpec((1, d), lambda i, pos: (0, 0))],
            out_specs=tuple(pl.BlockSpec((tm, d), lambda i, pos: (i, 0)) for _ in out_dtypes),
            scratch_shapes=[pltpu.VMEM((2, tm, d), F32), pltpu.SemaphoreType.DMA((2,))]),
        compiler_params=_cparams(("arbitrary",)),
        name="moe_combine_ln2",
    )(pos, y_sorted, h1x, ln_g.reshape(1, d), ln_b.reshape(1, d))


def _moe(h1x, rt, wg_b, wu_b, wd_b, layer, ln_g, ln_b, alpha, out_dtypes):
    t = h1x.shape[0]
    src, pos, tile_x, tile_y, n_used = _dispatch_plan(rt, t)
    y_sorted = _expert_mlp(h1x, src, tile_x, tile_y, n_used, wg_b, wu_b, wd_b, layer)
    return _combine_ln(pos, y_sorted, h1x, ln_g, ln_b, alpha, out_dtypes)


def kernel(x, positions, w_in, w_gate, b_gate, w_branch_a, w_branch_b, w_branch_c, w_out, sinks,
           ln1_g, ln1_b, w_router_group, w_router_expert, w_exp_gate, w_exp_up, w_exp_down,
           ln2_g, ln2_b):
    batch, seq, d = x.shape
    depth = w_in.shape[0]
    t = batch * seq
    alpha = (2 * depth) ** 0.25
    hf = x.reshape(t, d)
    cc, ss, hb = _rope_tables(positions, hf)
    wa_b, wb_b, wc_b = w_branch_a.astype(BF16), w_branch_b.astype(BF16), w_branch_c.astype(BF16)
    w_in_b = w_in[0].astype(BF16)
    w_in2, w_gate2, w_out2 = (w.reshape(depth * d, w.shape[-1]) for w in (w_in, w_gate, w_out))
    de = w_exp_gate.shape[-1]
    w_exp = ((w_exp_gate.reshape(depth * N_EXPERTS * d, de), N_EXPERTS * d),
             (w_exp_up.reshape(depth * N_EXPERTS * d, de), N_EXPERTS * d),
             (w_exp_down.reshape(depth * N_EXPERTS * de, d), N_EXPERTS * de))
    for l in range(depth):
        last = l == depth - 1
        qkv, w_gate_b, w_out_b = _qkv_proj(hb, w_in_b, cc, ss, sides=((w_gate2, d, l), (w_out2, d, l)))
        out_a = _dilated_attention(qkv, batch, seq)
        out_b = _moba_attention(qkv, batch, seq)
        out_c = _swa_attention(qkv, sinks[l], batch, seq)
        sides = [(w, rows, l) for w, rows in w_exp] + ([] if last else [(w_in2, d, l + 1)])
        merged, wg_b, wu_b, wd_b, *nxt = _merge(hb, out_a, out_b, out_c, w_gate_b, b_gate, wa_b, wb_b, wc_b,
                                                l, sides)
        if not last:
            w_in_b = nxt[0]
        wg_b, wu_b = wg_b.reshape(N_EXPERTS, d, de), wu_b.reshape(N_EXPERTS, d, de)
        wd_b = wd_b.reshape(N_EXPERTS, de, d)
        w_router = jnp.zeros((d, HEAD_DIM), F32)
        w_router = w_router.at[:, 0:N_GROUPS].set(w_router_group[l])
        w_router = w_router.at[:, ROUTER_EXPERT_COL:ROUTER_EXPERT_COL + N_EXPERTS].set(w_router_expert[l])
        h1f, rt = _outproj_ln_route(merged, w_out_b, hf, ln1_g[l], ln1_b[l], w_router, alpha)
        outs = _moe(h1f, rt, wg_b, wu_b, wd_b, 0, ln2_g[l], ln2_b[l], alpha,
                    (F32,) if last else (F32, BF16))
        hf, hb = (outs[0], None) if last else outs
    return hf.reshape(batch, seq, d)
```

```python
import functools

import jax
import jax.numpy as jnp
from jax import lax
from jax.experimental import pallas as pl
from jax.experimental.pallas import tpu as pltpu

F32 = jnp.float32
BF16 = jnp.bfloat16

HEAD_DIM = 128
ATTN_SCALE = HEAD_DIM ** -0.5
ROPE_THETA = 10000.0
A_PATTERNS = ((128, 1), (512, 4), (2048, 16))
A_HEADS_PER_GROUP = 2
A_HEADS = A_HEADS_PER_GROUP * len(A_PATTERNS)
B_HEADS = 6
MOBA_BLOCK = 256
MOBA_TOPK = 3
C_Q_HEADS = 8
C_KV_HEADS = 2
C_GROUP = C_Q_HEADS // C_KV_HEADS
C_WINDOW = 128
BAND = 128
N_Q_HEADS = A_HEADS + B_HEADS + C_Q_HEADS
N_KV_HEADS = A_HEADS + B_HEADS + C_KV_HEADS
N_GROUPS = 4
EXPERTS_PER_GROUP = 4
N_EXPERTS = N_GROUPS * EXPERTS_PER_GROUP
PAIRS_PER_GROUP = EXPERTS_PER_GROUP * (EXPERTS_PER_GROUP - 1) // 2
N_CLASSES = N_GROUPS * PAIRS_PER_GROUP
PAIR_X = (0, 0, 0, 1, 1, 3)
PAIR_Y = (1, 2, 3, 3, 2, 2)
ROUTER_EXPERT_COL = 8
LN_EPS = 1e-5
NEG = -1e30

V7X_SUBLANES = 8
V7X_VMEM_BYTES = 64 * 1024 * 1024
VMEM_LIMIT = V7X_VMEM_BYTES - 8 * 1024 * 1024
MLP_VMEM_LIMIT = V7X_VMEM_BYTES - 2 * 1024 * 1024


def _cparams(sem, vmem=VMEM_LIMIT):
    return pltpu.CompilerParams(dimension_semantics=sem, vmem_limit_bytes=vmem)


ROPE_SLOT_Q, ROPE_SLOT_K, ROPE_SLOT_V, ROPE_SLOT_Q_BASE2 = 0, 1, 2, 3
ROPE_SLOTS = 4
LOG2_E = 1.4426950408889634


def _rope_table_kernel(pos_ref, inv_ref, x_ref, cc_ref, ss_ref, xb_ref):
    xb_ref[...] = x_ref[...].astype(BF16)
    ang = pos_ref[...].astype(F32) * inv_ref[...]
    lane = lax.broadcasted_iota(jnp.int32, ang.shape, 1)
    s = jnp.sin(ang)
    c = jnp.cos(ang)
    s = jnp.where(lane < HEAD_DIM // 2, -s, s)
    cc_ref[ROPE_SLOT_Q] = c * ATTN_SCALE
    ss_ref[ROPE_SLOT_Q] = s * ATTN_SCALE
    cc_ref[ROPE_SLOT_K] = c
    ss_ref[ROPE_SLOT_K] = s
    cc_ref[ROPE_SLOT_V] = jnp.ones_like(c)
    ss_ref[ROPE_SLOT_V] = jnp.zeros_like(s)
    cc_ref[ROPE_SLOT_Q_BASE2] = c * (ATTN_SCALE * LOG2_E)
    ss_ref[ROPE_SLOT_Q_BASE2] = s * (ATTN_SCALE * LOG2_E)


def _rope_tables(positions, x):
    t, d = x.shape
    tm = 512
    inv = ROPE_THETA ** (-jnp.arange(0, HEAD_DIM, 2, dtype=F32) / HEAD_DIM)
    inv2 = jnp.concatenate([inv, inv]).reshape(1, HEAD_DIM)
    pos = positions.reshape(t, 1)
    table = jax.ShapeDtypeStruct((ROPE_SLOTS, t, HEAD_DIM), F32)
    table_spec = pl.BlockSpec((ROPE_SLOTS, tm, HEAD_DIM), lambda i: (0, i, 0))
    return pl.pallas_call(
        _rope_table_kernel,
        out_shape=(table, table, jax.ShapeDtypeStruct((t, d), BF16)),
        grid=(t // tm,),
        in_specs=[pl.BlockSpec((tm, 1), lambda i: (i, 0)),
                  pl.BlockSpec((1, HEAD_DIM), lambda i: (0, 0)),
                  pl.BlockSpec((tm, d), lambda i: (i, 0))],
        out_specs=(table_spec, table_spec, pl.BlockSpec((tm, d), lambda i: (i, 0))),
        compiler_params=_cparams(("arbitrary",)),
        name="rope_tables",
    )(pos, inv2, x)


def _side_cast_specs(sides, n_steps, step_of):
    ins, outs, shapes = [], [], []
    for w, rows, layer in sides:
        assert rows % n_steps == 0, (rows, n_steps)
        sr = rows // n_steps
        ins.append(pl.BlockSpec((sr, w.shape[1]), lambda *g, layer=layer: (layer * n_steps + step_of(*g), 0)))
        outs.append(pl.BlockSpec((sr, w.shape[1]), lambda *g: (step_of(*g), 0)))
        shapes.append(jax.ShapeDtypeStruct((rows, w.shape[1]), BF16))
    return ins, outs, shapes


def _cast_sides(in_refs, out_refs):
    for i_ref, o_ref in zip(in_refs, out_refs, strict=True):
        o_ref[...] = i_ref[...].astype(BF16)


def _qkv_kernel(x_ref, w_ref, cc_ref, ss_ref, *refs, heads_per_tile):
    n_side = len(refs) // 2
    o_ref = refs[n_side]
    _cast_sides(refs[:n_side], refs[n_side + 1:])
    j = pl.program_id(1)
    acc = jnp.dot(x_ref[...], w_ref[...], preferred_element_type=F32)
    for hh in range(heads_per_tile):
        head = j * heads_per_tile + hh
        in_b = (head >= A_HEADS) & (head < A_HEADS + B_HEADS)
        slot = ((head >= N_Q_HEADS).astype(jnp.int32) + (head >= N_Q_HEADS + N_KV_HEADS).astype(jnp.int32)
                + ROPE_SLOT_Q_BASE2 * in_b.astype(jnp.int32))
        t = acc[:, hh * HEAD_DIM:(hh + 1) * HEAD_DIM]
        r = t * cc_ref[slot] + pltpu.roll(t, HEAD_DIM // 2, 1) * ss_ref[slot]
        o_ref[hh] = r.astype(BF16)


def _qkv_proj(hb, w_in_b, cc, ss, sides=()):
    t, d = hb.shape
    n = w_in_b.shape[1]
    tm, tn = 1024, 1536
    hpt = tn // HEAD_DIM
    ni, nj = t // tm, n // tn
    side_in, side_out, side_shape = _side_cast_specs(sides, ni * nj, lambda i, j: i * nj + j)
    return pl.pallas_call(
        functools.partial(_qkv_kernel, heads_per_tile=hpt),
        out_shape=(jax.ShapeDtypeStruct((n // HEAD_DIM, t, HEAD_DIM), BF16), *side_shape),
        grid=(ni, nj),
        in_specs=[pl.BlockSpec((tm, d), lambda i, j: (i, 0)),
                  pl.BlockSpec((d, tn), lambda i, j: (0, j)),
                  pl.BlockSpec((ROPE_SLOTS, tm, HEAD_DIM), lambda i, j: (0, i, 0)),
                  pl.BlockSpec((ROPE_SLOTS, tm, HEAD_DIM), lambda i, j: (0, i, 0)), *side_in],
        out_specs=(pl.BlockSpec((hpt, tm, HEAD_DIM), lambda i, j: (j, i, 0)), *side_out),
        compiler_params=_cparams(("arbitrary", "arbitrary")),
        name="qkv_rope",
    )(hb, w_in_b, cc, ss, *[w for w, _, _ in sides])


def _band_bias(max_dist):
    qi = lax.broadcasted_iota(jnp.int32, (BAND, 2 * BAND), 0)
    kj = lax.broadcasted_iota(jnp.int32, (BAND, 2 * BAND), 1)
    dist = qi + BAND - kj
    valid = (dist >= 0) & (dist <= max_dist)
    bias = jnp.where(valid, 0.0, NEG).astype(F32)
    bias_first = jnp.where(valid & (kj >= BAND), 0.0, NEG).astype(F32)
    return bias, bias_first


def _band_block(qb, kwin, vwin, bias):
    g = qb.shape[0] // BAND
    s = lax.dot_general(qb, kwin, (((1,), (1,)), ((), ())), preferred_element_type=F32)
    s = s.reshape(g, BAND, 2 * BAND) + bias[None]
    m = jnp.max(s, axis=-1, keepdims=True)
    p = jnp.exp(s - m)
    l = jnp.sum(p, axis=-1, keepdims=True)
    o = jnp.dot(p.reshape(g * BAND, 2 * BAND).astype(BF16), vwin, preferred_element_type=F32)
    o = o.reshape(g, BAND, HEAD_DIM) / l
    return o, m + jnp.log(l)


def _swa_kernel(sink_ref, q_ref, k_ref, kh_ref, v_ref, vh_ref, o_ref, *, tq):
    kv = pl.program_id(1)
    t = pl.program_id(2)
    bias, bias_first = _band_bias(C_WINDOW - 1)
    for blk in range(tq // BAND):
        qb = q_ref[:, blk * BAND:(blk + 1) * BAND, :].reshape(C_GROUP * BAND, HEAD_DIM)
        if blk == 0:
            kwin = jnp.concatenate([kh_ref[0], k_ref[0, :BAND]], axis=0)
            vwin = jnp.concatenate([vh_ref[0], v_ref[0, :BAND]], axis=0)
            b = jnp.where(t == 0, bias_first, bias)
        else:
            kwin = k_ref[0, (blk - 1) * BAND:(blk + 1) * BAND]
            vwin = v_ref[0, (blk - 1) * BAND:(blk + 1) * BAND]
            b = bias
        o, lse = _band_block(qb, kwin, vwin, b)
        for g in range(C_GROUP):
            keep = jax.nn.sigmoid(lse[g] - sink_ref[kv * C_GROUP + g])
            o_ref[blk * BAND:(blk + 1) * BAND, g * HEAD_DIM:(g + 1) * HEAD_DIM] = (o[g] * keep).astype(BF16)


def _swa_attention(qkv, sinks, batch, seq):
    tq = 2048
    nt = seq // tq
    q0 = A_HEADS + B_HEADS
    k0 = N_Q_HEADS + A_HEADS + B_HEADS
    v0 = N_Q_HEADS + N_KV_HEADS + A_HEADS + B_HEADS
    rb = tq // BAND

    def halo(b, kv, t):
        return jnp.maximum(b * (seq // BAND) + t * rb - 1, 0)

    return pl.pallas_call(
        functools.partial(_swa_kernel, tq=tq),
        out_shape=jax.ShapeDtypeStruct((batch * seq, C_Q_HEADS * HEAD_DIM), BF16),
        grid=(batch, C_KV_HEADS, nt),
        in_specs=[
            pl.BlockSpec(memory_space=pltpu.SMEM),
            pl.BlockSpec((C_GROUP, tq, HEAD_DIM), lambda b, kv, t: (q0 // C_GROUP + kv, b * nt + t, 0)),
            pl.BlockSpec((1, tq, HEAD_DIM), lambda b, kv, t: (k0 + kv, b * nt + t, 0)),
            pl.BlockSpec((1, BAND, HEAD_DIM), lambda b, kv, t: (k0 + kv, halo(b, kv, t), 0)),
            pl.BlockSpec((1, tq, HEAD_DIM), lambda b, kv, t: (v0 + kv, b * nt + t, 0)),
            pl.BlockSpec((1, BAND, HEAD_DIM), lambda b, kv, t: (v0 + kv, halo(b, kv, t), 0)),
        ],
        out_specs=pl.BlockSpec((tq, C_GROUP * HEAD_DIM), lambda b, kv, t: (b * nt + t, kv)),
        compiler_params=_cparams(("arbitrary",) * 3),
        name="swa_sink",
    )(sinks, qkv, qkv, qkv, qkv, qkv)


def _dilated_kernel(*refs, seq):
    ng = len(A_PATTERNS)
    in_refs = refs[:3 * ng]
    o_ref = refs[3 * ng]
    qf, kf, vf, og, lg = refs[3 * ng + 1:]
    for g, (window, dil) in enumerate(A_PATTERNS):
        q_ref, k_ref, v_ref = in_refs[3 * g:3 * g + 3]
        qf[...] = q_ref[0].astype(F32)
        kf[...] = k_ref[0].astype(F32)
        vf[...] = v_ref[0].astype(F32)
        nblk = seq // dil // BAND
        bias, bias_first = _band_bias(window // dil)

        for r in range(dil):
            for n in range(nblk):
                own = pl.ds(r + dil * BAND * n, BAND, stride=dil)
                prev = pl.ds(r + dil * BAND * max(n - 1, 0), BAND, stride=dil)
                qb = qf[own, :].astype(BF16)
                kwin = jnp.concatenate([kf[prev, :], kf[own, :]], axis=0).astype(BF16)
                vwin = jnp.concatenate([vf[prev, :], vf[own, :]], axis=0).astype(BF16)
                o, lse = _band_block(qb, kwin, vwin, bias_first if n == 0 else bias)
                og[g, own, :] = o[0]
                lg[g, own, :] = jnp.broadcast_to(lse[0], (BAND, HEAD_DIM))

    rows = 256

    def combine(c, carry):
        sl = pl.ds(pl.multiple_of(c * rows, rows), rows)
        ls = [lg[g, sl, :] for g in range(ng)]
        m = functools.reduce(jnp.maximum, ls)
        es = [jnp.exp(l - m) for l in ls]
        num = functools.reduce(jnp.add, [e * og[g, sl, :] for g, e in enumerate(es)])
        o_ref[sl, :] = (num / functools.reduce(jnp.add, es)).astype(BF16)
        return carry

    lax.fori_loop(0, seq // rows, combine, 0)


def _dilated_attention(qkv, batch, seq):
    ng = len(A_PATTERNS)
    k0 = N_Q_HEADS
    v0 = N_Q_HEADS + N_KV_HEADS
    in_specs = []
    for g in range(ng):
        for base in (0, k0, v0):
            in_specs.append(pl.BlockSpec(
                (1, seq, HEAD_DIM),
                lambda b, j, h=base + g * A_HEADS_PER_GROUP: (h + j, b, 0)))
    return pl.pallas_call(
        functools.partial(_dilated_kernel, seq=seq),
        out_shape=jax.ShapeDtypeStruct((batch * seq, A_HEADS_PER_GROUP * HEAD_DIM), BF16),
        grid=(batch, A_HEADS_PER_GROUP),
        in_specs=in_specs,
        out_specs=pl.BlockSpec((seq, HEAD_DIM), lambda b, j: (b, j)),
        scratch_shapes=[pltpu.VMEM((seq, HEAD_DIM), F32)] * 3
        + [pltpu.VMEM((ng, seq, HEAD_DIM), F32)] * 2,
        compiler_params=_cparams(("arbitrary",) * 2),
        name="dilated_attn",
    )(*([qkv] * (3 * ng)))


MOBA_TQ = 1024
MOBA_HEADS = 2


def _online_update(state, s, v):
    mc = jnp.max(s, axis=1, keepdims=True)
    if state is None:
        p = jnp.exp2(s - mc)
        return mc, jnp.sum(p, axis=1, keepdims=True), jnp.dot(p.astype(BF16), v, preferred_element_type=F32)
    m, l, acc = state
    m_new = jnp.maximum(m, mc)
    alpha = jnp.exp2(m - m_new)
    p = jnp.exp2(s - m_new)
    l_new = alpha * l + jnp.sum(p, axis=1, keepdims=True)
    acc_new = alpha * acc + jnp.dot(p.astype(BF16), v, preferred_element_type=F32)
    return m_new, l_new, acc_new


def _moba_kernel(q_ref, k_ref, v_ref, o_ref, kaug, kmean, *, seq):
    t = pl.program_id(2)
    nblk = seq // MOBA_BLOCK
    blk = MOBA_BLOCK
    tq = MOBA_TQ
    nb = tq // blk
    nh = MOBA_HEADS

    @pl.when(t == 0)
    def _():
        row = lax.broadcasted_iota(jnp.int32, (seq, HEAD_DIM), 0)
        lane = lax.broadcasted_iota(jnp.int32, (seq, HEAD_DIM), 1)
        onehot = jnp.where(row // blk == lane, 1.0, 0.0).astype(BF16)
        for h in range(nh):
            kaug[h, :, :HEAD_DIM] = k_ref[h]
            kaug[h, :, HEAD_DIM:] = onehot
            kmean[h] = jnp.zeros((HEAD_DIM, HEAD_DIM), F32)
            for n in range(nblk):
                kb = k_ref[h, n * blk:(n + 1) * blk, :].astype(F32)
                kmean[h, n:n + 1, :] = jnp.mean(kb, axis=0, keepdims=True)

    nt = (((1,), (1,)), ((), ()))
    nrow = -(-nblk // V7X_SUBLANES) * V7X_SUBLANES
    row = lax.broadcasted_iota(jnp.int32, (nrow, tq), 0).astype(F32)
    own = (t * nb + lax.broadcasted_iota(jnp.int32, (nrow, tq), 1) // blk).astype(F32)
    qaugs = []
    for h in range(nh):
        q = q_ref[h]
        km = kmean[h]
        km_hi = km.astype(BF16)
        km_lo = (km - km_hi.astype(F32)).astype(BF16)
        gate = (lax.dot_general(km_hi, q, nt, preferred_element_type=F32)
                + lax.dot_general(km_lo, q, nt, preferred_element_type=F32))
        g = jnp.where(row < own, gate[:nrow], NEG)
        sel = jnp.zeros((nrow, tq), jnp.bool_)
        for _ in range(MOBA_TOPK):
            mx = jnp.max(g, axis=0, keepdims=True)
            first = jnp.min(jnp.where(g == mx, row, float(nrow)), axis=0, keepdims=True)
            pick = row == first
            sel = sel | pick
            g = jnp.where(pick, NEG, g)
        visible = (sel & (row < own)) | (row == own)
        bias_t = jnp.concatenate([jnp.where(visible, 0.0, NEG), jnp.zeros((HEAD_DIM - nrow, tq), F32)], axis=0)
        qaugs.append(jnp.concatenate([q, bias_t.T.astype(BF16)], axis=1))

    ri = lax.broadcasted_iota(jnp.int32, (blk, blk), 0)
    ci = lax.broadcasted_iota(jnp.int32, (blk, blk), 1)
    causal = ri >= ci

    for c in range(seq // tq):
        @pl.when(t == c)
        def _(c=c):
            states = [None] * nh
            for cc in range(c):
                keys = slice(cc * tq, (cc + 1) * tq)
                for h in range(nh):
                    s = lax.dot_general(qaugs[h], kaug[h, keys, :], nt, preferred_element_type=F32)
                    states[h] = _online_update(states[h], s, v_ref[h, keys, :])
            n_split = 2 if c == 0 else 1
            hb = nb // n_split
            for h in range(nh):
                parts = []
                for p in range(n_split):
                    rows = slice(p * hb * blk, (p + 1) * hb * blk)
                    keys = slice(c * tq, c * tq + (p + 1) * hb * blk)
                    s = lax.dot_general(qaugs[h][rows], kaug[h, keys, :], nt, preferred_element_type=F32)
                    s = jnp.concatenate(
                        [jnp.concatenate(
                            [jnp.where(causal, s[a * blk:(a + 1) * blk, e * blk:(e + 1) * blk], NEG)
                             if e == a + p * hb else s[a * blk:(a + 1) * blk, e * blk:(e + 1) * blk]
                             for e in range((p + 1) * hb)], axis=1) for a in range(hb)], axis=0)
                    st = None if states[h] is None else tuple(x[rows] for x in states[h])
                    parts.append(_online_update(st, s, v_ref[h, keys, :]))
                m, l, acc = (jnp.concatenate([pp[i] for pp in parts], axis=0) for i in range(3))
                o_ref[:, h * HEAD_DIM:(h + 1) * HEAD_DIM] = (acc / l).astype(BF16)


def _moba_attention(qkv, batch, seq):
    nt = seq // MOBA_TQ
    nh = MOBA_HEADS
    q0 = A_HEADS
    k0 = N_Q_HEADS + A_HEADS
    v0 = N_Q_HEADS + N_KV_HEADS + A_HEADS
    return pl.pallas_call(
        functools.partial(_moba_kernel, seq=seq),
        out_shape=jax.ShapeDtypeStruct((batch * seq, B_HEADS * HEAD_DIM), BF16),
        grid=(batch, B_HEADS // nh, nt),
        in_specs=[
            pl.BlockSpec((nh, MOBA_TQ, HEAD_DIM), lambda b, h, t: (q0 // nh + h, b * nt + t, 0)),
            pl.BlockSpec((nh, seq, HEAD_DIM), lambda b, h, t: (k0 // nh + h, b, 0)),
            pl.BlockSpec((nh, seq, HEAD_DIM), lambda b, h, t: (v0 // nh + h, b, 0)),
        ],
        out_specs=pl.BlockSpec((MOBA_TQ, nh * HEAD_DIM), lambda b, h, t: (b * nt + t, h)),
        scratch_shapes=[pltpu.VMEM((nh, seq, 2 * HEAD_DIM), BF16), pltpu.VMEM((nh, HEAD_DIM, HEAD_DIM), F32)],
        compiler_params=_cparams(("arbitrary",) * 3),
        name="moba_attn",
    )(qkv, qkv, qkv)


def _merge_kernel(h_ref, oa_ref, ob_ref, oc_ref, g0_ref, g1_ref, g2_ref, b0_ref, b1_ref, b2_ref,
                  wa_ref, wb_ref, wc_ref, *refs):
    n_side = len(refs) // 2
    o_ref = refs[n_side]
    _cast_sides(refs[:n_side], refs[n_side + 1:])
    x = h_ref[...]
    acc = None
    for g_ref, b_ref, br_ref, w_ref in ((g0_ref, b0_ref, oa_ref, wa_ref),
                                        (g1_ref, b1_ref, ob_ref, wb_ref),
                                        (g2_ref, b2_ref, oc_ref, wc_ref)):
        gate = jax.nn.sigmoid(jnp.dot(x, g_ref[...], preferred_element_type=F32) + b_ref[...])
        term = gate * jnp.dot(br_ref[...], w_ref[...], preferred_element_type=F32)
        acc = term if acc is None else acc + term
    o_ref[...] = acc.astype(BF16)


def _merge(hb, out_a, out_b, out_c, w_gate_b, b_gate, wa_b, wb_b, wc_b, layer, sides=()):
    t, d = hb.shape
    tm, tn = 1024, 512
    nj = d // tn
    ni = t // tm
    side_in, side_out, side_shape = _side_cast_specs(sides, nj * ni, lambda j, i: j * ni + i)
    row = lambda w: pl.BlockSpec((tm, w), lambda j, i: (i, 0))
    gate_w = lambda x: pl.BlockSpec((d, tn), lambda j, i, x=x: (0, x * nj + j))
    gate_b = lambda x: pl.BlockSpec((None, 1, tn), lambda j, i, x=x: (layer, 0, x * nj + j))
    br_w = lambda k: pl.BlockSpec((None, k, tn), lambda j, i: (layer, 0, j))
    bg = b_gate.reshape(b_gate.shape[0], 1, -1)
    return pl.pallas_call(
        _merge_kernel,
        out_shape=(jax.ShapeDtypeStruct((t, d), BF16), *side_shape),
        grid=(nj, ni),
        in_specs=[row(d), row(out_a.shape[1]), row(out_b.shape[1]), row(out_c.shape[1]),
                  gate_w(0), gate_w(1), gate_w(2), gate_b(0), gate_b(1), gate_b(2),
                  br_w(wa_b.shape[1]), br_w(wb_b.shape[1]), br_w(wc_b.shape[1]), *side_in],
        out_specs=(pl.BlockSpec((tm, tn), lambda j, i: (i, j)), *side_out),
        compiler_params=_cparams(("arbitrary",) * 2),
        name="branch_merge",
    )(hb, out_a, out_b, out_c, w_gate_b, w_gate_b, w_gate_b, bg, bg, bg, wa_b, wb_b, wc_b,
      *[w for w, _, _ in sides])


def _layer_norm_rows(y, g, b):
    mu = jnp.mean(y, axis=-1, keepdims=True)
    yc = y - mu
    var = jnp.mean(yc * yc, axis=-1, keepdims=True)
    return yc * lax.rsqrt(var + LN_EPS) * g + b


def _route(lt):
    tm = lt.shape[1]
    gl = lt[0:N_GROUPS]
    gmax = jnp.max(gl, axis=0, keepdims=True)
    ge = jnp.exp(gl - gmax)
    gp = ge / jnp.sum(ge, axis=0, keepdims=True)
    g_val = jnp.max(gp, axis=0, keepdims=True)
    row = lax.broadcasted_iota(jnp.int32, (N_GROUPS, tm), 0)
    g_sel = jnp.min(jnp.where(gp == g_val, row, N_GROUPS), axis=0, keepdims=True)
    el = jnp.zeros((EXPERTS_PER_GROUP, tm), F32)
    for g in range(N_GROUPS):
        lo = ROUTER_EXPERT_COL + g * EXPERTS_PER_GROUP
        el = jnp.where(g_sel == g, lt[lo:lo + EXPERTS_PER_GROUP], el)
    v0 = jnp.max(el, axis=0, keepdims=True)
    i0 = jnp.min(jnp.where(el == v0, row, EXPERTS_PER_GROUP), axis=0, keepdims=True)
    el1 = jnp.where(row == i0, -jnp.inf, el)
    v1 = jnp.max(el1, axis=0, keepdims=True)
    i1 = jnp.min(jnp.where(el1 == v1, row, EXPERTS_PER_GROUP), axis=0, keepdims=True)
    e1 = jnp.exp(v1 - v0)
    den = 1.0 + e1
    w0 = 1.0 / den * g_val
    w1 = e1 / den * g_val
    a = jnp.minimum(i0, i1)
    b = jnp.maximum(i0, i1)
    swap = (a == 2) & (b == 3)
    x = jnp.where(swap, b, a)
    y = jnp.where(swap, a, b)
    rank = jnp.where(a == 0, b - 1, jnp.where(a == 1, jnp.where(b == 3, 3, 4), 5))
    cls = (g_sel * PAIRS_PER_GROUP + rank).astype(F32)
    wx = jnp.where(i0 == x, w0, w1)
    wy = jnp.where(i0 == y, w0, w1)
    return cls, wx, wy


def _outproj_ln_route_kernel(m_ref, w_ref, h_ref, g_ref, b_ref, wr_ref, hf_ref, rt_ref, mix_buf,
                             wrs_ref, *, alpha):
    @pl.when(pl.program_id(0) == 0)
    def _():
        mix_buf[...] = jnp.zeros_like(mix_buf)
        wr = wr_ref[...]
        wr_hi = wr.astype(BF16)
        wrs_ref[:, :HEAD_DIM] = wr_hi
        wrs_ref[:, HEAD_DIM:] = (wr - wr_hi.astype(F32)).astype(BF16)

    prev = mix_buf[...]
    mix_buf[...] = jnp.dot(m_ref[...], w_ref[...], preferred_element_type=F32)
    h1 = _layer_norm_rows(alpha * h_ref[...] + prev, g_ref[...], b_ref[...])
    h_hi = h1.astype(BF16)
    h_lo = (h1 - h_hi.astype(F32)).astype(BF16)
    both = jnp.dot(h_hi, wrs_ref[...], preferred_element_type=F32)
    logits = (both[:, :HEAD_DIM] + both[:, HEAD_DIM:]
              + jnp.dot(h_lo, wrs_ref[:, :HEAD_DIM], preferred_element_type=F32))
    cls, wx, wy = _route(logits.T)
    tm = logits.shape[0]
    rt_ref[...] = jnp.concatenate([cls, wx, wy, jnp.zeros((5, tm), F32)], axis=0)
    ext = jnp.concatenate([wx, wy, jnp.zeros((HEAD_DIM - 2, tm), F32)], axis=0).T
    d = h1.shape[1]
    hf_ref[:, :d] = h1
    hf_ref[:, d:] = ext


def _outproj_ln_route(merged, w_out_b, h, ln_g, ln_b, w_router, alpha):
    t, d = h.shape
    tm = 512
    n = t // tm
    cur = lambda i: (jnp.minimum(i, n - 1), 0)
    prv = lambda i: (jnp.maximum(i - 1, 0), 0)
    return pl.pallas_call(
        functools.partial(_outproj_ln_route_kernel, alpha=alpha),
        out_shape=(jax.ShapeDtypeStruct((t, d + HEAD_DIM), F32), jax.ShapeDtypeStruct((8, t), F32)),
        grid=(n + 1,),
        in_specs=[pl.BlockSpec((tm, d), cur),
                  pl.BlockSpec((d, d), lambda i: (0, 0)),
                  pl.BlockSpec((tm, d), prv),
                  pl.BlockSpec((1, d), lambda i: (0, 0)),
                  pl.BlockSpec((1, d), lambda i: (0, 0)),
                  pl.BlockSpec((d, HEAD_DIM), lambda i: (0, 0))],
        out_specs=(pl.BlockSpec((tm, d + HEAD_DIM), prv),
                   pl.BlockSpec((8, tm), lambda i: (0, jnp.maximum(i - 1, 0)))),
        scratch_shapes=[pltpu.VMEM((tm, d), F32), pltpu.VMEM((d, 2 * HEAD_DIM), BF16)],
        compiler_params=_cparams(("arbitrary",)),
        name="outproj_ln1_route",
    )(merged, w_out_b, h, ln_g.reshape(1, d), ln_b.reshape(1, d), w_router)


MOE_TILE = 256
GATHER_TILE = 512
DMA_UNROLL = 8
MLP_CHUNKS = 2


def _dispatch_plan(rt, t):
    cls = rt[0].astype(jnp.int32)
    onehot = (cls[:, None] == jnp.arange(N_CLASSES, dtype=jnp.int32)[None, :]).astype(jnp.int32)
    csum = jnp.cumsum(onehot, axis=0)
    counts = csum[-1]
    rank = jnp.take_along_axis(csum, cls[:, None], axis=1)[:, 0] - 1
    padded = (counts + MOE_TILE - 1) // MOE_TILE * MOE_TILE
    ends = jnp.cumsum(padded)
    pos = ((ends - padded)[cls] + rank).astype(jnp.int32)
    p_rows = t + N_CLASSES * MOE_TILE
    tok = jnp.arange(t, dtype=jnp.int32)
    src = jnp.zeros((p_rows + MOE_TILE,), jnp.int32).at[pos].set(tok)
    n_tiles = p_rows // MOE_TILE
    n_used = (ends[-1] // MOE_TILE).astype(jnp.int32)
    tile_idx = jnp.arange(n_tiles, dtype=jnp.int32)
    tile_start = jnp.minimum(tile_idx, n_used - 1) * MOE_TILE
    tile_c = jnp.sum((ends[None, :] <= tile_start[:, None]).astype(jnp.int32), axis=1)
    tile_c = jnp.minimum(tile_c, N_CLASSES - 1)
    group = tile_c // PAIRS_PER_GROUP
    pair = tile_c % PAIRS_PER_GROUP
    tile_x = group * EXPERTS_PER_GROUP + jnp.asarray(PAIR_X, jnp.int32)[pair]
    tile_y = group * EXPERTS_PER_GROUP + jnp.asarray(PAIR_Y, jnp.int32)[pair]
    return src, pos, tile_x.astype(jnp.int32), tile_y.astype(jnp.int32), n_used.reshape(1)


def _expert_mlp_kernel(src_ref, tx_ref, ty_ref, nu_ref, h_hbm, wgx_ref, wux_ref, wdx_ref,
                       wgy_ref, wuy_ref, wdy_ref, y_ref, xbuf, sem):
    i = pl.program_id(0)
    n_used = nu_ref[0]
    slot = i % 2
    d = y_ref.shape[1]

    def issue_rows(tile, dst_slot, lo, hi):
        for j in range(lo, hi):
            row = src_ref[tile * MOE_TILE + j]
            pltpu.make_async_copy(h_hbm.at[pl.ds(row, 1)], xbuf.at[dst_slot, pl.ds(j, 1)],
                                  sem.at[dst_slot]).start()

    def wait_tile(s):
        pltpu.make_async_copy(h_hbm.at[pl.ds(0, MOE_TILE)], xbuf.at[s], sem.at[s]).wait()

    @pl.when(i == 0)
    def _():
        def first(c, carry):
            base = pl.multiple_of(c * DMA_UNROLL, DMA_UNROLL)
            for j in range(DMA_UNROLL):
                row = src_ref[base + j]
                pltpu.make_async_copy(h_hbm.at[pl.ds(row, 1)], xbuf.at[0, pl.ds(base + j, 1)],
                                      sem.at[0]).start()
            return carry
        lax.fori_loop(0, MOE_TILE // DMA_UNROLL, first, 0)

    @pl.when(i < n_used)
    def _():
        wait_tile(slot)
        nxt = 1 - slot
        de = wgx_ref.shape[2]
        cw = de // MLP_CHUNKS
        n_pieces = 4 * MLP_CHUNKS
        bounds = [MOE_TILE * k // n_pieces for k in range(n_pieces + 1)]
        piece = iter(range(n_pieces))

        def issue_next():
            k = next(piece)
            issue_rows(i + 1, nxt, bounds[k], bounds[k + 1])

        out = None
        for wg_ref, wu_ref, wd_ref, lane in ((wgx_ref, wux_ref, wdx_ref, d), (wgy_ref, wuy_ref, wdy_ref, d + 1)):
            hid = []
            for c in range(MLP_CHUNKS):
                cols = slice(c * cw, (c + 1) * cw)
                issue_next()
                a = jnp.dot(xbuf[slot, :, :d].astype(BF16), wg_ref[0, :, cols], preferred_element_type=F32)
                issue_next()
                u = jnp.dot(xbuf[slot, :, :d].astype(BF16), wu_ref[0, :, cols], preferred_element_type=F32)
                hid.append((jax.nn.silu(a) * u).astype(BF16))
            y = None
            for c in range(MLP_CHUNKS):
                part = jnp.dot(hid[c], wd_ref[0, c * cw:(c + 1) * cw, :], preferred_element_type=F32)
                y = part if y is None else y + part
            y = y * xbuf[slot, :, lane:lane + 1]
            out = y if out is None else out + y
        y_ref[...] = out

    @pl.when(i >= n_used)
    def _():
        y_ref[...] = jnp.zeros_like(y_ref)

    @pl.when(i == n_used)
    def _():
        wait_tile(slot)


def _expert_mlp(h1x, src, tile_x, tile_y, n_used, wg_b, wu_b, wd_b, layer):
    dx = h1x.shape[1]
    d = dx - HEAD_DIM
    p = src.shape[0] - MOE_TILE
    de = wg_b.shape[2]
    ex = lambda i, src, tx, ty, nu: (layer * N_EXPERTS + tx[i], 0, 0)
    ey = lambda i, src, tx, ty, nu: (layer * N_EXPERTS + ty[i], 0, 0)
    return pl.pallas_call(
        _expert_mlp_kernel,
        out_shape=jax.ShapeDtypeStruct((p, d), F32),
        grid_spec=pltpu.PrefetchScalarGridSpec(
            num_scalar_prefetch=4, grid=(p // MOE_TILE,),
            in_specs=[pl.BlockSpec(memory_space=pl.ANY),
                      pl.BlockSpec((1, d, de), ex),
                      pl.BlockSpec((1, d, de), ex),
                      pl.BlockSpec((1, de, d), ex),
                      pl.BlockSpec((1, d, de), ey),
                      pl.BlockSpec((1, d, de), ey),
                      pl.BlockSpec((1, de, d), ey)],
            out_specs=pl.BlockSpec((MOE_TILE, d), lambda i, src, tx, ty, nu: (i, 0)),
            scratch_shapes=[pltpu.VMEM((2, MOE_TILE, dx), F32), pltpu.SemaphoreType.DMA((2,))]),
        compiler_params=_cparams(("arbitrary",), vmem=MLP_VMEM_LIMIT),
        name="moe_expert_mlp",
    )(src, tile_x, tile_y, n_used, h1x, wg_b, wu_b, wd_b, wg_b, wu_b, wd_b)


def _combine_ln_kernel(pos_ref, y_hbm, h_ref, g_ref, b_ref, *refs, alpha):
    out_refs, (buf, sem) = refs[:-2], refs[-2:]
    i = pl.program_id(0)
    tm = h_ref.shape[0]

    def issue_tile(tile, slot):
        def issue(c, carry):
            base = pl.multiple_of(c * DMA_UNROLL, DMA_UNROLL)
            dst = buf.at[slot, pl.ds(base, DMA_UNROLL)]
            for j in range(DMA_UNROLL):
                row = pos_ref[tile * tm + base + j]
                pltpu.make_async_copy(y_hbm.at[pl.ds(row, 1)], dst.at[pl.ds(j, 1)],
                                      sem.at[slot]).start()
            return carry

        lax.fori_loop(0, tm // DMA_UNROLL, issue, 0)

    @pl.when(i == 0)
    def _():
        issue_tile(0, 0)

    @pl.when(i + 1 < pl.num_programs(0))
    def _():
        issue_tile(i + 1, (i + 1) % 2)

    slot = i % 2
    pltpu.make_async_copy(y_hbm.at[pl.ds(0, tm)], buf.at[slot], sem.at[slot]).wait()
    out = _layer_norm_rows(alpha * h_ref[...] + buf[slot], g_ref[...], b_ref[...])
    for o_ref in out_refs:
        o_ref[...] = out.astype(o_ref.dtype)


def _combine_ln(pos, y_sorted, h1x, ln_g, ln_b, alpha, out_dtypes):
    t = h1x.shape[0]
    d = y_sorted.shape[1]
    tm = GATHER_TILE
    return pl.pallas_call(
        functools.partial(_combine_ln_kernel, alpha=alpha),
        out_shape=tuple(jax.ShapeDtypeStruct((t, d), dt) for dt in out_dtypes),
        grid_spec=pltpu.PrefetchScalarGridSpec(
            num_scalar_prefetch=1, grid=(t // tm,),
            in_specs=[pl.BlockSpec(memory_space=pl.ANY),
                      pl.BlockSpec((tm, d), lambda i, pos: (i, 0)),
                      pl.BlockSpec((1, d), lambda i, pos: (0, 0)),
                      pl.BlockSpec((1, d), lambda i, pos: (0, 0))],
            out_specs=tuple(pl.BlockSpec((tm, d), lambda i, pos: (i, 0)) for _ in out_dtypes),
            scratch_shapes=[pltpu.VMEM((2, tm, d), F32), pltpu.SemaphoreType.DMA((2,))]),
        compiler_params=_cparams(("arbitrary",)),
        name="moe_combine_ln2",
    )(pos, y_sorted, h1x, ln_g.reshape(1, d), ln_b.reshape(1, d))


def _moe(h1x, rt, wg_b, wu_b, wd_b, layer, ln_g, ln_b, alpha, out_dtypes):
    t = h1x.shape[0]
    src, pos, tile_x, tile_y, n_used = _dispatch_plan(rt, t)
    y_sorted = _expert_mlp(h1x, src, tile_x, tile_y, n_used, wg_b, wu_b, wd_b, layer)
    return _combine_ln(pos, y_sorted, h1x, ln_g, ln_b, alpha, out_dtypes)


def kernel(x, positions, w_in, w_gate, b_gate, w_branch_a, w_branch_b, w_branch_c, w_out, sinks,
           ln1_g, ln1_b, w_router_group, w_router_expert, w_exp_gate, w_exp_up, w_exp_down,
           ln2_g, ln2_b):
    batch, seq, d = x.shape
    depth = w_in.shape[0]
    t = batch * seq
    alpha = (2 * depth) ** 0.25
    hf = x.reshape(t, d)
    cc, ss, hb = _rope_tables(positions, hf)
    wa_b, wb_b, wc_b = w_branch_a.astype(BF16), w_branch_b.astype(BF16), w_branch_c.astype(BF16)
    w_in_b = w_in[0].astype(BF16)
    w_in2, w_gate2, w_out2 = (w.reshape(depth * d, w.shape[-1]) for w in (w_in, w_gate, w_out))
    de = w_exp_gate.shape[-1]
    w_exp = ((w_exp_gate.reshape(depth * N_EXPERTS * d, de), N_EXPERTS * d),
             (w_exp_up.reshape(depth * N_EXPERTS * d, de), N_EXPERTS * d),
             (w_exp_down.reshape(depth * N_EXPERTS * de, d), N_EXPERTS * de))
    for l in range(depth):
        last = l == depth - 1
        qkv, w_gate_b, w_out_b = _qkv_proj(hb, w_in_b, cc, ss, sides=((w_gate2, d, l), (w_out2, d, l)))
        out_a = _dilated_attention(qkv, batch, seq)
        out_b = _moba_attention(qkv, batch, seq)
        out_c = _swa_attention(qkv, sinks[l], batch, seq)
        sides = [(w, rows, l) for w, rows in w_exp] + ([] if last else [(w_in2, d, l + 1)])
        merged, wg_b, wu_b, wd_b, *nxt = _merge(hb, out_a, out_b, out_c, w_gate_b, b_gate, wa_b, wb_b, wc_b,
                                                l, sides)
        if not last:
            w_in_b = nxt[0]
        wg_b, wu_b = wg_b.reshape(N_EXPERTS, d, de), wu_b.reshape(N_EXPERTS, d, de)
        wd_b = wd_b.reshape(N_EXPERTS, de, d)
        w_router = jnp.zeros((d, HEAD_DIM), F32)
        w_router = w_router.at[:, 0:N_GROUPS].set(w_router_group[l])
        w_router = w_router.at[:, ROUTER_EXPERT_COL:ROUTER_EXPERT_COL + N_EXPERTS].set(w_router_expert[l])
        h1f, rt = _outproj_ln_route(merged, w_out_b, hf, ln1_g[l], ln1_b[l], w_router, alpha)
        outs = _moe(h1f, rt, wg_b, wu_b, wd_b, 0, ln2_g[l], ln2_b[l], alpha,
                    (F32,) if last else (F32, BF16))
        hf, hb = (outs[0], None) if last else outs
    return hf.reshape(batch, seq, d)
```

```python
import functools

import jax
import jax.numpy as jnp
from jax import lax
from jax.experimental import pallas as pl
from jax.experimental.pallas import tpu as pltpu

F32 = jnp.float32
BF16 = jnp.bfloat16

HEAD_DIM = 128
ATTN_SCALE = HEAD_DIM ** -0.5
ROPE_THETA = 10000.0
A_PATTERNS = ((128, 1), (512, 4), (2048, 16))
A_HEADS_PER_GROUP = 2
A_HEADS = A_HEADS_PER_GROUP * len(A_PATTERNS)
B_HEADS = 6
MOBA_BLOCK = 256
MOBA_TOPK = 3
C_Q_HEADS = 8
C_KV_HEADS = 2
C_GROUP = C_Q_HEADS // C_KV_HEADS
C_WINDOW = 128
BAND = 128
N_Q_HEADS = A_HEADS + B_HEADS + C_Q_HEADS
N_KV_HEADS = A_HEADS + B_HEADS + C_KV_HEADS
N_GROUPS = 4
EXPERTS_PER_GROUP = 4
N_EXPERTS = N_GROUPS * EXPERTS_PER_GROUP
PAIRS_PER_GROUP = EXPERTS_PER_GROUP * (EXPERTS_PER_GROUP - 1) // 2
N_CLASSES = N_GROUPS * PAIRS_PER_GROUP
PAIR_X = (0, 0, 0, 1, 1, 3)
PAIR_Y = (1, 2, 3, 3, 2, 2)
ROUTER_EXPERT_COL = 8
LN_EPS = 1e-5
NEG = -1e30

V7X_SUBLANES = 8
V7X_VMEM_BYTES = 64 * 1024 * 1024
VMEM_LIMIT = V7X_VMEM_BYTES - 8 * 1024 * 1024
MLP_VMEM_LIMIT = V7X_VMEM_BYTES - 2 * 1024 * 1024


def _cparams(sem, vmem=VMEM_LIMIT):
    return pltpu.CompilerParams(dimension_semantics=sem, vmem_limit_bytes=vmem)


ROPE_SLOT_Q, ROPE_SLOT_K, ROPE_SLOT_V, ROPE_SLOT_Q_BASE2 = 0, 1, 2, 3
ROPE_SLOTS = 4
LOG2_E = 1.4426950408889634


def _rope_table_kernel(pos_ref, inv_ref, x_ref, cc_ref, ss_ref, xb_ref):
    xb_ref[...] = x_ref[...].astype(BF16)
    ang = pos_ref[...].astype(F32) * inv_ref[...]
    lane = lax.broadcasted_iota(jnp.int32, ang.shape, 1)
    s = jnp.sin(ang)
    c = jnp.cos(ang)
    s = jnp.where(lane < HEAD_DIM // 2, -s, s)
    cc_ref[ROPE_SLOT_Q] = c * ATTN_SCALE
    ss_ref[ROPE_SLOT_Q] = s * ATTN_SCALE
    cc_ref[ROPE_SLOT_K] = c
    ss_ref[ROPE_SLOT_K] = s
    cc_ref[ROPE_SLOT_V] = jnp.ones_like(c)
    ss_ref[ROPE_SLOT_V] = jnp.zeros_like(s)
    cc_ref[ROPE_SLOT_Q_BASE2] = c * (ATTN_SCALE * LOG2_E)
    ss_ref[ROPE_SLOT_Q_BASE2] = s * (ATTN_SCALE * LOG2_E)


def _rope_tables(positions, x):
    t, d = x.shape
    tm = 512
    inv = ROPE_THETA ** (-jnp.arange(0, HEAD_DIM, 2, dtype=F32) / HEAD_DIM)
    inv2 = jnp.concatenate([inv, inv]).reshape(1, HEAD_DIM)
    pos = positions.reshape(t, 1)
    table = jax.ShapeDtypeStruct((ROPE_SLOTS, t, HEAD_DIM), F32)
    table_spec = pl.BlockSpec((ROPE_SLOTS, tm, HEAD_DIM), lambda i: (0, i, 0))
    return pl.pallas_call(
        _rope_table_kernel,
        out_shape=(table, table, jax.ShapeDtypeStruct((t, d), BF16)),
        grid=(t // tm,),
        in_specs=[pl.BlockSpec((tm, 1), lambda i: (i, 0)),
                  pl.BlockSpec((1, HEAD_DIM), lambda i: (0, 0)),
                  pl.BlockSpec((tm, d), lambda i: (i, 0))],
        out_specs=(table_spec, table_spec, pl.BlockSpec((tm, d), lambda i: (i, 0))),
        compiler_params=_cparams(("arbitrary",)),
        name="rope_tables",
    )(pos, inv2, x)


def _side_cast_specs(sides, n_steps, step_of):
    ins, outs, shapes = [], [], []
    for w, rows, layer in sides:
        assert rows % n_steps == 0, (rows, n_steps)
        sr = rows // n_steps
        ins.append(pl.BlockSpec((sr, w.shape[1]), lambda *g, layer=layer: (layer * n_steps + step_of(*g), 0)))
        outs.append(pl.BlockSpec((sr, w.shape[1]), lambda *g: (step_of(*g), 0)))
        shapes.append(jax.ShapeDtypeStruct((rows, w.shape[1]), BF16))
    return ins, outs, shapes


def _cast_sides(in_refs, out_refs):
    for i_ref, o_ref in zip(in_refs, out_refs, strict=True):
        o_ref[...] = i_ref[...].astype(BF16)


def _qkv_kernel(x_ref, w_ref, cc_ref, ss_ref, *refs, heads_per_tile):
    n_side = len(refs) // 2
    o_ref = refs[n_side]
    _cast_sides(refs[:n_side], refs[n_side + 1:])
    j = pl.program_id(1)
    acc = jnp.dot(x_ref[...], w_ref[...], preferred_element_type=F32)
    for hh in range(heads_per_tile):
        head = j * heads_per_tile + hh
        in_b = (head >= A_HEADS) & (head < A_HEADS + B_HEADS)
        slot = ((head >= N_Q_HEADS).astype(jnp.int32) + (head >= N_Q_HEADS + N_KV_HEADS).astype(jnp.int32)
                + ROPE_SLOT_Q_BASE2 * in_b.astype(jnp.int32))
        t = acc[:, hh * HEAD_DIM:(hh + 1) * HEAD_DIM]
        r = t * cc_ref[slot] + pltpu.roll(t, HEAD_DIM // 2, 1) * ss_ref[slot]
        o_ref[hh] = r.astype(BF16)


def _qkv_proj(hb, w_in_b, cc, ss, sides=()):
    t, d = hb.shape
    n = w_in_b.shape[1]
    tm, tn = 1024, 1536
    hpt = tn // HEAD_DIM
    ni, nj = t // tm, n // tn
    side_in, side_out, side_shape = _side_cast_specs(sides, ni * nj, lambda i, j: i * nj + j)
    return pl.pallas_call(
        functools.partial(_qkv_kernel, heads_per_tile=hpt),
        out_shape=(jax.ShapeDtypeStruct((n // HEAD_DIM, t, HEAD_DIM), BF16), *side_shape),
        grid=(ni, nj),
        in_specs=[pl.BlockSpec((tm, d), lambda i, j: (i, 0)),
                  pl.BlockSpec((d, tn), lambda i, j: (0, j)),
                  pl.BlockSpec((ROPE_SLOTS, tm, HEAD_DIM), lambda i, j: (0, i, 0)),
                  pl.BlockSpec((ROPE_SLOTS, tm, HEAD_DIM), lambda i, j: (0, i, 0)), *side_in],
        out_specs=(pl.BlockSpec((hpt, tm, HEAD_DIM), lambda i, j: (j, i, 0)), *side_out),
        compiler_params=_cparams(("arbitrary", "arbitrary")),
        name="qkv_rope",
    )(hb, w_in_b, cc, ss, *[w for w, _, _ in sides])


def _band_bias(max_dist):
    qi = lax.broadcasted_iota(jnp.int32, (BAND, 2 * BAND), 0)
    kj = lax.broadcasted_iota(jnp.int32, (BAND, 2 * BAND), 1)
    dist = qi + BAND - kj
    valid = (dist >= 0) & (dist <= max_dist)
    bias = jnp.where(valid, 0.0, NEG).astype(F32)
    bias_first = jnp.where(valid & (kj >= BAND), 0.0, NEG).astype(F32)
    return bias, bias_first


def _band_block(qb, kwin, vwin, bias):
    g = qb.shape[0] // BAND
    s = lax.dot_general(qb, kwin, (((1,), (1,)), ((), ())), preferred_element_type=F32)
    s = s.reshape(g, BAND, 2 * BAND) + bias[None]
    m = jnp.max(s, axis=-1, keepdims=True)
    p = jnp.exp(s - m)
    l = jnp.sum(p, axis=-1, keepdims=True)
    o = jnp.dot(p.reshape(g * BAND, 2 * BAND).astype(BF16), vwin, preferred_element_type=F32)
    o = o.reshape(g, BAND, HEAD_DIM) / l
    return o, m + jnp.log(l)


def _swa_kernel(sink_ref, q_ref, k_ref, kh_ref, v_ref, vh_ref, o_ref, *, tq):
    kv = pl.program_id(1)
    t = pl.program_id(2)
    bias, bias_first = _band_bias(C_WINDOW - 1)
    for blk in range(tq // BAND):
        qb = q_ref[:, blk * BAND:(blk + 1) * BAND, :].reshape(C_GROUP * BAND, HEAD_DIM)
        if blk == 0:
            kwin = jnp.concatenate([kh_ref[0], k_ref[0, :BAND]], axis=0)
            vwin = jnp.concatenate([vh_ref[0], v_ref[0, :BAND]], axis=0)
            b = jnp.where(t == 0, bias_first, bias)
        else:
            kwin = k_ref[0, (blk - 1) * BAND:(blk + 1) * BAND]
            vwin = v_ref[0, (blk - 1) * BAND:(blk + 1) * BAND]
            b = bias
        o, lse = _band_block(qb, kwin, vwin, b)
        for g in range(C_GROUP):
            keep = jax.nn.sigmoid(lse[g] - sink_ref[kv * C_GROUP + g])
            o_ref[blk * BAND:(blk + 1) * BAND, g * HEAD_DIM:(g + 1) * HEAD_DIM] = (o[g] * keep).astype(BF16)


def _swa_attention(qkv, sinks, batch, seq):
    tq = 2048
    nt = seq // tq
    q0 = A_HEADS + B_HEADS
    k0 = N_Q_HEADS + A_HEADS + B_HEADS
    v0 = N_Q_HEADS + N_KV_HEADS + A_HEADS + B_HEADS
    rb = tq // BAND

    def halo(b, kv, t):
        return jnp.maximum(b * (seq // BAND) + t * rb - 1, 0)

    return pl.pallas_call(
        functools.partial(_swa_kernel, tq=tq),
        out_shape=jax.ShapeDtypeStruct((batch * seq, C_Q_HEADS * HEAD_DIM), BF16),
        grid=(batch, C_KV_HEADS, nt),
        in_specs=[
            pl.BlockSpec(memory_space=pltpu.SMEM),
            pl.BlockSpec((C_GROUP, tq, HEAD_DIM), lambda b, kv, t: (q0 // C_GROUP + kv, b * nt + t, 0)),
            pl.BlockSpec((1, tq, HEAD_DIM), lambda b, kv, t: (k0 + kv, b * nt + t, 0)),
            pl.BlockSpec((1, BAND, HEAD_DIM), lambda b, kv, t: (k0 + kv, halo(b, kv, t), 0)),
            pl.BlockSpec((1, tq, HEAD_DIM), lambda b, kv, t: (v0 + kv, b * nt + t, 0)),
            pl.BlockSpec((1, BAND, HEAD_DIM), lambda b, kv, t: (v0 + kv, halo(b, kv, t), 0)),
        ],
        out_specs=pl.BlockSpec((tq, C_GROUP * HEAD_DIM), lambda b, kv, t: (b * nt + t, kv)),
        compiler_params=_cparams(("arbitrary",) * 3),
        name="swa_sink",
    )(sinks, qkv, qkv, qkv, qkv, qkv)


def _dilated_kernel(*refs, seq):
    ng = len(A_PATTERNS)
    in_refs = refs[:3 * ng]
    o_ref = refs[3 * ng]
    qf, kf, vf, og, lg = refs[3 * ng + 1:]
    for g, (window, dil) in enumerate(A_PATTERNS):
        q_ref, k_ref, v_ref = in_refs[3 * g:3 * g + 3]
        qf[...] = q_ref[0].astype(F32)
        kf[...] = k_ref[0].astype(F32)
        vf[...] = v_ref[0].astype(F32)
        nblk = seq // dil // BAND
        bias, bias_first = _band_bias(window // dil)

        for r in range(dil):
            for n in range(nblk):
                own = pl.ds(r + dil * BAND * n, BAND, stride=dil)
                prev = pl.ds(r + dil * BAND * max(n - 1, 0), BAND, stride=dil)
                qb = qf[own, :].astype(BF16)
                kwin = jnp.concatenate([kf[prev, :], kf[own, :]], axis=0).astype(BF16)
                vwin = jnp.concatenate([vf[prev, :], vf[own, :]], axis=0).astype(BF16)
                o, lse = _band_block(qb, kwin, vwin, bias_first if n == 0 else bias)
                og[g, own, :] = o[0]
                lg[g, own, :] = jnp.broadcast_to(lse[0], (BAND, HEAD_DIM))

    rows = 256

    def combine(c, carry):
        sl = pl.ds(pl.multiple_of(c * rows, rows), rows)
        ls = [lg[g, sl, :] for g in range(ng)]
        m = functools.reduce(jnp.maximum, ls)
        es = [jnp.exp(l - m) for l in ls]
        num = functools.reduce(jnp.add, [e * og[g, sl, :] for g, e in enumerate(es)])
        o_ref[sl, :] = (num / functools.reduce(jnp.add, es)).astype(BF16)
        return carry

    lax.fori_loop(0, seq // rows, combine, 0)


def _dilated_attention(qkv, batch, seq):
    ng = len(A_PATTERNS)
    k0 = N_Q_HEADS
    v0 = N_Q_HEADS + N_KV_HEADS
    in_specs = []
    for g in range(ng):
        for base in (0, k0, v0):
            in_specs.append(pl.BlockSpec(
                (1, seq, HEAD_DIM),
                lambda b, j, h=base + g * A_HEADS_PER_GROUP: (h + j, b, 0)))
    return pl.pallas_call(
        functools.partial(_dilated_kernel, seq=seq),
        out_shape=jax.ShapeDtypeStruct((batch * seq, A_HEADS_PER_GROUP * HEAD_DIM), BF16),
        grid=(batch, A_HEADS_PER_GROUP),
        in_specs=in_specs,
        out_specs=pl.BlockSpec((seq, HEAD_DIM), lambda b, j: (b, j)),
        scratch_shapes=[pltpu.VMEM((seq, HEAD_DIM), F32)] * 3
        + [pltpu.VMEM((ng, seq, HEAD_DIM), F32)] * 2,
        compiler_params=_cparams(("arbitrary",) * 2),
        name="dilated_attn",
    )(*([qkv] * (3 * ng)))


MOBA_TQ = 1024
MOBA_HEADS = 2


def _online_update(state, s, v):
    mc = jnp.max(s, axis=1, keepdims=True)
    if state is None:
        p = jnp.exp2(s - mc)
        return mc, jnp.sum(p, axis=1, keepdims=True), jnp.dot(p.astype(BF16), v, preferred_element_type=F32)
    m, l, acc = state
    m_new = jnp.maximum(m, mc)
    alpha = jnp.exp2(m - m_new)
    p = jnp.exp2(s - m_new)
    l_new = alpha * l + jnp.sum(p, axis=1, keepdims=True)
    acc_new = alpha * acc + jnp.dot(p.astype(BF16), v, preferred_element_type=F32)
    return m_new, l_new, acc_new


def _moba_kernel(q_ref, k_ref, v_ref, o_ref, kaug, kmean, *, seq):
    t = pl.program_id(2)
    nblk = seq // MOBA_BLOCK
    blk = MOBA_BLOCK
    tq = MOBA_TQ
    nb = tq // blk
    nh = MOBA_HEADS

    @pl.when(t == 0)
    def _():
        row = lax.broadcasted_iota(jnp.int32, (seq, HEAD_DIM), 0)
        lane = lax.broadcasted_iota(jnp.int32, (seq, HEAD_DIM), 1)
        onehot = jnp.where(row // blk == lane, 1.0, 0.0).astype(BF16)
        for h in range(nh):
            kaug[h, :, :HEAD_DIM] = k_ref[h]
            kaug[h, :, HEAD_DIM:] = onehot
            kmean[h] = jnp.zeros((HEAD_DIM, HEAD_DIM), F32)
            for n in range(nblk):
                kb = k_ref[h, n * blk:(n + 1) * blk, :].astype(F32)
                kmean[h, n:n + 1, :] = jnp.mean(kb, axis=0, keepdims=True)

    nt = (((1,), (1,)), ((), ()))
    nrow = -(-nblk // V7X_SUBLANES) * V7X_SUBLANES
    row = lax.broadcasted_iota(jnp.int32, (nrow, tq), 0).astype(F32)
    own = (t * nb + lax.broadcasted_iota(jnp.int32, (nrow, tq), 1) // blk).astype(F32)
    qaugs = []
    for h in range(nh):
        q = q_ref[h]
        km = kmean[h]
        km_hi = km.astype(BF16)
        km_lo = (km - km_hi.astype(F32)).astype(BF16)
        gate = (lax.dot_general(km_hi, q, nt, preferred_element_type=F32)
                + lax.dot_general(km_lo, q, nt, preferred_element_type=F32))
        g = jnp.where(row < own, gate[:nrow], NEG)
        sel = jnp.zeros((nrow, tq), jnp.bool_)
        for _ in range(MOBA_TOPK):
            mx = jnp.max(g, axis=0, keepdims=True)
            first = jnp.min(jnp.where(g == mx, row, float(nrow)), axis=0, keepdims=True)
            pick = row == first
            sel = sel | pick
            g = jnp.where(pick, NEG, g)
        visible = (sel & (row < own)) | (row == own)
        bias_t = jnp.concatenate([jnp.where(visible, 0.0, NEG), jnp.zeros((HEAD_DIM - nrow, tq), F32)], axis=0)
        qaugs.append(jnp.concatenate([q, bias_t.T.astype(BF16)], axis=1))

    ri = lax.broadcasted_iota(jnp.int32, (blk, blk), 0)
    ci = lax.broadcasted_iota(jnp.int32, (blk, blk), 1)
    causal = ri >= ci

    for c in range(seq // tq):
        @pl.when(t == c)
        def _(c=c):
            states = [None] * nh
            for cc in range(c):
                keys = slice(cc * tq, (cc + 1) * tq)
                for h in range(nh):
                    s = lax.dot_general(qaugs[h], kaug[h, keys, :], nt, preferred_element_type=F32)
                    states[h] = _online_update(states[h], s, v_ref[h, keys, :])
            n_split = 2 if c == 0 else 1
            hb = nb // n_split
            for h in range(nh):
                parts = []
                for p in range(n_split):
                    rows = slice(p * hb * blk, (p + 1) * hb * blk)
                    keys = slice(c * tq, c * tq + (p + 1) * hb * blk)
                    s = lax.dot_general(qaugs[h][rows], kaug[h, keys, :], nt, preferred_element_type=F32)
                    s = jnp.concatenate(
                        [jnp.concatenate(
                            [jnp.where(causal, s[a * blk:(a + 1) * blk, e * blk:(e + 1) * blk], NEG)
                             if e == a + p * hb else s[a * blk:(a + 1) * blk, e * blk:(e + 1) * blk]
                             for e in range((p + 1) * hb)], axis=1) for a in range(hb)], axis=0)
                    st = None if states[h] is None else tuple(x[rows] for x in states[h])
                    parts.append(_online_update(st, s, v_ref[h, keys, :]))
                m, l, acc = (jnp.concatenate([pp[i] for pp in parts], axis=0) for i in range(3))
                o_ref[:, h * HEAD_DIM:(h + 1) * HEAD_DIM] = (acc / l).astype(BF16)


def _moba_attention(qkv, batch, seq):
    nt = seq // MOBA_TQ
    nh = MOBA_HEADS
    q0 = A_HEADS
    k0 = N_Q_HEADS + A_HEADS
    v0 = N_Q_HEADS + N_KV_HEADS + A_HEADS
    return pl.pallas_call(
        functools.partial(_moba_kernel, seq=seq),
        out_shape=jax.ShapeDtypeStruct((batch * seq, B_HEADS * HEAD_DIM), BF16),
        grid=(batch, B_HEADS // nh, nt),
        in_specs=[
            pl.BlockSpec((nh, MOBA_TQ, HEAD_DIM), lambda b, h, t: (q0 // nh + h, b * nt + t, 0)),
            pl.BlockSpec((nh, seq, HEAD_DIM), lambda b, h, t: (k0 // nh + h, b, 0)),
            pl.BlockSpec((nh, seq, HEAD_DIM), lambda b, h, t: (v0 // nh + h, b, 0)),
        ],
        out_specs=pl.BlockSpec((MOBA_TQ, nh * HEAD_DIM), lambda b, h, t: (b * nt + t, h)),
        scratch_shapes=[pltpu.VMEM((nh, seq, 2 * HEAD_DIM), BF16), pltpu.VMEM((nh, HEAD_DIM, HEAD_DIM), F32)],
        compiler_params=_cparams(("arbitrary",) * 3),
        name="moba_attn",
    )(qkv, qkv, qkv)


def _merge_kernel(h_ref, oa_ref, ob_ref, oc_ref, g0_ref, g1_ref, g2_ref, b0_ref, b1_ref, b2_ref,
                  wa_ref, wb_ref, wc_ref, *refs):
    n_side = len(refs) // 2
    o_ref = refs[n_side]
    _cast_sides(refs[:n_side], refs[n_side + 1:])
    x = h_ref[...]
    acc = None
    for g_ref, b_ref, br_ref, w_ref in ((g0_ref, b0_ref, oa_ref, wa_ref),
                                        (g1_ref, b1_ref, ob_ref, wb_ref),
                                        (g2_ref, b2_ref, oc_ref, wc_ref)):
        gate = jax.nn.sigmoid(jnp.dot(x, g_ref[...], preferred_element_type=F32) + b_ref[...])
        term = gate * jnp.dot(br_ref[...], w_ref[...], preferred_element_type=F32)
        acc = term if acc is None else acc + term
    o_ref[...] = acc.astype(BF16)


def _merge(hb, out_a, out_b, out_c, w_gate_b, b_gate, wa_b, wb_b, wc_b, layer, sides=()):
    t, d = hb.shape
    tm, tn = 1024, 512
    nj = d // tn
    ni = t // tm
    side_in, side_out, side_shape = _side_cast_specs(sides, nj * ni, lambda j, i: j * ni + i)
    row = lambda w: pl.BlockSpec((tm, w), lambda j, i: (i, 0))
    gate_w = lambda x: pl.BlockSpec((d, tn), lambda j, i, x=x: (0, x * nj + j))
    gate_b = lambda x: pl.BlockSpec((None, 1, tn), lambda j, i, x=x: (layer, 0, x * nj + j))
    br_w = lambda k: pl.BlockSpec((None, k, tn), lambda j, i: (layer, 0, j))
    bg = b_gate.reshape(b_gate.shape[0], 1, -1)
    return pl.pallas_call(
        _merge_kernel,
        out_shape=(jax.ShapeDtypeStruct((t, d), BF16), *side_shape),
        grid=(nj, ni),
        in_specs=[row(d), row(out_a.shape[1]), row(out_b.shape[1]), row(out_c.shape[1]),
                  gate_w(0), gate_w(1), gate_w(2), gate_b(0), gate_b(1), gate_b(2),
                  br_w(wa_b.shape[1]), br_w(wb_b.shape[1]), br_w(wc_b.shape[1]), *side_in],
        out_specs=(pl.BlockSpec((tm, tn), lambda j, i: (i, j)), *side_out),
        compiler_params=_cparams(("arbitrary",) * 2),
        name="branch_merge",
    )(hb, out_a, out_b, out_c, w_gate_b, w_gate_b, w_gate_b, bg, bg, bg, wa_b, wb_b, wc_b,
      *[w for w, _, _ in sides])


def _layer_norm_rows(y, g, b):
    mu = jnp.mean(y, axis=-1, keepdims=True)
    yc = y - mu
    var = jnp.mean(yc * yc, axis=-1, keepdims=True)
    return yc * lax.rsqrt(var + LN_EPS) * g + b


def _route(lt):
    tm = lt.shape[1]
    gl = lt[0:N_GROUPS]
    gmax = jnp.max(gl, axis=0, keepdims=True)
    ge = jnp.exp(gl - gmax)
    gp = ge / jnp.sum(ge, axis=0, keepdims=True)
    g_val = jnp.max(gp, axis=0, keepdims=True)
    row = lax.broadcasted_iota(jnp.int32, (N_GROUPS, tm), 0)
    g_sel = jnp.min(jnp.where(gp == g_val, row, N_GROUPS), axis=0, keepdims=True)
    el = jnp.zeros((EXPERTS_PER_GROUP, tm), F32)
    for g in range(N_GROUPS):
        lo = ROUTER_EXPERT_COL + g * EXPERTS_PER_GROUP
        el = jnp.where(g_sel == g, lt[lo:lo + EXPERTS_PER_GROUP], el)
    v0 = jnp.max(el, axis=0, keepdims=True)
    i0 = jnp.min(jnp.where(el == v0, row, EXPERTS_PER_GROUP), axis=0, keepdims=True)
    el1 = jnp.where(row == i0, -jnp.inf, el)
    v1 = jnp.max(el1, axis=0, keepdims=True)
    i1 = jnp.min(jnp.where(el1 == v1, row, EXPERTS_PER_GROUP), axis=0, keepdims=True)
    e1 = jnp.exp(v1 - v0)
    den = 1.0 + e1
    w0 = 1.0 / den * g_val
    w1 = e1 / den * g_val
    a = jnp.minimum(i0, i1)
    b = jnp.maximum(i0, i1)
    swap = (a == 2) & (b == 3)
    x = jnp.where(swap, b, a)
    y = jnp.where(swap, a, b)
    rank = jnp.where(a == 0, b - 1, jnp.where(a == 1, jnp.where(b == 3, 3, 4), 5))
    cls = (g_sel * PAIRS_PER_GROUP + rank).astype(F32)
    wx = jnp.where(i0 == x, w0, w1)
    wy = jnp.where(i0 == y, w0, w1)
    return cls, wx, wy


def _outproj_ln_route_kernel(m_ref, w_ref, h_ref, g_ref, b_ref, wr_ref, hf_ref, rt_ref, mix_buf,
                             wrs_ref, *, alpha):
    @pl.when(pl.program_id(0) == 0)
    def _():
        mix_buf[...] = jnp.zeros_like(mix_buf)
        wr = wr_ref[...]
        wr_hi = wr.astype(BF16)
        wrs_ref[:, :HEAD_DIM] = wr_hi
        wrs_ref[:, HEAD_DIM:] = (wr - wr_hi.astype(F32)).astype(BF16)

    prev = mix_buf[...]
    mix_buf[...] = jnp.dot(m_ref[...], w_ref[...], preferred_element_type=F32)
    h1 = _layer_norm_rows(alpha * h_ref[...] + prev, g_ref[...], b_ref[...])
    h_hi = h1.astype(BF16)
    h_lo = (h1 - h_hi.astype(F32)).astype(BF16)
    both = jnp.dot(h_hi, wrs_ref[...], preferred_element_type=F32)
    logits = (both[:, :HEAD_DIM] + both[:, HEAD_DIM:]
              + jnp.dot(h_lo, wrs_ref[:, :HEAD_DIM], preferred_element_type=F32))
    cls, wx, wy = _route(logits.T)
    tm = logits.shape[0]
    rt_ref[...] = jnp.concatenate([cls, wx, wy, jnp.zeros((5, tm), F32)], axis=0)
    ext = jnp.concatenate([wx, wy, jnp.zeros((HEAD_DIM - 2, tm), F32)], axis=0).T
    d = h1.shape[1]
    hf_ref[:, :d] = h1
    hf_ref[:, d:] = ext


def _outproj_ln_route(merged, w_out_b, h, ln_g, ln_b, w_router, alpha):
    t, d = h.shape
    tm = 512
    n = t // tm
    cur = lambda i: (jnp.minimum(i, n - 1), 0)
    prv = lambda i: (jnp.maximum(i - 1, 0), 0)
    return pl.pallas_call(
        functools.partial(_outproj_ln_route_kernel, alpha=alpha),
        out_shape=(jax.ShapeDtypeStruct((t, d + HEAD_DIM), F32), jax.ShapeDtypeStruct((8, t), F32)),
        grid=(n + 1,),
        in_specs=[pl.BlockSpec((tm, d), cur),
                  pl.BlockSpec((d, d), lambda i: (0, 0)),
                  pl.BlockSpec((tm, d), prv),
                  pl.BlockSpec((1, d), lambda i: (0, 0)),
                  pl.BlockSpec((1, d), lambda i: (0, 0)),
                  pl.BlockSpec((d, HEAD_DIM), lambda i: (0, 0))],
        out_specs=(pl.BlockSpec((tm, d + HEAD_DIM), prv),
                   pl.BlockSpec((8, tm), lambda i: (0, jnp.maximum(i - 1, 0)))),
        scratch_shapes=[pltpu.VMEM((tm, d), F32), pltpu.VMEM((d, 2 * HEAD_DIM), BF16)],
        compiler_params=_cparams(("arbitrary",)),
        name="outproj_ln1_route",
    )(merged, w_out_b, h, ln_g.reshape(1, d), ln_b.reshape(1, d), w_router)


MOE_TILE = 256
GATHER_TILE = 512
DMA_UNROLL = 8
MLP_CHUNKS = 1


def _dispatch_plan(rt, t):
    cls = rt[0].astype(jnp.int32)
    onehot = (cls[:, None] == jnp.arange(N_CLASSES, dtype=jnp.int32)[None, :]).astype(jnp.int32)
    csum = jnp.cumsum(onehot, axis=0)
    counts = csum[-1]
    rank = jnp.take_along_axis(csum, cls[:, None], axis=1)[:, 0] - 1
    padded = (counts + MOE_TILE - 1) // MOE_TILE * MOE_TILE
    ends = jnp.cumsum(padded)
    pos = ((ends - padded)[cls] + rank).astype(jnp.int32)
    p_rows = t + N_CLASSES * MOE_TILE
    tok = jnp.arange(t, dtype=jnp.int32)
    src = jnp.zeros((p_rows + MOE_TILE,), jnp.int32).at[pos].set(tok)
    n_tiles = p_rows // MOE_TILE
    n_used = (ends[-1] // MOE_TILE).astype(jnp.int32)
    tile_idx = jnp.arange(n_tiles, dtype=jnp.int32)
    tile_start = jnp.minimum(tile_idx, n_used - 1) * MOE_TILE
    tile_c = jnp.sum((ends[None, :] <= tile_start[:, None]).astype(jnp.int32), axis=1)
    tile_c = jnp.minimum(tile_c, N_CLASSES - 1)
    group = tile_c // PAIRS_PER_GROUP
    pair = tile_c % PAIRS_PER_GROUP
    tile_x = group * EXPERTS_PER_GROUP + jnp.asarray(PAIR_X, jnp.int32)[pair]
    tile_y = group * EXPERTS_PER_GROUP + jnp.asarray(PAIR_Y, jnp.int32)[pair]
    return src, pos, tile_x.astype(jnp.int32), tile_y.astype(jnp.int32), n_used.reshape(1)


def _expert_mlp_kernel(src_ref, tx_ref, ty_ref, nu_ref, h_hbm, wgx_ref, wux_ref, wdx_ref,
                       wgy_ref, wuy_ref, wdy_ref, y_ref, xbuf, sem):
    i = pl.program_id(0)
    n_used = nu_ref[0]
    slot = i % 2
    d = y_ref.shape[1]

    def issue_rows(tile, dst_slot, lo, hi):
        for j in range(lo, hi):
            row = src_ref[tile * MOE_TILE + j]
            pltpu.make_async_copy(h_hbm.at[pl.ds(row, 1)], xbuf.at[dst_slot, pl.ds(j, 1)],
                                  sem.at[dst_slot]).start()

    def wait_tile(s):
        pltpu.make_async_copy(h_hbm.at[pl.ds(0, MOE_TILE)], xbuf.at[s], sem.at[s]).wait()

    @pl.when(i == 0)
    def _():
        def first(c, carry):
            base = pl.multiple_of(c * DMA_UNROLL, DMA_UNROLL)
            for j in range(DMA_UNROLL):
                row = src_ref[base + j]
                pltpu.make_async_copy(h_hbm.at[pl.ds(row, 1)], xbuf.at[0, pl.ds(base + j, 1)],
                                      sem.at[0]).start()
            return carry
        lax.fori_loop(0, MOE_TILE // DMA_UNROLL, first, 0)

    @pl.when(i < n_used)
    def _():
        wait_tile(slot)
        nxt = 1 - slot
        de = wgx_ref.shape[2]
        cw = de // MLP_CHUNKS
        n_pieces = 4 * MLP_CHUNKS
        bounds = [MOE_TILE * k // n_pieces for k in range(n_pieces + 1)]
        piece = iter(range(n_pieces))

        def issue_next():
            k = next(piece)
            issue_rows(i + 1, nxt, bounds[k], bounds[k + 1])

        out = None
        for wg_ref, wu_ref, wd_ref, lane in ((wgx_ref, wux_ref, wdx_ref, d), (wgy_ref, wuy_ref, wdy_ref, d + 1)):
            hid = []
            for c in range(MLP_CHUNKS):
                cols = slice(c * cw, (c + 1) * cw)
                issue_next()
                a = jnp.dot(xbuf[slot, :, :d].astype(BF16), wg_ref[0, :, cols], preferred_element_type=F32)
                issue_next()
                u = jnp.dot(xbuf[slot, :, :d].astype(BF16), wu_ref[0, :, cols], preferred_element_type=F32)
                hid.append((jax.nn.silu(a) * u).astype(BF16))
            y = None
            for c in range(MLP_CHUNKS):
                part = jnp.dot(hid[c], wd_ref[0, c * cw:(c + 1) * cw, :], preferred_element_type=F32)
                y = part if y is None else y + part
            y = y * xbuf[slot, :, lane:lane + 1]
            out = y if out is None else out + y
        y_ref[...] = out

    @pl.when(i >= n_used)
    def _():
        y_ref[...] = jnp.zeros_like(y_ref)

    @pl.when(i == n_used)
    def _():
        wait_tile(slot)


def _expert_mlp(h1x, src, tile_x, tile_y, n_used, wg_b, wu_b, wd_b, layer):
    dx = h1x.shape[1]
    d = dx - HEAD_DIM
    p = src.shape[0] - MOE_TILE
    de = wg_b.shape[2]
    ex = lambda i, src, tx, ty, nu: (layer * N_EXPERTS + tx[i], 0, 0)
    ey = lambda i, src, tx, ty, nu: (layer * N_EXPERTS + ty[i], 0, 0)
    return pl.pallas_call(
        _expert_mlp_kernel,
        out_shape=jax.ShapeDtypeStruct((p, d), F32),
        grid_spec=pltpu.PrefetchScalarGridSpec(
            num_scalar_prefetch=4, grid=(p // MOE_TILE,),
            in_specs=[pl.BlockSpec(memory_space=pl.ANY),
                      pl.BlockSpec((1, d, de), ex),
                      pl.BlockSpec((1, d, de), ex),
                      pl.BlockSpec((1, de, d), ex),
                      pl.BlockSpec((1, d, de), ey),
                      pl.BlockSpec((1, d, de), ey),
                      pl.BlockSpec((1, de, d), ey)],
            out_specs=pl.BlockSpec((MOE_TILE, d), lambda i, src, tx, ty, nu: (i, 0)),
            scratch_shapes=[pltpu.VMEM((2, MOE_TILE, dx), F32), pltpu.SemaphoreType.DMA((2,))]),
        compiler_params=_cparams(("arbitrary",), vmem=MLP_VMEM_LIMIT),
        name="moe_expert_mlp",
    )(src, tile_x, tile_y, n_used, h1x, wg_b, wu_b, wd_b, wg_b, wu_b, wd_b)


def _combine_ln_kernel(pos_ref, y_hbm, h_ref, g_ref, b_ref, *refs, alpha):
    out_refs, (buf, sem) = refs[:-2], refs[-2:]
    i = pl.program_id(0)
    tm = h_ref.shape[0]

    def issue_tile(tile, slot):
        def issue(c, carry):
            base = pl.multiple_of(c * DMA_UNROLL, DMA_UNROLL)
            dst = buf.at[slot, pl.ds(base, DMA_UNROLL)]
            for j in range(DMA_UNROLL):
                row = pos_ref[tile * tm + base + j]
                pltpu.make_async_copy(y_hbm.at[pl.ds(row, 1)], dst.at[pl.ds(j, 1)],
                                      sem.at[slot]).start()
            return carry

        lax.fori_loop(0, tm // DMA_UNROLL, issue, 0)

    @pl.when(i == 0)
    def _():
        issue_tile(0, 0)

    @pl.when(i + 1 < pl.num_programs(0))
    def _():
        issue_tile(i + 1, (i + 1) % 2)

    slot = i % 2
    pltpu.make_async_copy(y_hbm.at[pl.ds(0, tm)], buf.at[slot], sem.at[slot]).wait()
    out = _layer_norm_rows(alpha * h_ref[...] + buf[slot], g_ref[...], b_ref[...])
    for o_ref in out_refs:
        o_ref[...] = out.astype(o_ref.dtype)


def _combine_ln(pos, y_sorted, h1x, ln_g, ln_b, alpha, out_dtypes):
    t = h1x.shape[0]
    d = y_sorted.shape[1]
    tm = GATHER_TILE
    return pl.pallas_call(
        functools.partial(_combine_ln_kernel, alpha=alpha),
        out_shape=tuple(jax.ShapeDtypeStruct((t, d), dt) for dt in out_dtypes),
        grid_spec=pltpu.PrefetchScalarGridSpec(
            num_scalar_prefetch=1, grid=(t // tm,),
            in_specs=[pl.BlockSpec(memory_space=pl.ANY),
                      pl.BlockSpec((tm, d), lambda i, pos: (i, 0)),
                      pl.BlockSpec((1, d), lambda i, pos: (0, 0)),
                      pl.BlockSpec((1, d), lambda i, pos: (0, 0))],
            out_specs=tuple(pl.BlockSpec((tm, d), lambda i, pos: (i, 0)) for _ in out_dtypes),
            scratch_shapes=[pltpu.VMEM((2, tm, d), F32), pltpu.SemaphoreType.DMA((2,))]),
        compiler_params=_cparams(("arbitrary",)),
        name="moe_combine_ln2",
    )(pos, y_sorted, h1x, ln_g.reshape(1, d), ln_b.reshape(1, d))


def _moe(h1x, rt, wg_b, wu_b, wd_b, layer, ln_g, ln_b, alpha, out_dtypes):
    t = h1x.shape[0]
    src, pos, tile_x, tile_y, n_used = _dispatch_plan(rt, t)
    y_sorted = _expert_mlp(h1x, src, tile_x, tile_y, n_used, wg_b, wu_b, wd_b, layer)
    return _combine_ln(pos, y_sorted, h1x, ln_g, ln_b, alpha, out_dtypes)


def kernel(x, positions, w_in, w_gate, b_gate, w_branch_a, w_branch_b, w_branch_c, w_out, sinks,
           ln1_g, ln1_b, w_router_group, w_router_expert, w_exp_gate, w_exp_up, w_exp_down,
           ln2_g, ln2_b):
    batch, seq, d = x.shape
    depth = w_in.shape[0]
    t = batch * seq
    alpha = (2 * depth) ** 0.25
    hf = x.reshape(t, d)
    cc, ss, hb = _rope_tables(positions, hf)
    wa_b, wb_b, wc_b = w_branch_a.astype(BF16), w_branch_b.astype(BF16), w_branch_c.astype(BF16)
    w_in_b = w_in[0].astype(BF16)
    w_in2, w_gate2, w_out2 = (w.reshape(depth * d, w.shape[-1]) for w in (w_in, w_gate, w_out))
    de = w_exp_gate.shape[-1]
    w_exp = ((w_exp_gate.reshape(depth * N_EXPERTS * d, de), N_EXPERTS * d),
             (w_exp_up.reshape(depth * N_EXPERTS * d, de), N_EXPERTS * d),
             (w_exp_down.reshape(depth * N_EXPERTS * de, d), N_EXPERTS * de))
    for l in range(depth):
        last = l == depth - 1
        qkv, w_gate_b, w_out_b = _qkv_proj(hb, w_in_b, cc, ss, sides=((w_gate2, d, l), (w_out2, d, l)))
        out_a = _dilated_attention(qkv, batch, seq)
        out_b = _moba_attention(qkv, batch, seq)
        out_c = _swa_attention(qkv, sinks[l], batch, seq)
        sides = [(w, rows, l) for w, rows in w_exp] + ([] if last else [(w_in2, d, l + 1)])
        merged, wg_b, wu_b, wd_b, *nxt = _merge(hb, out_a, out_b, out_c, w_gate_b, b_gate, wa_b, wb_b, wc_b,
                                                l, sides)
        if not last:
            w_in_b = nxt[0]
        wg_b, wu_b = wg_b.reshape(N_EXPERTS, d, de), wu_b.reshape(N_EXPERTS, d, de)
        wd_b = wd_b.reshape(N_EXPERTS, de, d)
        w_router = jnp.zeros((d, HEAD_DIM), F32)
        w_router = w_router.at[:, 0:N_GROUPS].set(w_router_group[l])
        w_router = w_router.at[:, ROUTER_EXPERT_COL:ROUTER_EXPERT_COL + N_EXPERTS].set(w_router_expert[l])
        h1f, rt = _outproj_ln_route(merged, w_out_b, hf, ln1_g[l], ln1_b[l], w_router, alpha)
        outs = _moe(h1f, rt, wg_b, wu_b, wd_b, 0, ln2_g[l], ln2_b[l], alpha,
                    (F32,) if last else (F32, BF16))
        hf, hb = (outs[0], None) if last else outs
    return hf.reshape(batch, seq, d)
```
